```python
import math
import jax, jax.numpy as jnp
from jax import lax
import numpy as np

D_MODEL = 2048
BATCH = 8
SEQ = 4096
DEPTH = 2

MLA_HEADS = 8
MLA_Q_RANK = 512
MLA_KV_RANK = 512
MLA_NOPE = 128
MLA_ROPE = 64
MLA_V = 128
ROPE_THETA = 10000.0
ML_HEADS = 4
ML_QK = 128
ML_V = 256
ML_CHUNK = 64
ML_F_BIAS_LO = 3.0
ML_F_BIAS_HI = 6.0
DA_HEADS = 8
DA_HEAD = 128
D_FF = 5632
Q_BLOCK = 128
EPS = 1e-6
NEG_INIT = -1e30

EVEN_SPLITS = (MLA_Q_RANK, MLA_KV_RANK, MLA_ROPE, ML_HEADS * ML_QK, ML_HEADS * ML_QK, ML_HEADS * ML_V, ML_HEADS * ML_V, 4 * ML_HEADS)
EVEN_IN = sum(EVEN_SPLITS)
EVEN_OUT = MLA_HEADS * MLA_V + ML_HEADS * ML_V
ODD_IN = 3 * DA_HEADS * 2 * DA_HEAD
ODD_OUT = DA_HEADS * 2 * DA_HEAD

kernel_name = "hybrid_mla_mlstm_diffattn_macaron_encoder"


def rmsnorm(x, g):
    xf = x.astype(jnp.float32)
    y = xf * lax.rsqrt(jnp.mean(xf * xf, axis=-1, keepdims=True) + EPS)
    return (y * g.astype(jnp.float32)).astype(x.dtype)


def swiglu(x, w_gu, w_down):
    gate, up = jnp.split(x @ w_gu, 2, axis=-1)
    return (jax.nn.silu(gate) * up) @ w_down


def rotate(x, cos, sin):
    x1, x2 = jnp.split(x, 2, axis=-1)
    return jnp.concatenate([x1 * cos - x2 * sin, x2 * cos + x1 * sin], axis=-1)


def split_cols(z, sizes):
    idx = np.cumsum(sizes)[:-1].tolist()
    return jnp.split(z, idx, axis=-1)


def alibi_slopes(n):
    return jnp.asarray([2.0 ** (-8.0 * (h + 1) / n) for h in range(n)], dtype=jnp.float32)


def to_query_blocks(t):
    b, h, s, d = t.shape
    return jnp.moveaxis(t.reshape(b, h, s // Q_BLOCK, Q_BLOCK, d), 2, 0)


def from_query_blocks(o):
    nb, b, h, q, d = o.shape
    return jnp.moveaxis(o, 0, 2).reshape(b, h, nb * q, d)


def mla_attention(q, k, v):
    q = q * (q.shape[-1] ** -0.5)

    def block(qi):
        s = jnp.einsum('bhqd,bhkd->bhqk', qi, k).astype(jnp.float32)
        p = jax.nn.softmax(s, axis=-1).astype(v.dtype)
        return jnp.einsum('bhqk,bhkd->bhqd', p, v)

    return from_query_blocks(lax.map(block, to_query_blocks(q)))


def mlstm_direction(q, k, v, log_i, log_f):
    b, h, s, _ = q.shape
    dk, dv = q.shape[-1], v.shape[-1]
    nc, L = s // ML_CHUNK, ML_CHUNK

    def to_chunks(t):
        return jnp.moveaxis(t.reshape(t.shape[:2] + (nc, L) + t.shape[3:]), 2, 0)

    lower = jnp.tril(jnp.ones((L, L), dtype=bool))

    def step(carry, inp):
        c_st, n_st, m_st = carry
        qc, kc, vc, li, lf = inp
        a = jnp.cumsum(lf, axis=-1)
        g = a[..., -1]
        dmat = jnp.where(lower, a[..., :, None] - a[..., None, :] + li[..., None, :], -jnp.inf)
        inter = a + m_st[..., None]
        m = jnp.maximum(inter, jnp.max(dmat, axis=-1))
        w_inter = jnp.exp(inter - m)
        qk = jnp.einsum('bhld,bhsd->bhls', qc, kc) * jnp.exp(dmat - m[..., None])
        num = w_inter[..., None] * jnp.einsum('bhld,bhde->bhle', qc, c_st) + jnp.einsum('bhls,bhse->bhle', qk, vc)
        den = w_inter * jnp.einsum('bhld,bhd->bhl', qc, n_st) + jnp.sum(qk, axis=-1)
        h_out = num / jnp.maximum(jnp.abs(den), jnp.exp(-m))[..., None]
        r = g[..., None] - a + li
        m_new = jnp.maximum(g + m_st, jnp.max(r, axis=-1))
        w_old = jnp.exp(g + m_st - m_new)
        w_r = jnp.exp(r - m_new[..., None])
        c_new = w_old[..., None, None] * c_st + jnp.einsum('bhs,bhsd,bhse->bhde', w_r, kc, vc)
        n_new = w_old[..., None] * n_st + jnp.einsum('bhs,bhsd->bhd', w_r, kc)
        return (c_new, n_new, m_new), h_out

    init = (jnp.zeros((b, h, dk, dv), jnp.float32), jnp.zeros((b, h, dk), jnp.float32),
            jnp.full((b, h), NEG_INIT, jnp.float32))
    _, hs = lax.scan(step, init, (to_chunks(q), to_chunks(k), to_chunks(v), to_chunks(log_i), to_chunks(log_f)))
    return jnp.moveaxis(hs, 0, 2).reshape(b, h, s, dv)


def mlstm_bidirectional(q, k, v, li_f, lf_f, li_b, lf_b):
    flip = lambda t: jnp.flip(t, axis=2)
    h_f = mlstm_direction(q, k, v, li_f, lf_f)
    h_b = flip(mlstm_direction(flip(q), flip(k), flip(v), flip(li_b), flip(lf_b)))
    return h_f + h_b


def even_mixer(h, cos, sin, w_in, g_cq, w_uq, g_ckv, w_ukv, b_gates, g_mlstm, w_o):
    b, s, _ = h.shape
    c_q, c_kv, k_r, m_q, m_k, m_v, m_o, m_g = split_cols(h @ w_in, EVEN_SPLITS)
    q = (rmsnorm(c_q, g_cq) @ w_uq).reshape(b, s, MLA_HEADS, MLA_NOPE + MLA_ROPE)
    kv = (rmsnorm(c_kv, g_ckv) @ w_ukv).reshape(b, s, MLA_HEADS, MLA_NOPE + MLA_V)
    q = jnp.concatenate([q[..., :MLA_NOPE], rotate(q[..., MLA_NOPE:], cos[:, :, None, :], sin[:, :, None, :])], axis=-1)
    k_rope = jnp.broadcast_to(rotate(k_r, cos, sin)[:, :, None, :], (b, s, MLA_HEADS, MLA_ROPE))
    k = jnp.concatenate([kv[..., :MLA_NOPE], k_rope], axis=-1)
    v = kv[..., MLA_NOPE:]
    a_out = mla_attention(q.transpose(0, 2, 1, 3), k.transpose(0, 2, 1, 3), v.transpose(0, 2, 1, 3))
    a_out = a_out.transpose(0, 2, 1, 3).reshape(b, s, MLA_HEADS * MLA_V)
    heads = lambda t, d: t.reshape(b, s, ML_HEADS, d).transpose(0, 2, 1, 3).astype(jnp.float32)
    mq = heads(m_q, ML_QK)
    mk = heads(m_k, ML_QK) * (ML_QK ** -0.5)
    mv = heads(m_v, ML_V)
    gates = (m_g + b_gates).astype(jnp.float32).reshape(b, s, 4, ML_HEADS).transpose(2, 0, 3, 1)
    hm = mlstm_bidirectional(mq, mk, mv, gates[0], jax.nn.log_sigmoid(gates[1]), gates[2], jax.nn.log_sigmoid(gates[3]))
    hm = hm * lax.rsqrt(jnp.mean(hm * hm, axis=-1, keepdims=True) + EPS) * g_mlstm.astype(jnp.float32).reshape(ML_HEADS, 1, ML_V)
    m_out = hm.transpose(0, 2, 1, 3).reshape(b, s, ML_HEADS * ML_V).astype(h.dtype) * jax.nn.sigmoid(m_o)
    return jnp.concatenate([a_out, m_out], axis=-1) @ w_o


def diff_attention(q1, q2, k1, k2, v, positions, lam):
    slopes = alibi_slopes(DA_HEADS)
    nb = positions.shape[1] // Q_BLOCK
    pb = jnp.moveaxis(positions.reshape(positions.shape[0], nb, Q_BLOCK), 1, 0)

    def block(args):
        q1i, q2i, pi = args
        dist = jnp.abs(pi[:, :, None] - positions[:, None, :]).astype(jnp.float32)
        bias = -slopes[None, :, None, None] * dist[:, None, :, :]
        s1 = jnp.einsum('bhqd,bhkd->bhqk', q1i, k1).astype(jnp.float32) + bias
        s2 = jnp.einsum('bhqd,bhkd->bhqk', q2i, k2).astype(jnp.float32) + bias
        a = jax.nn.softmax(s1, axis=-1) - lam * jax.nn.softmax(s2, axis=-1)
        return jnp.einsum('bhqk,bhkd->bhqd', a.astype(v.dtype), v)

    return from_query_blocks(lax.map(block, (to_query_blocks(q1), to_query_blocks(q2), pb)))


def odd_mixer(h, positions, w_in, lam_q1, lam_k1, lam_q2, lam_k2, g_sub, w_o, lam_init):
    b, s, _ = h.shape
    q, k, v = jnp.split(h @ w_in, 3, axis=-1)
    q = q.reshape(b, s, DA_HEADS, 2, DA_HEAD) * (DA_HEAD ** -0.5)
    k = k.reshape(b, s, DA_HEADS, 2, DA_HEAD)
    v = v.reshape(b, s, DA_HEADS, 2 * DA_HEAD).transpose(0, 2, 1, 3)
    q1, q2 = q[..., 0, :].transpose(0, 2, 1, 3), q[..., 1, :].transpose(0, 2, 1, 3)
    k1, k2 = k[..., 0, :].transpose(0, 2, 1, 3), k[..., 1, :].transpose(0, 2, 1, 3)
    f32 = jnp.float32
    lam = (jnp.exp(jnp.sum(lam_q1.astype(f32) * lam_k1.astype(f32)))
           - jnp.exp(jnp.sum(lam_q2.astype(f32) * lam_k2.astype(f32))) + lam_init)
    o = diff_attention(q1, q2, k1, k2, v, positions, lam)
    o = rmsnorm(o, g_sub) * (1.0 - lam_init)
    return o.transpose(0, 2, 1, 3).reshape(b, s, ODD_OUT) @ w_o


def setup_inputs(seed: int = 0) -> dict:
    key = jax.random.key(seed)
    ks = iter(jax.random.split(key, 64))
    f32 = jnp.float32

    def dense(fan_in, fan_out):
        return jax.random.normal(next(ks), (fan_in, fan_out), f32) * (fan_in ** -0.5)

    def gain(n):
        return 1.0 + 0.02 * jax.random.normal(next(ks), (n,), f32)

    def small(n, scale):
        return scale * jax.random.normal(next(ks), (n,), f32)

    x = jax.random.normal(next(ks), (BATCH, SEQ, D_MODEL), f32)
    offset = jax.random.randint(next(ks), (BATCH, 1), 0, SEQ, dtype=jnp.int32)
    positions = offset + jnp.arange(SEQ, dtype=jnp.int32)[None, :]
    f_bias = jnp.linspace(ML_F_BIAS_LO, ML_F_BIAS_HI, ML_HEADS, dtype=f32)
    b_gates = jnp.concatenate([small(ML_HEADS, 0.1), f_bias + small(ML_HEADS, 0.1),
                               small(ML_HEADS, 0.1), f_bias + small(ML_HEADS, 0.1)])
    return {
        "x": x,
        "positions": positions,
        "l0_ffn1_norm": gain(D_MODEL),
        "l0_ffn1_w_gu": dense(D_MODEL, 2 * D_FF),
        "l0_ffn1_w_down": dense(D_FF, D_MODEL),
        "l0_mix_norm": gain(D_MODEL),
        "l0_w_in": dense(D_MODEL, EVEN_IN),
        "l0_g_cq": gain(MLA_Q_RANK),
        "l0_w_uq": dense(MLA_Q_RANK, MLA_HEADS * (MLA_NOPE + MLA_ROPE)),
        "l0_g_ckv": gain(MLA_KV_RANK),
        "l0_w_ukv": dense(MLA_KV_RANK, MLA_HEADS * (MLA_NOPE + MLA_V)),
        "l0_b_gates": b_gates,
        "l0_g_mlstm": gain(ML_HEADS * ML_V),
        "l0_w_o": dense(EVEN_OUT, D_MODEL),
        "l0_ffn2_norm": gain(D_MODEL),
        "l0_ffn2_w_gu": dense(D_MODEL, 2 * D_FF),
        "l0_ffn2_w_down": dense(D_FF, D_MODEL),
        "l1_ffn1_norm": gain(D_MODEL),
        "l1_ffn1_w_gu": dense(D_MODEL, 2 * D_FF),
        "l1_ffn1_w_down": dense(D_FF, D_MODEL),
        "l1_mix_norm": gain(D_MODEL),
        "l1_w_in": dense(D_MODEL, ODD_IN),
        "l1_lam_q1": small(DA_HEAD, 0.1),
        "l1_lam_k1": small(DA_HEAD, 0.1),
        "l1_lam_q2": small(DA_HEAD, 0.1),
        "l1_lam_k2": small(DA_HEAD, 0.1),
        "l1_g_sub": gain(2 * DA_HEAD),
        "l1_w_o": dense(ODD_OUT, D_MODEL),
        "l1_ffn2_norm": gain(D_MODEL),
        "l1_ffn2_w_gu": dense(D_MODEL, 2 * D_FF),
        "l1_ffn2_w_down": dense(D_FF, D_MODEL),
        "final_norm": gain(D_MODEL),
    }


def reference(x, positions,
              l0_ffn1_norm, l0_ffn1_w_gu, l0_ffn1_w_down,
              l0_mix_norm, l0_w_in, l0_g_cq, l0_w_uq, l0_g_ckv, l0_w_ukv, l0_b_gates, l0_g_mlstm, l0_w_o,
              l0_ffn2_norm, l0_ffn2_w_gu, l0_ffn2_w_down,
              l1_ffn1_norm, l1_ffn1_w_gu, l1_ffn1_w_down,
              l1_mix_norm, l1_w_in, l1_lam_q1, l1_lam_k1, l1_lam_q2, l1_lam_k2, l1_g_sub, l1_w_o,
              l1_ffn2_norm, l1_ffn2_w_gu, l1_ffn2_w_down,
              final_norm):
    inv_freq = ROPE_THETA ** (-jnp.arange(0, MLA_ROPE, 2, dtype=jnp.float32) / MLA_ROPE)
    ang = positions.astype(jnp.float32)[..., None] * inv_freq
    cos, sin = jnp.cos(ang).astype(x.dtype), jnp.sin(ang).astype(x.dtype)

    ffn_params = [((l0_ffn1_norm, l0_ffn1_w_gu, l0_ffn1_w_down), (l0_ffn2_norm, l0_ffn2_w_gu, l0_ffn2_w_down)),
                  ((l1_ffn1_norm, l1_ffn1_w_gu, l1_ffn1_w_down), (l1_ffn2_norm, l1_ffn2_w_gu, l1_ffn2_w_down))]
    mix_norms = [l0_mix_norm, l1_mix_norm]
    even_params = [(l0_w_in, l0_g_cq, l0_w_uq, l0_g_ckv, l0_w_ukv, l0_b_gates, l0_g_mlstm, l0_w_o)]
    odd_params = [(l1_w_in, l1_lam_q1, l1_lam_k1, l1_lam_q2, l1_lam_k2, l1_g_sub, l1_w_o)]

    for layer in range(DEPTH):
        f1, f2 = ffn_params[layer]
        x = x + 0.5 * swiglu(rmsnorm(x, f1[0]), f1[1], f1[2])
        hn = rmsnorm(x, mix_norms[layer])
        if layer % 2 == 0:
            x = x + even_mixer(hn, cos, sin, *even_params[layer // 2])
        else:
            lam_init = 0.8 - 0.6 * math.exp(-0.3 * layer)
            x = x + odd_mixer(hn, positions, *odd_params[layer // 2], lam_init=lam_init)
        x = x + 0.5 * swiglu(rmsnorm(x, f2[0]), f2[1], f2[2])
    return rmsnorm(x, final_norm)
```

```python
import functools
import math

import jax
import jax.numpy as jnp
import numpy as np
from jax import lax
from jax.experimental import pallas as pl
from jax.experimental.pallas import tpu as pltpu

F32 = jnp.float32
BF16 = jnp.bfloat16

D_MODEL = 2048
MLA_HEADS = 8
MLA_Q_RANK = 512
MLA_KV_RANK = 512
MLA_NOPE = 128
MLA_ROPE = 64
MLA_V = 128
ROPE_THETA = 10000.0
ML_HEADS = 4
ML_QK = 128
ML_V = 256
DA_HEADS = 8
DA_HEAD = 128
D_FF = 5632
EPS = 1e-6
NEG_INIT = -1e30

LANES = 128
MLA_QK_PAD = 256
V7X_VMEM_BYTES = 64 * 1024 * 1024

EV_CQ = 0
EV_CKV = EV_CQ + MLA_Q_RANK
EV_MV = EV_CKV + MLA_KV_RANK
EV_MO = EV_MV + ML_HEADS * ML_V
EV_MQ = EV_MO + ML_HEADS * ML_V
EV_MK = EV_MQ + ML_HEADS * ML_QK
EV_KR = EV_MK + ML_HEADS * ML_QK
EV_MG = EV_KR + LANES
EV_COLS = EV_MG + LANES
assert EV_MV % ML_V == 0 and EV_MO % ML_V == 0 and EV_MQ % ML_QK == 0 and EV_MK % ML_QK == 0


class Tiles:
    ffn_rows = 512
    ffn_cols = 512
    proj_rows = 512
    proj_col_steps_even = 2
    proj_col_steps_odd = 3
    prep_rows = 512
    out_rows = 512
    attn_q = 512
    attn_k = 512
    mlstm_chunk = 256
    mlstm_post_rows = 512


def _compiler_params(semantics, vmem_bytes):
    return pltpu.CompilerParams(dimension_semantics=semantics,
                                vmem_limit_bytes=int(min(vmem_bytes, V7X_VMEM_BYTES - (6 << 20))))


def _rms(x, gain):
    y = x * lax.rsqrt(jnp.mean(x * x, axis=-1, keepdims=True) + EPS)
    return y * gain


def _ffn_kernel(*refs, n_ff_steps, final):
    if final:
        x_ref, g_ref, wg_ref, wu_ref, wd_ref, gf_ref, o_ref, xn_ref = refs
    else:
        x_ref, g_ref, wg_ref, wu_ref, wd_ref, o_ref, xn_ref = refs
    j = pl.program_id(1)

    @pl.when(j == 0)
    def _():
        x = x_ref[...]
        xn_ref[...] = _rms(x, g_ref[...]).astype(BF16)
        o_ref[...] = x

    xn = xn_ref[...]
    gate = jnp.dot(xn, wg_ref[...], preferred_element_type=F32)
    up = jnp.dot(xn, wu_ref[...], preferred_element_type=F32)
    act = (0.5 * (gate * jax.nn.sigmoid(gate)) * up).astype(BF16)
    o_ref[...] += jnp.dot(act, wd_ref[...], preferred_element_type=F32)

    if final:
        @pl.when(j == n_ff_steps - 1)
        def _():
            o_ref[...] = _rms(o_ref[...], gf_ref[...])


def _ffn(x, gain, w_gu, w_down, final_gain=None):
    n, d = x.shape
    tm, tf = Tiles.ffn_rows, Tiles.ffn_cols
    nf = D_FF // tf
    wgu = w_gu.astype(BF16)
    wd = w_down.astype(BF16)
    final = final_gain is not None
    in_specs = [
        pl.BlockSpec((tm, d), lambda i, j: (i, 0)),
        pl.BlockSpec((1, d), lambda i, j: (0, 0)),
        pl.BlockSpec((d, tf), lambda i, j: (0, j)),
        pl.BlockSpec((d, tf), lambda i, j: (0, j + nf)),
        pl.BlockSpec((tf, d), lambda i, j: (j, 0)),
    ]
    args = [x, gain.reshape(1, d), wgu, wgu, wd]
    if final:
        in_specs.append(pl.BlockSpec((1, d), lambda i, j: (0, 0)))
        args.append(final_gain.reshape(1, d))
    vmem = (2 * 2 * tm * d * 4 + tm * d * 2 + 2 * 3 * d * tf * 2 + 4 * tm * tf * 4 + tm * d * 4)
    return pl.pallas_call(
        functools.partial(_ffn_kernel, n_ff_steps=nf, final=final),
        grid=(n // tm, nf),
        in_specs=in_specs,
        out_specs=pl.BlockSpec((tm, d), lambda i, j: (i, 0)),
        out_shape=jax.ShapeDtypeStruct((n, d), F32),
        scratch_shapes=[pltpu.VMEM((tm, d), BF16)],
        compiler_params=_compiler_params(("parallel", "arbitrary"), vmem),
        name="ffn",
    )(*args)


def _normproj_kernel(x_ref, g_ref, w_ref, cs_ref, o_ref, xn_ref):
    @pl.when(pl.program_id(1) == 0)
    def _():
        xn_ref[...] = _rms(x_ref[...], g_ref[...]).astype(BF16)

    acc = jnp.dot(xn_ref[...], w_ref[...], preferred_element_type=F32)
    o_ref[...] = (acc * cs_ref[...]).astype(o_ref.dtype)


def _normproj(x, gain, w, col_scale, out_dtype, col_steps):
    n, d = x.shape
    cols = w.shape[1]
    tm, tn = Tiles.proj_rows, cols // col_steps
    assert tn * col_steps == cols and tn % LANES == 0
    out_bytes = jnp.dtype(out_dtype).itemsize
    vmem = 2 * tm * d * 4 + tm * d * 2 + 2 * d * tn * 2 + 2 * tm * tn * out_bytes + 2 * tm * tn * 4
    return pl.pallas_call(
        _normproj_kernel,
        grid=(n // tm, col_steps),
        in_specs=[
            pl.BlockSpec((tm, d), lambda i, j: (i, 0)),
            pl.BlockSpec((1, d), lambda i, j: (0, 0)),
            pl.BlockSpec((d, tn), lambda i, j: (0, j)),
            pl.BlockSpec((1, tn), lambda i, j: (0, j)),
        ],
        out_specs=pl.BlockSpec((tm, tn), lambda i, j: (i, j)),
        out_shape=jax.ShapeDtypeStruct((n, cols), out_dtype),
        scratch_shapes=[pltpu.VMEM((tm, d), BF16)],
        compiler_params=_compiler_params(("parallel", "arbitrary"), vmem),
        name="normproj",
    )(x, gain.reshape(1, d), w.astype(BF16), col_scale.reshape(1, cols))


def _rope_tile(x, cos_t, sin_t):
    lane = lax.broadcasted_iota(jnp.int32, x.shape, 1)
    half = MLA_ROPE // 2
    partner = jnp.where(lane % MLA_ROPE < half, pltpu.roll(x, LANES - half, 1), pltpu.roll(x, half, 1))
    return x * cos_t + partner * sin_t


def _mla_prep_kernel(cq_ref, ckv_ref, kr_ref, gq_ref, gkv_ref, wq_ref, wk_ref, wv_ref, cos_ref, sin_ref,
                     q_ref, k_ref, v_ref):
    cos_t = cos_ref[...]
    sin_t = sin_ref[...]
    cqn = _rms(cq_ref[...], gq_ref[...]).astype(BF16)
    ckvn = _rms(ckv_ref[...], gkv_ref[...]).astype(BF16)
    scale = (MLA_NOPE + MLA_ROPE) ** -0.5
    q = jnp.dot(cqn, wq_ref[...], preferred_element_type=F32)
    kn = jnp.dot(ckvn, wk_ref[...], preferred_element_type=F32)
    v_ref[...] = jnp.dot(ckvn, wv_ref[...], preferred_element_type=F32).astype(BF16)
    lane = lax.broadcasted_iota(jnp.int32, cos_t.shape, 1)
    k_rope = jnp.where(lane < MLA_ROPE, _rope_tile(kr_ref[...], cos_t, sin_t), 0.0).astype(BF16)
    for h in range(MLA_HEADS):
        base = h * MLA_QK_PAD
        q_ref[:, base:base + MLA_NOPE] = (q[:, base:base + MLA_NOPE] * scale).astype(BF16)
        q_rope = _rope_tile(q[:, base + MLA_NOPE:base + MLA_QK_PAD], cos_t, sin_t)
        q_ref[:, base + MLA_NOPE:base + MLA_QK_PAD] = (q_rope * scale).astype(BF16)
        k_ref[:, base:base + MLA_NOPE] = kn[:, h * MLA_NOPE:(h + 1) * MLA_NOPE].astype(BF16)
        k_ref[:, base + MLA_NOPE:base + MLA_QK_PAD] = k_rope


def _mla_prep(z, g_cq, g_ckv, wq, wk, wv, cos_t, sin_t):
    n = z.shape[0]
    tm = Tiles.prep_rows
    qk_cols = MLA_HEADS * MLA_QK_PAD
    v_cols = MLA_HEADS * MLA_V
    row = lambda i: (i, 0)
    fixed = lambda i: (0, 0)
    vmem = (2 * tm * (MLA_Q_RANK + MLA_KV_RANK + 3 * LANES) * 4 + 2 * MLA_Q_RANK * (qk_cols + 2 * v_cols) * 2
            + 2 * tm * (2 * qk_cols + v_cols) * 2 + 3 * tm * qk_cols * 4)
    return pl.pallas_call(
        _mla_prep_kernel,
        grid=(n // tm,),
        in_specs=[
            pl.BlockSpec((tm, MLA_Q_RANK), lambda i: (i, EV_CQ // MLA_Q_RANK)),
            pl.BlockSpec((tm, MLA_KV_RANK), lambda i: (i, EV_CKV // MLA_KV_RANK)),
            pl.BlockSpec((tm, LANES), lambda i: (i, EV_KR // LANES)),
            pl.BlockSpec((1, MLA_Q_RANK), fixed),
            pl.BlockSpec((1, MLA_KV_RANK), fixed),
            pl.BlockSpec((MLA_Q_RANK, qk_cols), fixed),
            pl.BlockSpec((MLA_KV_RANK, v_cols), fixed),
            pl.BlockSpec((MLA_KV_RANK, v_cols), fixed),
            pl.BlockSpec((tm, LANES), row),
            pl.BlockSpec((tm, LANES), row),
        ],
        out_specs=[
            pl.BlockSpec((tm, qk_cols), row),
            pl.BlockSpec((tm, qk_cols), row),
            pl.BlockSpec((tm, v_cols), row),
        ],
        out_shape=[
            jax.ShapeDtypeStruct((n, qk_cols), BF16),
            jax.ShapeDtypeStruct((n, qk_cols), BF16),
            jax.ShapeDtypeStruct((n, v_cols), BF16),
        ],
        compiler_params=_compiler_params(("parallel",), vmem),
        name="mla_prep",
    )(z, z, z, g_cq.reshape(1, -1), g_ckv.reshape(1, -1), wq, wk, wv, cos_t, sin_t)


def _mla_attn_kernel(q_ref, k_ref, v_ref, o_ref, *, n_chunks, tk):
    q = q_ref[0]
    tq = q.shape[0]

    def body(c, carry):
        m, l, acc = carry
        start = pl.multiple_of(c * tk, tk)
        k = k_ref[0, pl.ds(start, tk), :]
        v = v_ref[0, pl.ds(start, tk), :]
        s = lax.dot_general(q, k, (((1,), (1,)), ((), ())), preferred_element_type=F32)
        m_new = jnp.maximum(m, jnp.max(s, axis=-1, keepdims=True))
        alpha = jnp.exp(m - m_new)
        p = jnp.exp(s - m_new)
        l = alpha * l + jnp.sum(p, axis=-1, keepdims=True)
        acc = alpha * acc + jnp.dot(p.astype(BF16), v, preferred_element_type=F32)
        return m_new, l, acc

    init = (jnp.full((tq, 1), NEG_INIT, F32), jnp.zeros((tq, 1), F32), jnp.zeros((tq, MLA_V), F32))
    _, l, acc = lax.fori_loop(0, n_chunks, body, init)
    o_ref[0] = (acc / l).astype(o_ref.dtype)


def _mla_attention(q, k, v):
    b, s, _ = q.shape
    tq, tk = Tiles.attn_q, Tiles.attn_k
    vmem = (2 * tq * MLA_QK_PAD * 2 + 2 * s * (MLA_QK_PAD + MLA_V) * 2 + 2 * tq * MLA_V * 2
            + 6 * tq * tk * 4)
    return pl.pallas_call(
        functools.partial(_mla_attn_kernel, n_chunks=s // tk, tk=tk),
        grid=(b, MLA_HEADS, s // tq),
        in_specs=[
            pl.BlockSpec((1, tq, MLA_QK_PAD), lambda bi, h, i: (bi, i, h)),
            pl.BlockSpec((1, s, MLA_QK_PAD), lambda bi, h, i: (bi, 0, h)),
            pl.BlockSpec((1, s, MLA_V), lambda bi, h, i: (bi, 0, h)),
        ],
        out_specs=pl.BlockSpec((1, tq, MLA_V), lambda bi, h, i: (bi, i, h)),
        out_shape=jax.ShapeDtypeStruct((b, s, MLA_HEADS * MLA_V), BF16),
        compiler_params=_compiler_params(("parallel", "parallel", "arbitrary"), vmem),
        name="mla_attn",
    )(q, k, v)


def _log_sigmoid(x):
    return jnp.minimum(x, 0.0) - jnp.log(1.0 + jnp.exp(-jnp.abs(x)))


def _mlstm_chunk(q_ref, k_ref, kt_ref, v_ref, gates, c_ref, n_row, m_st, start, chunk, li_idx, lf_idx, reverse):
    L = chunk
    q = q_ref[0, pl.ds(start, L), :]
    k = k_ref[0, pl.ds(start, L), :]
    kt = kt_ref[0, 0, :, pl.ds(start, L)]
    v = v_ref[0, pl.ds(start, L), :].astype(BF16)
    li_row = gates[li_idx:li_idx + 1, :]
    lf_row = _log_sigmoid(gates[lf_idx:lf_idx + 1, :])
    t_idx = lax.broadcasted_iota(jnp.int32, (L, L), 0)
    s_idx = lax.broadcasted_iota(jnp.int32, (L, L), 1)
    eye = t_idx == s_idx
    visible = (s_idx >= t_idx) if reverse else (s_idx <= t_idx)

    def to_col(row):
        return jnp.sum(jnp.where(eye, row, 0.0), axis=1, keepdims=True)

    lf_col = to_col(lf_row)
    li_col = to_col(li_row)
    a_col = jnp.sum(jnp.where(visible, lf_row, 0.0), axis=1, keepdims=True)
    visible_t = (t_idx >= s_idx) if reverse else (t_idx <= s_idx)
    a_row = jnp.sum(jnp.where(visible_t, lf_col, 0.0), axis=0, keepdims=True)
    g = jnp.sum(lf_row, axis=1, keepdims=True)

    dmat = jnp.where(visible, a_col - a_row + li_row, -jnp.inf)
    inter = a_col + m_st
    m = jnp.maximum(inter, jnp.max(dmat, axis=1, keepdims=True))
    w_inter = jnp.exp(inter - m)
    q16 = q.astype(BF16)
    qk = jnp.dot(q16, kt.astype(BF16), preferred_element_type=F32) * jnp.exp(dmat - m)
    num = (w_inter * jnp.dot(q16, c_ref[...].astype(BF16), preferred_element_type=F32)
           + jnp.dot(qk.astype(BF16), v, preferred_element_type=F32))
    den = w_inter * jnp.sum(q * n_row, axis=1, keepdims=True) + jnp.sum(qk, axis=1, keepdims=True)
    h_out = num / jnp.maximum(jnp.abs(den), jnp.exp(-m))

    r_row = g - a_row + li_row
    r_col = g - a_col + li_col
    m_new = jnp.maximum(g + m_st, jnp.max(r_row, axis=1, keepdims=True))
    w_old = jnp.exp(g + m_st - m_new)
    w_r_row = jnp.exp(r_row - m_new)
    w_r_col = jnp.exp(r_col - m_new)
    c_ref[...] = w_old * c_ref[...] + jnp.dot((kt * w_r_row).astype(BF16), v, preferred_element_type=F32)
    n_new = w_old * n_row + jnp.sum(k * w_r_col, axis=0, keepdims=True)
    return h_out, n_new, m_new


def _mlstm_kernel(q_ref, k_ref, kt_ref, v_ref, gate_ref, bias_ref, mo_ref, gn_ref, o_ref,
                  hf_ref, hb_ref, cf_ref, cb_ref, *, seq, chunk, post_rows):
    nc = seq // chunk
    cf_ref[...] = jnp.zeros_like(cf_ref)
    cb_ref[...] = jnp.zeros_like(cb_ref)
    bias = bias_ref[0]

    def body(i, carry):
        n_f, m_f, n_b, m_b = carry
        sf = pl.multiple_of(i * chunk, chunk)
        sb = pl.multiple_of((nc - 1 - i) * chunk, chunk)
        gates_f = gate_ref[0, 0, :, pl.ds(sf, chunk)] + bias
        gates_b = gate_ref[0, 0, :, pl.ds(sb, chunk)] + bias
        h_f, n_f, m_f = _mlstm_chunk(q_ref, k_ref, kt_ref, v_ref, gates_f, cf_ref, n_f, m_f, sf, chunk, 0, 1, False)
        hf_ref[pl.ds(sf, chunk), :] = h_f
        h_b, n_b, m_b = _mlstm_chunk(q_ref, k_ref, kt_ref, v_ref, gates_b, cb_ref, n_b, m_b, sb, chunk, 2, 3, True)
        hb_ref[pl.ds(sb, chunk), :] = h_b
        return n_f, m_f, n_b, m_b

    n0 = jnp.zeros((1, ML_QK), F32)
    m0 = jnp.full((1, 1), NEG_INIT, F32)
    lax.fori_loop(0, nc, body, (n0, m0, n0, m0))

    gn = gn_ref[...]

    def post(i, carry):
        r = pl.multiple_of(i * post_rows, post_rows)
        hm = hf_ref[pl.ds(r, post_rows), :] + hb_ref[pl.ds(r, post_rows), :]
        y = _rms(hm, gn)
        o_ref[0, pl.ds(r, post_rows), :] = (y * jax.nn.sigmoid(mo_ref[0, pl.ds(r, post_rows), :])).astype(o_ref.dtype)
        return carry

    lax.fori_loop(0, seq // post_rows, post, 0)


def _mlstm(z3, kt, gates_t, gate_bias, g_mlstm):
    b, s, _ = z3.shape
    chunk = Tiles.mlstm_chunk
    vmem = (2 * s * (2 * ML_QK + ML_QK + 2 * ML_V) * 4 + 2 * s * ML_V * 2 + 2 * s * ML_V * 4
            + 2 * ML_QK * ML_V * 4 + 16 * chunk * chunk * 4 + (2 << 20))
    return pl.pallas_call(
        functools.partial(_mlstm_kernel, seq=s, chunk=chunk, post_rows=Tiles.mlstm_post_rows),
        grid=(b, ML_HEADS),
        in_specs=[
            pl.BlockSpec((1, s, ML_QK), lambda bi, h: (bi, 0, EV_MQ // ML_QK + h)),
            pl.BlockSpec((1, s, ML_QK), lambda bi, h: (bi, 0, EV_MK // ML_QK + h)),
            pl.BlockSpec((1, 1, ML_QK, s), lambda bi, h: (bi, h, 0, 0)),
            pl.BlockSpec((1, s, ML_V), lambda bi, h: (bi, 0, EV_MV // ML_V + h)),
            pl.BlockSpec((1, 1, 4, s), lambda bi, h: (bi, h, 0, 0)),
            pl.BlockSpec((1, 4, 1), lambda bi, h: (h, 0, 0)),
            pl.BlockSpec((1, s, ML_V), lambda bi, h: (bi, 0, EV_MO // ML_V + h)),
            pl.BlockSpec((1, ML_V), lambda bi, h: (0, h)),
        ],
        out_specs=pl.BlockSpec((1, s, ML_V), lambda bi, h: (bi, 0, h)),
        out_shape=jax.ShapeDtypeStruct((b, s, ML_HEADS * ML_V), BF16),
        scratch_shapes=[
            pltpu.VMEM((s, ML_V), F32),
            pltpu.VMEM((s, ML_V), F32),
            pltpu.VMEM((ML_QK, ML_V), F32),
            pltpu.VMEM((ML_QK, ML_V), F32),
        ],
        compiler_params=_compiler_params(("parallel", "parallel"), vmem),
        name="mlstm",
    )(z3, z3, kt, z3, gates_t, gate_bias, z3, g_mlstm.reshape(1, -1))


def _outproj_kernel(*refs, n_lhs):
    x_ref = refs[0]
    a_refs = refs[1:1 + n_lhs]
    w_refs = refs[1 + n_lhs:1 + 2 * n_lhs]
    o_ref = refs[1 + 2 * n_lhs]
    acc = x_ref[...]
    for a_ref, w_ref in zip(a_refs, w_refs):
        acc = acc + jnp.dot(a_ref[...], w_ref[...], preferred_element_type=F32)
    o_ref[...] = acc


def _outproj(x, lhs, weights):
    n, d = x.shape
    tm = Tiles.out_rows
    row = lambda i: (i, 0)
    fixed = lambda i: (0, 0)
    k_total = sum(a.shape[1] for a in lhs)
    vmem = 2 * 2 * tm * d * 4 + 2 * tm * k_total * 2 + 2 * k_total * d * 2 + tm * d * 4
    return pl.pallas_call(
        functools.partial(_outproj_kernel, n_lhs=len(lhs)),
        grid=(n // tm,),
        in_specs=([pl.BlockSpec((tm, d), row)]
                  + [pl.BlockSpec((tm, a.shape[1]), row) for a in lhs]
                  + [pl.BlockSpec(w.shape, fixed) for w in weights]),
        out_specs=pl.BlockSpec((tm, d), row),
        out_shape=jax.ShapeDtypeStruct((n, d), F32),
        compiler_params=_compiler_params(("parallel",), vmem),
        name="outproj",
    )(x, *lhs, *[w.astype(BF16) for w in weights])


def _diff_attn_kernel(slope_ref, q1_ref, q2_ref, k1_ref, k2_ref, v_ref, pq_ref, pk_ref, lam_ref, gs_ref, o_ref,
                      *, n_chunks, tk, lam_init):
    h = pl.program_id(1)
    slope = slope_ref[h]
    q1 = q1_ref[0]
    q2 = q2_ref[0]
    pos_q = pq_ref[0]
    tq = q1.shape[0]
    dv = 2 * DA_HEAD

    def softmax_step(s, v, state):
        m, l, acc = state
        m_new = jnp.maximum(m, jnp.max(s, axis=-1, keepdims=True))
        alpha = jnp.exp(m - m_new)
        p = jnp.exp(s - m_new)
        l = alpha * l + jnp.sum(p, axis=-1, keepdims=True)
        acc = alpha * acc + jnp.dot(p.astype(BF16), v, preferred_element_type=F32)
        return m_new, l, acc

    def body(c, carry):
        st1, st2 = carry
        start = pl.multiple_of(c * tk, tk)
        k1 = k1_ref[0, pl.ds(start, tk), :]
        k2 = k2_ref[0, pl.ds(start, tk), :]
        v = v_ref[0, pl.ds(start, tk), :]
        pos_k = pk_ref[0, :, pl.ds(start, tk)]
        bias = -slope * jnp.abs(pos_q - pos_k).astype(F32)
        nt = (((1,), (1,)), ((), ()))
        s1 = lax.dot_general(q1, k1, nt, preferred_element_type=F32) + bias
        s2 = lax.dot_general(q2, k2, nt, preferred_element_type=F32) + bias
        return softmax_step(s1, v, st1), softmax_step(s2, v, st2)

    init = (jnp.full((tq, 1), NEG_INIT, F32), jnp.zeros((tq, 1), F32), jnp.zeros((tq, dv), F32))
    (_, l1, acc1), (_, l2, acc2) = lax.fori_loop(0, n_chunks, body, (init, init))

    lam_vec = lam_ref[...]
    lam = (jnp.exp(jnp.sum(lam_vec[0:1] * lam_vec[1:2], axis=1, keepdims=True))
           - jnp.exp(jnp.sum(lam_vec[2:3] * lam_vec[3:4], axis=1, keepdims=True)) + lam_init)
    o = acc1 / l1 - lam * (acc2 / l2)
    o_ref[0] = (_rms(o, gs_ref[...]) * (1.0 - lam_init)).astype(o_ref.dtype)


def _diff_attention(zq, pos_col, pos_row, slopes, lam_vec, g_sub, lam_init):
    b, s, _ = zq.shape
    tq, tk = Tiles.attn_q, Tiles.attn_k
    dv = 2 * DA_HEAD
    k_base = DA_HEADS * 2
    v_base = 2 * DA_HEADS
    vmem = (2 * 2 * tq * DA_HEAD * 2 + 2 * s * (2 * DA_HEAD + dv) * 2 + 2 * tq * LANES * 4 + 2 * tq * dv * 2
            + 10 * tq * tk * 4)
    return pl.pallas_call(
        functools.partial(_diff_attn_kernel, n_chunks=s // tk, tk=tk, lam_init=lam_init),
        grid=(b, DA_HEADS, s // tq),
        in_specs=[
            pl.BlockSpec(memory_space=pltpu.SMEM),
            pl.BlockSpec((1, tq, DA_HEAD), lambda bi, h, i: (bi, i, 2 * h)),
            pl.BlockSpec((1, tq, DA_HEAD), lambda bi, h, i: (bi, i, 2 * h + 1)),
            pl.BlockSpec((1, s, DA_HEAD), lambda bi, h, i: (bi, 0, k_base + 2 * h)),
            pl.BlockSpec((1, s, DA_HEAD), lambda bi, h, i: (bi, 0, k_base + 2 * h + 1)),
            pl.BlockSpec((1, s, dv), lambda bi, h, i: (bi, 0, v_base + h)),
            pl.BlockSpec((1, tq, 1), lambda bi, h, i: (bi, i, 0)),
            pl.BlockSpec((1, 1, s), lambda bi, h, i: (bi, 0, 0)),
            pl.BlockSpec((4, DA_HEAD), lambda bi, h, i: (0, 0)),
            pl.BlockSpec((1, dv), lambda bi, h, i: (0, 0)),
        ],
        out_specs=pl.BlockSpec((1, tq, dv), lambda bi, h, i: (bi, i, h)),
        out_shape=jax.ShapeDtypeStruct((b, s, DA_HEADS * dv), BF16),
        compiler_params=_compiler_params(("parallel", "parallel", "arbitrary"), vmem),
        name="diff_attn",
    )(slopes, zq, zq, zq, zq, zq, pos_col, pos_row, lam_vec, g_sub.reshape(1, dv))


def _pack_even_w_in(w_in):
    d = w_in.shape[0]
    sizes = (MLA_Q_RANK, MLA_KV_RANK, MLA_ROPE, ML_HEADS * ML_QK, ML_HEADS * ML_QK, ML_HEADS * ML_V,
             ML_HEADS * ML_V, 4 * ML_HEADS)
    c_q, c_kv, k_r, m_q, m_k, m_v, m_o, m_g = jnp.split(w_in, np.cumsum(sizes)[:-1].tolist(), axis=1)
    pad = lambda w, width: jnp.concatenate([w, jnp.zeros((d, width - w.shape[1]), w.dtype)], axis=1)
    packed = jnp.concatenate([c_q, c_kv, m_v, m_o, m_q, m_k, pad(k_r, LANES), pad(m_g, LANES)], axis=1)
    assert packed.shape[1] == EV_COLS
    return packed


def _pack_w_uq(w_uq):
    r = w_uq.shape[0]
    w = w_uq.reshape(r, MLA_HEADS, MLA_NOPE + MLA_ROPE)
    w = jnp.concatenate([w, jnp.zeros((r, MLA_HEADS, MLA_QK_PAD - MLA_NOPE - MLA_ROPE), w.dtype)], axis=2)
    return w.reshape(r, MLA_HEADS * MLA_QK_PAD).astype(BF16)


def _split_w_ukv(w_ukv):
    r = w_ukv.shape[0]
    w = w_ukv.reshape(r, MLA_HEADS, MLA_NOPE + MLA_V)
    wk = w[:, :, :MLA_NOPE].reshape(r, MLA_HEADS * MLA_NOPE)
    wv = w[:, :, MLA_NOPE:].reshape(r, MLA_HEADS * MLA_V)
    return wk.astype(BF16), wv.astype(BF16)


def _rope_tables(positions):
    inv_freq = ROPE_THETA ** (-jnp.arange(0, MLA_ROPE, 2, dtype=F32) / MLA_ROPE)
    ang = positions.astype(F32).reshape(-1, 1) * inv_freq
    cos, sin = jnp.cos(ang), jnp.sin(ang)
    zeros = jnp.zeros((ang.shape[0], LANES - MLA_ROPE), F32)
    return jnp.concatenate([cos, cos, zeros], axis=1), jnp.concatenate([-sin, sin, zeros], axis=1)


def _even_mixer(x, b, s, positions, norm, w_in, g_cq, w_uq, g_ckv, w_ukv, b_gates, g_mlstm, w_o):
    n = b * s
    col_scale = jnp.ones((EV_COLS,), F32).at[EV_MK:EV_MK + ML_HEADS * ML_QK].set(ML_QK ** -0.5)
    z = _normproj(x, norm, _pack_even_w_in(w_in), col_scale, F32, Tiles.proj_col_steps_even)

    cos_t, sin_t = _rope_tables(positions)
    wk, wv = _split_w_ukv(w_ukv)
    q, k, v = _mla_prep(z, g_cq, g_ckv, _pack_w_uq(w_uq), wk, wv, cos_t, sin_t)
    a_out = _mla_attention(q.reshape(b, s, -1), k.reshape(b, s, -1), v.reshape(b, s, -1))

    z3 = z.reshape(b, s, EV_COLS)
    kt = z3[:, :, EV_MK:EV_MK + ML_HEADS * ML_QK].reshape(b, s, ML_HEADS, ML_QK).transpose(0, 2, 3, 1)
    gates_t = z3[:, :, EV_MG:EV_MG + 4 * ML_HEADS].reshape(b, s, 4, ML_HEADS).transpose(0, 3, 2, 1)
    gate_bias = b_gates.astype(F32).reshape(4, ML_HEADS).T.reshape(ML_HEADS, 4, 1)
    m_out = _mlstm(z3, kt, gates_t, gate_bias, g_mlstm)

    split = MLA_HEADS * MLA_V
    return _outproj(x, [a_out.reshape(n, -1), m_out.reshape(n, -1)], [w_o[:split], w_o[split:]])


def _odd_mixer(x, b, s, positions, norm, w_in, lam_q1, lam_k1, lam_q2, lam_k2, g_sub, w_o, lam_init):
    n = b * s
    width = DA_HEADS * 2 * DA_HEAD
    col_scale = jnp.ones((3 * width,), F32).at[:width].set(DA_HEAD ** -0.5)
    zq = _normproj(x, norm, w_in, col_scale, BF16, Tiles.proj_col_steps_odd).reshape(b, s, 3 * width)
    slopes = jnp.asarray([2.0 ** (-8.0 * (h + 1) / DA_HEADS) for h in range(DA_HEADS)], dtype=F32)
    lam_vec = jnp.stack([lam_q1, lam_k1, lam_q2, lam_k2]).astype(F32)
    o = _diff_attention(zq, positions.reshape(b, s, 1), positions.reshape(b, 1, s), slopes, lam_vec, g_sub, lam_init)
    return _outproj(x, [o.reshape(n, width)], [w_o])


def kernel(x, positions, l0_ffn1_norm, l0_ffn1_w_gu, l0_ffn1_w_down, l0_mix_norm, l0_w_in, l0_g_cq, l0_w_uq, l0_g_ckv, l0_w_ukv, l0_b_gates, l0_g_mlstm, l0_w_o, l0_ffn2_norm, l0_ffn2_w_gu, l0_ffn2_w_down, l1_ffn1_norm, l1_ffn1_w_gu, l1_ffn1_w_down, l1_mix_norm, l1_w_in, l1_lam_q1, l1_lam_k1, l1_lam_q2, l1_lam_k2, l1_g_sub, l1_w_o, l1_ffn2_norm, l1_ffn2_w_gu, l1_ffn2_w_down, final_norm):
    b, s, d = x.shape
    h = x.reshape(b * s, d)
    h = _ffn(h, l0_ffn1_norm, l0_ffn1_w_gu, l0_ffn1_w_down)
    h = _even_mixer(h, b, s, positions, l0_mix_norm, l0_w_in, l0_g_cq, l0_w_uq, l0_g_ckv, l0_w_ukv, l0_b_gates,
                    l0_g_mlstm, l0_w_o)
    h = _ffn(h, l0_ffn2_norm, l0_ffn2_w_gu, l0_ffn2_w_down)
    h = _ffn(h, l1_ffn1_norm, l1_ffn1_w_gu, l1_ffn1_w_down)
    lam_init = 0.8 - 0.6 * math.exp(-0.3 * 1)
    h = _odd_mixer(h, b, s, positions, l1_mix_norm, l1_w_in, l1_lam_q1, l1_lam_k1, l1_lam_q2, l1_lam_k2, l1_g_sub,
                   l1_w_o, lam_init)
    h = _ffn(h, l1_ffn2_norm, l1_ffn2_w_gu, l1_ffn2_w_down, final_gain=final_norm)
    return h.reshape(b, s, d)
```

```python
import functools
import math

import jax
import jax.numpy as jnp
import numpy as np
from jax import lax
from jax.experimental import pallas as pl
from jax.experimental.pallas import tpu as pltpu

F32 = jnp.float32
BF16 = jnp.bfloat16

D_MODEL = 2048
MLA_HEADS = 8
MLA_Q_RANK = 512
MLA_KV_RANK = 512
MLA_NOPE = 128
MLA_ROPE = 64
MLA_V = 128
ROPE_THETA = 10000.0
ML_HEADS = 4
ML_QK = 128
ML_V = 256
DA_HEADS = 8
DA_HEAD = 128
D_FF = 5632
EPS = 1e-6
NEG_INIT = -1e30

LANES = 128
BF16_SUBLANES = 16
MLA_QK_PAD = 256
V7X_VMEM_BYTES = 64 * 1024 * 1024

EV_CQ = 0
EV_CKV = EV_CQ + MLA_Q_RANK
EV_MV = EV_CKV + MLA_KV_RANK
EV_MO = EV_MV + ML_HEADS * ML_V
EV_MQ = EV_MO + ML_HEADS * ML_V
EV_MK = EV_MQ + ML_HEADS * ML_QK
EV_KR = EV_MK + ML_HEADS * ML_QK
EV_MG = EV_KR + LANES
EV_COLS = EV_MG + LANES
assert EV_MV % ML_V == 0 and EV_MO % ML_V == 0 and EV_MQ % ML_QK == 0 and EV_MK % ML_QK == 0


class Tiles:
    ffn_rows = 512
    ffn_cols = 512
    proj_rows = 512
    proj_col_steps_even = 2
    proj_col_steps_odd = 3
    prep_rows = 512
    out_rows = 512
    mla_q = 1024
    mla_k = 512
    attn_q = 512
    attn_k = 512
    mlstm_chunk = 256
    mlstm_post_rows = 512


def _compiler_params(semantics, vmem_bytes):
    return pltpu.CompilerParams(dimension_semantics=semantics,
                                vmem_limit_bytes=int(min(vmem_bytes, V7X_VMEM_BYTES - (6 << 20))))


def _rms(x, gain):
    y = x * lax.rsqrt(jnp.mean(x * x, axis=-1, keepdims=True) + EPS)
    return y * gain


def _ffn_kernel(*refs, n_ff_steps, final):
    if final:
        x_ref, g_ref, wg_ref, wu_ref, wd_ref, gf_ref, o_ref, xn_ref = refs
    else:
        x_ref, g_ref, wg_ref, wu_ref, wd_ref, o_ref, xn_ref = refs
    j = pl.program_id(1)

    @pl.when(j == 0)
    def _():
        x = x_ref[...]
        xn_ref[...] = _rms(x, g_ref[...]).astype(BF16)
        o_ref[...] = x

    xn = xn_ref[...]
    gate = jnp.dot(xn, wg_ref[...], preferred_element_type=F32)
    up = jnp.dot(xn, wu_ref[...], preferred_element_type=F32)
    act = (0.5 * (gate * jax.nn.sigmoid(gate)) * up).astype(BF16)
    o_ref[...] += jnp.dot(act, wd_ref[...], preferred_element_type=F32)

    if final:
        @pl.when(j == n_ff_steps - 1)
        def _():
            o_ref[...] = _rms(o_ref[...], gf_ref[...])


def _ffn(x, gain, w_gu, w_down, final_gain=None):
    n, d = x.shape
    tm, tf = Tiles.ffn_rows, Tiles.ffn_cols
    nf = D_FF // tf
    wgu = w_gu.astype(BF16)
    wd = w_down.astype(BF16)
    final = final_gain is not None
    in_specs = [
        pl.BlockSpec((tm, d), lambda i, j: (i, 0)),
        pl.BlockSpec((1, d), lambda i, j: (0, 0)),
        pl.BlockSpec((d, tf), lambda i, j: (0, j)),
        pl.BlockSpec((d, tf), lambda i, j: (0, j + nf)),
        pl.BlockSpec((tf, d), lambda i, j: (j, 0)),
    ]
    args = [x, gain.reshape(1, d), wgu, wgu, wd]
    if final:
        in_specs.append(pl.BlockSpec((1, d), lambda i, j: (0, 0)))
        args.append(final_gain.reshape(1, d))
    vmem = (2 * 2 * tm * d * 4 + tm * d * 2 + 2 * 3 * d * tf * 2 + 4 * tm * tf * 4 + tm * d * 4)
    return pl.pallas_call(
        functools.partial(_ffn_kernel, n_ff_steps=nf, final=final),
        grid=(n // tm, nf),
        in_specs=in_specs,
        out_specs=pl.BlockSpec((tm, d), lambda i, j: (i, 0)),
        out_shape=jax.ShapeDtypeStruct((n, d), F32),
        scratch_shapes=[pltpu.VMEM((tm, d), BF16)],
        compiler_params=_compiler_params(("parallel", "arbitrary"), vmem),
        name="ffn",
    )(*args)


def _normproj_kernel(x_ref, g_ref, w_ref, cs_ref, o_ref, xn_ref):
    @pl.when(pl.program_id(1) == 0)
    def _():
        xn_ref[...] = _rms(x_ref[...], g_ref[...]).astype(BF16)

    acc = jnp.dot(xn_ref[...], w_ref[...], preferred_element_type=F32)
    o_ref[...] = (acc * cs_ref[...]).astype(o_ref.dtype)


def _normproj(x, gain, w, col_scale, out_dtype, col_steps):
    n, d = x.shape
    cols = w.shape[1]
    tm, tn = Tiles.proj_rows, cols // col_steps
    assert tn * col_steps == cols and tn % LANES == 0
    out_bytes = jnp.dtype(out_dtype).itemsize
    vmem = 2 * tm * d * 4 + tm * d * 2 + 2 * d * tn * 2 + 2 * tm * tn * out_bytes + 2 * tm * tn * 4
    return pl.pallas_call(
        _normproj_kernel,
        grid=(n // tm, col_steps),
        in_specs=[
            pl.BlockSpec((tm, d), lambda i, j: (i, 0)),
            pl.BlockSpec((1, d), lambda i, j: (0, 0)),
            pl.BlockSpec((d, tn), lambda i, j: (0, j)),
            pl.BlockSpec((1, tn), lambda i, j: (0, j)),
        ],
        out_specs=pl.BlockSpec((tm, tn), lambda i, j: (i, j)),
        out_shape=jax.ShapeDtypeStruct((n, cols), out_dtype),
        scratch_shapes=[pltpu.VMEM((tm, d), BF16)],
        compiler_params=_compiler_params(("parallel", "arbitrary"), vmem),
        name="normproj",
    )(x, gain.reshape(1, d), w.astype(BF16), col_scale.reshape(1, cols))


def _rope_tile(x, cos_t, sin_t):
    lane = lax.broadcasted_iota(jnp.int32, x.shape, 1)
    half = MLA_ROPE // 2
    partner = jnp.where(lane % MLA_ROPE < half, pltpu.roll(x, LANES - half, 1), pltpu.roll(x, half, 1))
    return x * cos_t + partner * sin_t


def _mla_prep_kernel(cq_ref, ckv_ref, kr_ref, gq_ref, gkv_ref, wq_ref, wk_ref, wv_ref, cos_ref, sin_ref,
                     q_ref, k_ref, v_ref):
    cos_t = cos_ref[...]
    sin_t = sin_ref[...]
    cqn = _rms(cq_ref[...], gq_ref[...]).astype(BF16)
    ckvn = _rms(ckv_ref[...], gkv_ref[...]).astype(BF16)
    scale = (MLA_NOPE + MLA_ROPE) ** -0.5
    q = jnp.dot(cqn, wq_ref[...], preferred_element_type=F32)
    kn = jnp.dot(ckvn, wk_ref[...], preferred_element_type=F32)
    v_ref[...] = jnp.dot(ckvn, wv_ref[...], preferred_element_type=F32).astype(BF16)
    lane = lax.broadcasted_iota(jnp.int32, cos_t.shape, 1)
    k_rope = jnp.where(lane < MLA_ROPE, _rope_tile(kr_ref[...], cos_t, sin_t), 0.0).astype(BF16)
    for h in range(MLA_HEADS):
        base = h * MLA_QK_PAD
        q_ref[:, base:base + MLA_NOPE] = (q[:, base:base + MLA_NOPE] * scale).astype(BF16)
        q_rope = _rope_tile(q[:, base + MLA_NOPE:base + MLA_QK_PAD], cos_t, sin_t)
        q_ref[:, base + MLA_NOPE:base + MLA_QK_PAD] = (q_rope * scale).astype(BF16)
        k_ref[:, base:base + MLA_NOPE] = kn[:, h * MLA_NOPE:(h + 1) * MLA_NOPE].astype(BF16)
        k_ref[:, base + MLA_NOPE:base + MLA_QK_PAD] = k_rope


def _mla_prep(z, g_cq, g_ckv, wq, wk, wv, cos_t, sin_t):
    n = z.shape[0]
    tm = Tiles.prep_rows
    qk_cols = MLA_HEADS * MLA_QK_PAD
    v_cols = MLA_HEADS * MLA_V
    row = lambda i: (i, 0)
    fixed = lambda i: (0, 0)
    vmem = (2 * tm * (MLA_Q_RANK + MLA_KV_RANK + 3 * LANES) * 4 + 2 * MLA_Q_RANK * (qk_cols + 2 * v_cols) * 2
            + 2 * tm * (2 * qk_cols + v_cols) * 2 + 3 * tm * qk_cols * 4)
    return pl.pallas_call(
        _mla_prep_kernel,
        grid=(n // tm,),
        in_specs=[
            pl.BlockSpec((tm, MLA_Q_RANK), lambda i: (i, EV_CQ // MLA_Q_RANK)),
            pl.BlockSpec((tm, MLA_KV_RANK), lambda i: (i, EV_CKV // MLA_KV_RANK)),
            pl.BlockSpec((tm, LANES), lambda i: (i, EV_KR // LANES)),
            pl.BlockSpec((1, MLA_Q_RANK), fixed),
            pl.BlockSpec((1, MLA_KV_RANK), fixed),
            pl.BlockSpec((MLA_Q_RANK, qk_cols), fixed),
            pl.BlockSpec((MLA_KV_RANK, v_cols), fixed),
            pl.BlockSpec((MLA_KV_RANK, v_cols), fixed),
            pl.BlockSpec((tm, LANES), row),
            pl.BlockSpec((tm, LANES), row),
        ],
        out_specs=[
            pl.BlockSpec((tm, qk_cols), row),
            pl.BlockSpec((tm, qk_cols), row),
            pl.BlockSpec((tm, v_cols), row),
        ],
        out_shape=[
            jax.ShapeDtypeStruct((n, qk_cols), BF16),
            jax.ShapeDtypeStruct((n, qk_cols), BF16),
            jax.ShapeDtypeStruct((n, v_cols), BF16),
        ],
        compiler_params=_compiler_params(("parallel",), vmem),
        name="mla_prep",
    )(z, z, z, g_cq.reshape(1, -1), g_ckv.reshape(1, -1), wq, wk, wv, cos_t, sin_t)


def _mla_attn_kernel(q_ref, k_ref, vt_ref, o_ref, *, n_chunks, tk):
    q = q_ref[0]
    tq = q.shape[0]
    rows = vt_ref.shape[2]

    def scores(c):
        k = k_ref[0, c * tk:(c + 1) * tk, :]
        return lax.dot_general(k, q, (((1,), (1,)), ((), ())), preferred_element_type=F32)

    m = jnp.full((1, tq), NEG_INIT, F32)
    acc = jnp.zeros((rows, tq), F32)
    s = scores(0)
    for c in range(n_chunks):
        s_next = scores(c + 1) if c + 1 < n_chunks else None
        m_new = jnp.maximum(m, jnp.max(s, axis=0, keepdims=True))
        alpha = jnp.exp(m - m_new)
        p = jnp.exp(s - m_new).astype(BF16)
        acc = alpha * acc + jnp.dot(vt_ref[0, 0, :, c * tk:(c + 1) * tk], p, preferred_element_type=F32)
        m, s = m_new, s_next
    o_t = acc[:MLA_V] / acc[MLA_V:MLA_V + 1]
    o_ref[0] = o_t.T.astype(o_ref.dtype)


def _values_transposed_with_ones(v, b, s, heads, width):
    vt = v.reshape(b, s, heads, width).transpose(0, 2, 3, 1)
    return jnp.concatenate([vt, jnp.ones((b, heads, BF16_SUBLANES, s), v.dtype)], axis=2)


def _mla_attention(q, k, vt):
    b, s, _ = q.shape
    tq, tk = Tiles.mla_q, Tiles.mla_k
    rows = vt.shape[2]
    vmem = (2 * tq * MLA_QK_PAD * 2 + 2 * s * (MLA_QK_PAD + rows) * 2 + 2 * tq * MLA_V * 2
            + 6 * tq * tk * 4)
    return pl.pallas_call(
        functools.partial(_mla_attn_kernel, n_chunks=s // tk, tk=tk),
        grid=(b, MLA_HEADS, s // tq),
        in_specs=[
            pl.BlockSpec((1, tq, MLA_QK_PAD), lambda bi, h, i: (bi, i, h)),
            pl.BlockSpec((1, s, MLA_QK_PAD), lambda bi, h, i: (bi, 0, h)),
            pl.BlockSpec((1, 1, rows, s), lambda bi, h, i: (bi, h, 0, 0)),
        ],
        out_specs=pl.BlockSpec((1, tq, MLA_V), lambda bi, h, i: (bi, i, h)),
        out_shape=jax.ShapeDtypeStruct((b, s, MLA_HEADS * MLA_V), BF16),
        compiler_params=_compiler_params(("parallel", "parallel", "arbitrary"), vmem),
        name="mla_attn",
    )(q, k, vt)


def _log_sigmoid(x):
    return jnp.minimum(x, 0.0) - jnp.log(1.0 + jnp.exp(-jnp.abs(x)))


def _mlstm_chunk(q_ref, k_ref, kt_ref, v_ref, gates, c_ref, n_row, m_st, start, chunk, li_idx, lf_idx, reverse):
    L = chunk
    q = q_ref[0, pl.ds(start, L), :]
    k = k_ref[0, pl.ds(start, L), :]
    kt = kt_ref[0, 0, :, pl.ds(start, L)]
    v = v_ref[0, pl.ds(start, L), :].astype(BF16)
    li_row = gates[li_idx:li_idx + 1, :]
    lf_row = _log_sigmoid(gates[lf_idx:lf_idx + 1, :])
    t_idx = lax.broadcasted_iota(jnp.int32, (L, L), 0)
    s_idx = lax.broadcasted_iota(jnp.int32, (L, L), 1)
    eye = t_idx == s_idx
    visible = (s_idx >= t_idx) if reverse else (s_idx <= t_idx)

    def to_col(row):
        return jnp.sum(jnp.where(eye, row, 0.0), axis=1, keepdims=True)

    lf_col = to_col(lf_row)
    li_col = to_col(li_row)
    a_col = jnp.sum(jnp.where(visible, lf_row, 0.0), axis=1, keepdims=True)
    visible_t = (t_idx >= s_idx) if reverse else (t_idx <= s_idx)
    a_row = jnp.sum(jnp.where(visible_t, lf_col, 0.0), axis=0, keepdims=True)
    g = jnp.sum(lf_row, axis=1, keepdims=True)

    dmat = jnp.where(visible, a_col - a_row + li_row, -jnp.inf)
    inter = a_col + m_st
    m = jnp.maximum(inter, jnp.max(dmat, axis=1, keepdims=True))
    w_inter = jnp.exp(inter - m)
    q16 = q.astype(BF16)
    qk = jnp.dot(q16, kt.astype(BF16), preferred_element_type=F32) * jnp.exp(dmat - m)
    num = (w_inter * jnp.dot(q16, c_ref[...].astype(BF16), preferred_element_type=F32)
           + jnp.dot(qk.astype(BF16), v, preferred_element_type=F32))
    den = w_inter * jnp.sum(q * n_row, axis=1, keepdims=True) + jnp.sum(qk, axis=1, keepdims=True)
    h_out = num / jnp.maximum(jnp.abs(den), jnp.exp(-m))

    r_row = g - a_row + li_row
    r_col = g - a_col + li_col
    m_new = jnp.maximum(g + m_st, jnp.max(r_row, axis=1, keepdims=True))
    w_old = jnp.exp(g + m_st - m_new)
    w_r_row = jnp.exp(r_row - m_new)
    w_r_col = jnp.exp(r_col - m_new)
    c_ref[...] = w_old * c_ref[...] + jnp.dot((kt * w_r_row).astype(BF16), v, preferred_element_type=F32)
    n_new = w_old * n_row + jnp.sum(k * w_r_col, axis=0, keepdims=True)
    return h_out, n_new, m_new


def _mlstm_kernel(q_ref, k_ref, kt_ref, v_ref, gate_ref, bias_ref, mo_ref, gn_ref, o_ref,
                  hf_ref, hb_ref, cf_ref, cb_ref, *, seq, chunk, post_rows):
    nc = seq // chunk
    cf_ref[...] = jnp.zeros_like(cf_ref)
    cb_ref[...] = jnp.zeros_like(cb_ref)
    bias = bias_ref[0]

    def body(i, carry):
        n_f, m_f, n_b, m_b = carry
        sf = pl.multiple_of(i * chunk, chunk)
        sb = pl.multiple_of((nc - 1 - i) * chunk, chunk)
        gates_f = gate_ref[0, 0, :, pl.ds(sf, chunk)] + bias
        gates_b = gate_ref[0, 0, :, pl.ds(sb, chunk)] + bias
        h_f, n_f, m_f = _mlstm_chunk(q_ref, k_ref, kt_ref, v_ref, gates_f, cf_ref, n_f, m_f, sf, chunk, 0, 1, False)
        hf_ref[pl.ds(sf, chunk), :] = h_f
        h_b, n_b, m_b = _mlstm_chunk(q_ref, k_ref, kt_ref, v_ref, gates_b, cb_ref, n_b, m_b, sb, chunk, 2, 3, True)
        hb_ref[pl.ds(sb, chunk), :] = h_b
        return n_f, m_f, n_b, m_b

    n0 = jnp.zeros((1, ML_QK), F32)
    m0 = jnp.full((1, 1), NEG_INIT, F32)
    lax.fori_loop(0, nc, body, (n0, m0, n0, m0))

    gn = gn_ref[...]

    def post(i, carry):
        r = pl.multiple_of(i * post_rows, post_rows)
        hm = hf_ref[pl.ds(r, post_rows), :] + hb_ref[pl.ds(r, post_rows), :]
        y = _rms(hm, gn)
        o_ref[0, pl.ds(r, post_rows), :] = (y * jax.nn.sigmoid(mo_ref[0, pl.ds(r, post_rows), :])).astype(o_ref.dtype)
        return carry

    lax.fori_loop(0, seq // post_rows, post, 0)


def _mlstm(z3, kt, gates_t, gate_bias, g_mlstm):
    b, s, _ = z3.shape
    chunk = Tiles.mlstm_chunk
    vmem = (2 * s * (2 * ML_QK + ML_QK + 2 * ML_V) * 4 + 2 * s * ML_V * 2 + 2 * s * ML_V * 4
            + 2 * ML_QK * ML_V * 4 + 16 * chunk * chunk * 4 + (2 << 20))
    return pl.pallas_call(
        functools.partial(_mlstm_kernel, seq=s, chunk=chunk, post_rows=Tiles.mlstm_post_rows),
        grid=(b, ML_HEADS),
        in_specs=[
            pl.BlockSpec((1, s, ML_QK), lambda bi, h: (bi, 0, EV_MQ // ML_QK + h)),
            pl.BlockSpec((1, s, ML_QK), lambda bi, h: (bi, 0, EV_MK // ML_QK + h)),
            pl.BlockSpec((1, 1, ML_QK, s), lambda bi, h: (bi, h, 0, 0)),
            pl.BlockSpec((1, s, ML_V), lambda bi, h: (bi, 0, EV_MV // ML_V + h)),
            pl.BlockSpec((1, 1, 4, s), lambda bi, h: (bi, h, 0, 0)),
            pl.BlockSpec((1, 4, 1), lambda bi, h: (h, 0, 0)),
            pl.BlockSpec((1, s, ML_V), lambda bi, h: (bi, 0, EV_MO // ML_V + h)),
            pl.BlockSpec((1, ML_V), lambda bi, h: (0, h)),
        ],
        out_specs=pl.BlockSpec((1, s, ML_V), lambda bi, h: (bi, 0, h)),
        out_shape=jax.ShapeDtypeStruct((b, s, ML_HEADS * ML_V), BF16),
        scratch_shapes=[
            pltpu.VMEM((s, ML_V), F32),
            pltpu.VMEM((s, ML_V), F32),
            pltpu.VMEM((ML_QK, ML_V), F32),
            pltpu.VMEM((ML_QK, ML_V), F32),
        ],
        compiler_params=_compiler_params(("parallel", "parallel"), vmem),
        name="mlstm",
    )(z3, z3, kt, z3, gates_t, gate_bias, z3, g_mlstm.reshape(1, -1))


def _outproj_kernel(*refs, n_lhs):
    x_ref = refs[0]
    a_refs = refs[1:1 + n_lhs]
    w_refs = refs[1 + n_lhs:1 + 2 * n_lhs]
    o_ref = refs[1 + 2 * n_lhs]
    acc = x_ref[...]
    for a_ref, w_ref in zip(a_refs, w_refs):
        acc = acc + jnp.dot(a_ref[...], w_ref[...], preferred_element_type=F32)
    o_ref[...] = acc


def _outproj(x, lhs, weights):
    n, d = x.shape
    tm = Tiles.out_rows
    row = lambda i: (i, 0)
    fixed = lambda i: (0, 0)
    k_total = sum(a.shape[1] for a in lhs)
    vmem = 2 * 2 * tm * d * 4 + 2 * tm * k_total * 2 + 2 * k_total * d * 2 + tm * d * 4
    return pl.pallas_call(
        functools.partial(_outproj_kernel, n_lhs=len(lhs)),
        grid=(n // tm,),
        in_specs=([pl.BlockSpec((tm, d), row)]
                  + [pl.BlockSpec((tm, a.shape[1]), row) for a in lhs]
                  + [pl.BlockSpec(w.shape, fixed) for w in weights]),
        out_specs=pl.BlockSpec((tm, d), row),
        out_shape=jax.ShapeDtypeStruct((n, d), F32),
        compiler_params=_compiler_params(("parallel",), vmem),
        name="outproj",
    )(x, *lhs, *[w.astype(BF16) for w in weights])


def _diff_attn_kernel(slope_ref, q1_ref, q2_ref, k1_ref, k2_ref, vt_ref, pq_ref, pk_ref, lam_ref, gs_ref, o_ref,
                      *, n_chunks, tk, lam_init):
    h = pl.program_id(1)
    slope = slope_ref[h]
    q1 = q1_ref[0]
    q2 = q2_ref[0]
    tq = q1.shape[0]
    dv = 2 * DA_HEAD
    rows = vt_ref.shape[2]
    pos_q = slope * pq_ref[0].astype(F32)
    nt = (((1,), (1,)), ((), ()))

    def scores(c):
        rows_c = slice(c * tk, (c + 1) * tk)
        dist = jnp.abs(slope * pk_ref[0, rows_c, :].astype(F32) - pos_q)
        s1 = lax.dot_general(k1_ref[0, rows_c, :], q1, nt, preferred_element_type=F32) - dist
        s2 = lax.dot_general(k2_ref[0, rows_c, :], q2, nt, preferred_element_type=F32) - dist
        return s1, s2

    def softmax_step(s, vt, m, acc):
        m_new = jnp.maximum(m, jnp.max(s, axis=0, keepdims=True))
        alpha = jnp.exp(m - m_new)
        p = jnp.exp(s - m_new).astype(BF16)
        return m_new, alpha * acc + jnp.dot(vt, p, preferred_element_type=F32)

    m1 = m2 = jnp.full((1, tq), NEG_INIT, F32)
    acc1 = acc2 = jnp.zeros((rows, tq), F32)
    s1, s2 = scores(0)
    for c in range(n_chunks):
        nxt = scores(c + 1) if c + 1 < n_chunks else (None, None)
        vt = vt_ref[0, 0, :, c * tk:(c + 1) * tk]
        m1, acc1 = softmax_step(s1, vt, m1, acc1)
        m2, acc2 = softmax_step(s2, vt, m2, acc2)
        s1, s2 = nxt

    lam_vec = lam_ref[...]
    lam = (jnp.exp(jnp.sum(lam_vec[0:1] * lam_vec[1:2], axis=1, keepdims=True))
           - jnp.exp(jnp.sum(lam_vec[2:3] * lam_vec[3:4], axis=1, keepdims=True)) + lam_init)
    o_t = acc1[:dv] / acc1[dv:dv + 1] - lam * (acc2[:dv] / acc2[dv:dv + 1])
    o_ref[0] = (_rms(o_t.T, gs_ref[...]) * (1.0 - lam_init)).astype(o_ref.dtype)


def _diff_attention(zq, vt, pos_col, pos_row, slopes, lam_vec, g_sub, lam_init):
    b, s, _ = zq.shape
    tq, tk = Tiles.attn_q, Tiles.attn_k
    dv = 2 * DA_HEAD
    rows = vt.shape[2]
    k_base = DA_HEADS * 2
    vmem = (2 * 2 * tq * DA_HEAD * 2 + 2 * s * (2 * DA_HEAD + rows) * 2 + 2 * s * LANES * 4 + 2 * tq * dv * 2
            + 14 * tq * tk * 4)
    return pl.pallas_call(
        functools.partial(_diff_attn_kernel, n_chunks=s // tk, tk=tk, lam_init=lam_init),
        grid=(b, DA_HEADS, s // tq),
        in_specs=[
            pl.BlockSpec(memory_space=pltpu.SMEM),
            pl.BlockSpec((1, tq, DA_HEAD), lambda bi, h, i: (bi, i, 2 * h)),
            pl.BlockSpec((1, tq, DA_HEAD), lambda bi, h, i: (bi, i, 2 * h + 1)),
            pl.BlockSpec((1, s, DA_HEAD), lambda bi, h, i: (bi, 0, k_base + 2 * h)),
            pl.BlockSpec((1, s, DA_HEAD), lambda bi, h, i: (bi, 0, k_base + 2 * h + 1)),
            pl.BlockSpec((1, 1, rows, s), lambda bi, h, i: (bi, h, 0, 0)),
            pl.BlockSpec((1, 1, tq), lambda bi, h, i: (bi, 0, i)),
            pl.BlockSpec((1, s, 1), lambda bi, h, i: (bi, 0, 0)),
            pl.BlockSpec((4, DA_HEAD), lambda bi, h, i: (0, 0)),
            pl.BlockSpec((1, dv), lambda bi, h, i: (0, 0)),
        ],
        out_specs=pl.BlockSpec((1, tq, dv), lambda bi, h, i: (bi, i, h)),
        out_shape=jax.ShapeDtypeStruct((b, s, DA_HEADS * dv), BF16),
        compiler_params=_compiler_params(("parallel", "parallel", "arbitrary"), vmem),
        name="diff_attn",
    )(slopes, zq, zq, zq, zq, vt, pos_row, pos_col, lam_vec, g_sub.reshape(1, dv))


def _pack_even_w_in(w_in):
    d = w_in.shape[0]
    sizes = (MLA_Q_RANK, MLA_KV_RANK, MLA_ROPE, ML_HEADS * ML_QK, ML_HEADS * ML_QK, ML_HEADS * ML_V,
             ML_HEADS * ML_V, 4 * ML_HEADS)
    c_q, c_kv, k_r, m_q, m_k, m_v, m_o, m_g = jnp.split(w_in, np.cumsum(sizes)[:-1].tolist(), axis=1)
    pad = lambda w, width: jnp.concatenate([w, jnp.zeros((d, width - w.shape[1]), w.dtype)], axis=1)
    packed = jnp.concatenate([c_q, c_kv, m_v, m_o, m_q, m_k, pad(k_r, LANES), pad(m_g, LANES)], axis=1)
    assert packed.shape[1] == EV_COLS
    return packed


def _pack_w_uq(w_uq):
    r = w_uq.shape[0]
    w = w_uq.reshape(r, MLA_HEADS, MLA_NOPE + MLA_ROPE)
    w = jnp.concatenate([w, jnp.zeros((r, MLA_HEADS, MLA_QK_PAD - MLA_NOPE - MLA_ROPE), w.dtype)], axis=2)
    return w.reshape(r, MLA_HEADS * MLA_QK_PAD).astype(BF16)


def _split_w_ukv(w_ukv):
    r = w_ukv.shape[0]
    w = w_ukv.reshape(r, MLA_HEADS, MLA_NOPE + MLA_V)
    wk = w[:, :, :MLA_NOPE].reshape(r, MLA_HEADS * MLA_NOPE)
    wv = w[:, :, MLA_NOPE:].reshape(r, MLA_HEADS * MLA_V)
    return wk.astype(BF16), wv.astype(BF16)


def _rope_tables(positions):
    inv_freq = ROPE_THETA ** (-jnp.arange(0, MLA_ROPE, 2, dtype=F32) / MLA_ROPE)
    ang = positions.astype(F32).reshape(-1, 1) * inv_freq
    cos, sin = jnp.cos(ang), jnp.sin(ang)
    zeros = jnp.zeros((ang.shape[0], LANES - MLA_ROPE), F32)
    return jnp.concatenate([cos, cos, zeros], axis=1), jnp.concatenate([-sin, sin, zeros], axis=1)


def _even_mixer(x, b, s, positions, norm, w_in, g_cq, w_uq, g_ckv, w_ukv, b_gates, g_mlstm, w_o):
    n = b * s
    col_scale = jnp.ones((EV_COLS,), F32).at[EV_MK:EV_MK + ML_HEADS * ML_QK].set(ML_QK ** -0.5)
    z = _normproj(x, norm, _pack_even_w_in(w_in), col_scale, F32, Tiles.proj_col_steps_even)

    cos_t, sin_t = _rope_tables(positions)
    wk, wv = _split_w_ukv(w_ukv)
    q, k, v = _mla_prep(z, g_cq, g_ckv, _pack_w_uq(w_uq), wk, wv, cos_t, sin_t)
    a_out = _mla_attention(q.reshape(b, s, -1), k.reshape(b, s, -1),
                           _values_transposed_with_ones(v, b, s, MLA_HEADS, MLA_V))

    z3 = z.reshape(b, s, EV_COLS)
    kt = z3[:, :, EV_MK:EV_MK + ML_HEADS * ML_QK].reshape(b, s, ML_HEADS, ML_QK).transpose(0, 2, 3, 1)
    gates_t = z3[:, :, EV_MG:EV_MG + 4 * ML_HEADS].reshape(b, s, 4, ML_HEADS).transpose(0, 3, 2, 1)
    gate_bias = b_gates.astype(F32).reshape(4, ML_HEADS).T.reshape(ML_HEADS, 4, 1)
    m_out = _mlstm(z3, kt, gates_t, gate_bias, g_mlstm)

    split = MLA_HEADS * MLA_V
    return _outproj(x, [a_out.reshape(n, -1), m_out.reshape(n, -1)], [w_o[:split], w_o[split:]])


def _odd_mixer(x, b, s, positions, norm, w_in, lam_q1, lam_k1, lam_q2, lam_k2, g_sub, w_o, lam_init):
    n = b * s
    width = DA_HEADS * 2 * DA_HEAD
    col_scale = jnp.ones((3 * width,), F32).at[:width].set(DA_HEAD ** -0.5)
    zq = _normproj(x, norm, w_in, col_scale, BF16, Tiles.proj_col_steps_odd).reshape(b, s, 3 * width)
    slopes = jnp.asarray([2.0 ** (-8.0 * (h + 1) / DA_HEADS) for h in range(DA_HEADS)], dtype=F32)
    lam_vec = jnp.stack([lam_q1, lam_k1, lam_q2, lam_k2]).astype(F32)
    vt = _values_transposed_with_ones(zq[:, :, 2 * width:], b, s, DA_HEADS, 2 * DA_HEAD)
    o = _diff_attention(zq, vt, positions.reshape(b, s, 1), positions.reshape(b, 1, s), slopes, lam_vec, g_sub,
                        lam_init)
    return _outproj(x, [o.reshape(n, width)], [w_o])


def kernel(x, positions, l0_ffn1_norm, l0_ffn1_w_gu, l0_ffn1_w_down, l0_mix_norm, l0_w_in, l0_g_cq, l0_w_uq, l0_g_ckv, l0_w_ukv, l0_b_gates, l0_g_mlstm, l0_w_o, l0_ffn2_norm, l0_ffn2_w_gu, l0_ffn2_w_down, l1_ffn1_norm, l1_ffn1_w_gu, l1_ffn1_w_down, l1_mix_norm, l1_w_in, l1_lam_q1, l1_lam_k1, l1_lam_q2, l1_lam_k2, l1_g_sub, l1_w_o, l1_ffn2_norm, l1_ffn2_w_gu, l1_ffn2_w_down, final_norm):
    b, s, d = x.shape
    h = x.reshape(b * s, d)
    h = _ffn(h, l0_ffn1_norm, l0_ffn1_w_gu, l0_ffn1_w_down)
    h = _even_mixer(h, b, s, positions, l0_mix_norm, l0_w_in, l0_g_cq, l0_w_uq, l0_g_ckv, l0_w_ukv, l0_b_gates,
                    l0_g_mlstm, l0_w_o)
    h = _ffn(h, l0_ffn2_norm, l0_ffn2_w_gu, l0_ffn2_w_down)
    h = _ffn(h, l1_ffn1_norm, l1_ffn1_w_gu, l1_ffn1_w_down)
    lam_init = 0.8 - 0.6 * math.exp(-0.3 * 1)
    h = _odd_mixer(h, b, s, positions, l1_mix_norm, l1_w_in, l1_lam_q1, l1_lam_k1, l1_lam_q2, l1_lam_k2, l1_g_sub,
                   l1_w_o, lam_init)
    h = _ffn(h, l1_ffn2_norm, l1_ffn2_w_gu, l1_ffn2_w_down, final_gain=final_norm)
    return h.reshape(b, s, d)
```

```python
import functools
import math

import jax
import jax.numpy as jnp
import numpy as np
from jax import lax
from jax.experimental import pallas as pl
from jax.experimental.pallas import tpu as pltpu

F32 = jnp.float32
BF16 = jnp.bfloat16

D_MODEL = 2048
MLA_HEADS = 8
MLA_Q_RANK = 512
MLA_KV_RANK = 512
MLA_NOPE = 128
MLA_ROPE = 64
MLA_V = 128
ROPE_THETA = 10000.0
ML_HEADS = 4
ML_QK = 128
ML_V = 256
DA_HEADS = 8
DA_HEAD = 128
D_FF = 5632
EPS = 1e-6
NEG_INIT = -1e30
LOG2_E = math.log2(math.e)

LANES = 128
BF16_SUBLANES = 16
MLA_QK_PAD = 256
V7X_VMEM_BYTES = 64 * 1024 * 1024

EV_CQ = 0
EV_CKV = EV_CQ + MLA_Q_RANK
EV_MV = EV_CKV + MLA_KV_RANK
EV_MO = EV_MV + ML_HEADS * ML_V
EV_MQ = EV_MO + ML_HEADS * ML_V
EV_MK = EV_MQ + ML_HEADS * ML_QK
EV_KR = EV_MK + ML_HEADS * ML_QK
EV_MG = EV_KR + LANES
EV_COLS = EV_MG + LANES
assert EV_MV % ML_V == 0 and EV_MO % ML_V == 0 and EV_MQ % ML_QK == 0 and EV_MK % ML_QK == 0


class Tiles:
    ffn_rows = 1024
    ffn_cols = 512
    proj_rows = 512
    proj_col_steps_even = 2
    proj_col_steps_odd = 3
    prep_rows = 512
    out_rows = 512
    mla_q = 1024
    mla_k = 512
    attn_q = 512
    attn_k = 512
    mlstm_chunk = 256
    mlstm_post_rows = 512


def _compiler_params(semantics, vmem_bytes):
    return pltpu.CompilerParams(dimension_semantics=semantics,
                                vmem_limit_bytes=int(min(vmem_bytes, V7X_VMEM_BYTES - (6 << 20))))


def _rms(x, gain):
    y = x * lax.rsqrt(jnp.mean(x * x, axis=-1, keepdims=True) + EPS)
    return y * gain


def _ffn_kernel(*refs, n_ff_steps, final):
    if final:
        x_ref, g_ref, wg_ref, wu_ref, wd_ref, gf_ref, o_ref, xn_ref = refs
    else:
        x_ref, g_ref, wg_ref, wu_ref, wd_ref, o_ref, xn_ref = refs
    j = pl.program_id(1)

    @pl.when(j == 0)
    def _():
        x = x_ref[...]
        xn_ref[...] = _rms(x, g_ref[...]).astype(BF16)
        o_ref[...] = x

    xn = xn_ref[...]
    gate = jnp.dot(xn, wg_ref[...], preferred_element_type=F32)
    up = jnp.dot(xn, wu_ref[...], preferred_element_type=F32)
    act = (0.5 * (gate * jax.nn.sigmoid(gate)) * up).astype(BF16)
    o_ref[...] += jnp.dot(act, wd_ref[...], preferred_element_type=F32)

    if final:
        @pl.when(j == n_ff_steps - 1)
        def _():
            o_ref[...] = _rms(o_ref[...], gf_ref[...])


def _ffn(x, gain, w_gu, w_down, final_gain=None):
    n, d = x.shape
    tm, tf = Tiles.ffn_rows, Tiles.ffn_cols
    nf = D_FF // tf
    wgu = w_gu.astype(BF16)
    wd = w_down.astype(BF16)
    final = final_gain is not None
    in_specs = [
        pl.BlockSpec((tm, d), lambda i, j: (i, 0)),
        pl.BlockSpec((1, d), lambda i, j: (0, 0)),
        pl.BlockSpec((d, tf), lambda i, j: (0, j)),
        pl.BlockSpec((d, tf), lambda i, j: (0, j + nf)),
        pl.BlockSpec((tf, d), lambda i, j: (j, 0)),
    ]
    args = [x, gain.reshape(1, d), wgu, wgu, wd]
    if final:
        in_specs.append(pl.BlockSpec((1, d), lambda i, j: (0, 0)))
        args.append(final_gain.reshape(1, d))
    vmem = (2 * 2 * tm * d * 4 + tm * d * 2 + 2 * 3 * d * tf * 2 + 4 * tm * tf * 4 + tm * d * 4)
    return pl.pallas_call(
        functools.partial(_ffn_kernel, n_ff_steps=nf, final=final),
        grid=(n // tm, nf),
        in_specs=in_specs,
        out_specs=pl.BlockSpec((tm, d), lambda i, j: (i, 0)),
        out_shape=jax.ShapeDtypeStruct((n, d), F32),
        scratch_shapes=[pltpu.VMEM((tm, d), BF16)],
        compiler_params=_compiler_params(("parallel", "arbitrary"), vmem),
        name="ffn",
    )(*args)


def _normproj_kernel(x_ref, g_ref, w_ref, cs_ref, o_ref, xn_ref):
    @pl.when(pl.program_id(1) == 0)
    def _():
        xn_ref[...] = _rms(x_ref[...], g_ref[...]).astype(BF16)

    acc = jnp.dot(xn_ref[...], w_ref[...], preferred_element_type=F32)
    o_ref[...] = (acc * cs_ref[...]).astype(o_ref.dtype)


def _normproj(x, gain, w, col_scale, out_dtype, col_steps):
    n, d = x.shape
    cols = w.shape[1]
    tm, tn = Tiles.proj_rows, cols // col_steps
    assert tn * col_steps == cols and tn % LANES == 0
    out_bytes = jnp.dtype(out_dtype).itemsize
    vmem = 2 * tm * d * 4 + tm * d * 2 + 2 * d * tn * 2 + 2 * tm * tn * out_bytes + 2 * tm * tn * 4
    return pl.pallas_call(
        _normproj_kernel,
        grid=(n // tm, col_steps),
        in_specs=[
            pl.BlockSpec((tm, d), lambda i, j: (i, 0)),
            pl.BlockSpec((1, d), lambda i, j: (0, 0)),
            pl.BlockSpec((d, tn), lambda i, j: (0, j)),
            pl.BlockSpec((1, tn), lambda i, j: (0, j)),
        ],
        out_specs=pl.BlockSpec((tm, tn), lambda i, j: (i, j)),
        out_shape=jax.ShapeDtypeStruct((n, cols), out_dtype),
        scratch_shapes=[pltpu.VMEM((tm, d), BF16)],
        compiler_params=_compiler_params(("parallel", "arbitrary"), vmem),
        name="normproj",
    )(x, gain.reshape(1, d), w.astype(BF16), col_scale.reshape(1, cols))


def _rope_tile(x, cos_t, sin_t):
    lane = lax.broadcasted_iota(jnp.int32, x.shape, 1)
    half = MLA_ROPE // 2
    partner = jnp.where(lane % MLA_ROPE < half, pltpu.roll(x, LANES - half, 1), pltpu.roll(x, half, 1))
    return x * cos_t + partner * sin_t


def _mla_prep_kernel(cq_ref, ckv_ref, kr_ref, gq_ref, gkv_ref, wq_ref, wk_ref, wvt_ref, cos_ref, sin_ref,
                     q_ref, k_ref, vt_ref):
    cos_t = cos_ref[...]
    sin_t = sin_ref[...]
    cqn = _rms(cq_ref[...], gq_ref[...]).astype(BF16)
    ckvn_f32 = _rms(ckv_ref[...], gkv_ref[...])
    ckvn = ckvn_f32.astype(BF16)
    scale = LOG2_E * (MLA_NOPE + MLA_ROPE) ** -0.5
    q = jnp.dot(cqn, wq_ref[...], preferred_element_type=F32)
    kn = jnp.dot(ckvn, wk_ref[...], preferred_element_type=F32)
    v_t = jnp.dot(wvt_ref[...], ckvn_f32.T.astype(BF16), preferred_element_type=F32)
    ones = jnp.ones((BF16_SUBLANES, v_t.shape[1]), BF16)
    lane = lax.broadcasted_iota(jnp.int32, cos_t.shape, 1)
    k_rope = jnp.where(lane < MLA_ROPE, _rope_tile(kr_ref[...], cos_t, sin_t), 0.0).astype(BF16)
    for h in range(MLA_HEADS):
        base = h * MLA_QK_PAD
        q_ref[:, base:base + MLA_NOPE] = (q[:, base:base + MLA_NOPE] * scale).astype(BF16)
        q_rope = _rope_tile(q[:, base + MLA_NOPE:base + MLA_QK_PAD], cos_t, sin_t)
        q_ref[:, base + MLA_NOPE:base + MLA_QK_PAD] = (q_rope * scale).astype(BF16)
        k_ref[:, base:base + MLA_NOPE] = kn[:, h * MLA_NOPE:(h + 1) * MLA_NOPE].astype(BF16)
        k_ref[:, base + MLA_NOPE:base + MLA_QK_PAD] = k_rope
        vt_ref[0, h, :MLA_V, :] = v_t[h * MLA_V:(h + 1) * MLA_V].astype(BF16)
        vt_ref[0, h, MLA_V:, :] = ones


def _mla_prep(z, b, s, g_cq, g_ckv, wq, wk, wv_t, cos_t, sin_t):
    n = z.shape[0]
    tm = Tiles.prep_rows
    tiles_per_seq = s // tm
    qk_cols = MLA_HEADS * MLA_QK_PAD
    v_cols = MLA_HEADS * MLA_V
    vt_rows = MLA_V + BF16_SUBLANES
    row = lambda i: (i, 0)
    fixed = lambda i: (0, 0)
    vmem = (2 * tm * (MLA_Q_RANK + MLA_KV_RANK + 3 * LANES) * 4 + 2 * MLA_Q_RANK * (qk_cols + 2 * v_cols) * 2
            + 2 * tm * (2 * qk_cols + v_cols) * 2 + 3 * tm * qk_cols * 4)
    return pl.pallas_call(
        _mla_prep_kernel,
        grid=(n // tm,),
        in_specs=[
            pl.BlockSpec((tm, MLA_Q_RANK), lambda i: (i, EV_CQ // MLA_Q_RANK)),
            pl.BlockSpec((tm, MLA_KV_RANK), lambda i: (i, EV_CKV // MLA_KV_RANK)),
            pl.BlockSpec((tm, LANES), lambda i: (i, EV_KR // LANES)),
            pl.BlockSpec((1, MLA_Q_RANK), fixed),
            pl.BlockSpec((1, MLA_KV_RANK), fixed),
            pl.BlockSpec((MLA_Q_RANK, qk_cols), fixed),
            pl.BlockSpec((MLA_KV_RANK, v_cols), fixed),
            pl.BlockSpec((v_cols, MLA_KV_RANK), fixed),
            pl.BlockSpec((tm, LANES), row),
            pl.BlockSpec((tm, LANES), row),
        ],
        out_specs=[
            pl.BlockSpec((tm, qk_cols), row),
            pl.BlockSpec((tm, qk_cols), row),
            pl.BlockSpec((1, MLA_HEADS, vt_rows, tm), lambda i: (i // tiles_per_seq, 0, 0, i % tiles_per_seq)),
        ],
        out_shape=[
            jax.ShapeDtypeStruct((n, qk_cols), BF16),
            jax.ShapeDtypeStruct((n, qk_cols), BF16),
            jax.ShapeDtypeStruct((b, MLA_HEADS, vt_rows, s), BF16),
        ],
        compiler_params=_compiler_params(("parallel",), vmem),
        name="mla_prep",
    )(z, z, z, g_cq.reshape(1, -1), g_ckv.reshape(1, -1), wq, wk, wv_t, cos_t, sin_t)


def _mla_attn_kernel(q_ref, k_ref, vt_ref, o_ref, *, n_chunks, tk):
    q = q_ref[0]
    tq = q.shape[0]
    rows = vt_ref.shape[2]

    def scores(c):
        k = k_ref[0, c * tk:(c + 1) * tk, :]
        return lax.dot_general(k, q, (((1,), (1,)), ((), ())), preferred_element_type=F32)

    m = jnp.full((1, tq), NEG_INIT, F32)
    acc = jnp.zeros((rows, tq), F32)
    s = scores(0)
    for c in range(n_chunks):
        s_next = scores(c + 1) if c + 1 < n_chunks else None
        m_new = jnp.maximum(m, jnp.max(s, axis=0, keepdims=True))
        alpha = jnp.exp2(m - m_new)
        p = jnp.exp2(s - m_new).astype(BF16)
        acc = alpha * acc + jnp.dot(vt_ref[0, 0, :, c * tk:(c + 1) * tk], p, preferred_element_type=F32)
        m, s = m_new, s_next
    o_t = acc[:MLA_V] / acc[MLA_V:MLA_V + 1]
    o_ref[0] = o_t.T.astype(o_ref.dtype)


def _mla_attention(q, k, vt):
    b, s, _ = q.shape
    tq, tk = Tiles.mla_q, Tiles.mla_k
    rows = vt.shape[2]
    vmem = (2 * tq * MLA_QK_PAD * 2 + 2 * s * (MLA_QK_PAD + rows) * 2 + 2 * tq * MLA_V * 2
            + 6 * tq * tk * 4)
    return pl.pallas_call(
        functools.partial(_mla_attn_kernel, n_chunks=s // tk, tk=tk),
        grid=(b, MLA_HEADS, s // tq),
        in_specs=[
            pl.BlockSpec((1, tq, MLA_QK_PAD), lambda bi, h, i: (bi, i, h)),
            pl.BlockSpec((1, s, MLA_QK_PAD), lambda bi, h, i: (bi, 0, h)),
            pl.BlockSpec((1, 1, rows, s), lambda bi, h, i: (bi, h, 0, 0)),
        ],
        out_specs=pl.BlockSpec((1, tq, MLA_V), lambda bi, h, i: (bi, i, h)),
        out_shape=jax.ShapeDtypeStruct((b, s, MLA_HEADS * MLA_V), BF16),
        compiler_params=_compiler_params(("parallel", "parallel", "arbitrary"), vmem),
        name="mla_attn",
    )(q, k, vt)


def _log_sigmoid(x):
    return jnp.minimum(x, 0.0) - jnp.log(1.0 + jnp.exp(-jnp.abs(x)))


def _mlstm_chunk(q_ref, k_ref, v_ref, gates, c_ref, n_row, m_st, start, chunk, li_idx, lf_idx, reverse):
    L = chunk
    q = q_ref[0, pl.ds(start, L), :]
    k = k_ref[0, pl.ds(start, L), :]
    v = v_ref[0, pl.ds(start, L), :].astype(BF16)
    li_row = gates[li_idx:li_idx + 1, :]
    lf_row = _log_sigmoid(gates[lf_idx:lf_idx + 1, :])
    t_idx = lax.broadcasted_iota(jnp.int32, (L, L), 0)
    s_idx = lax.broadcasted_iota(jnp.int32, (L, L), 1)
    eye = t_idx == s_idx
    visible = (s_idx >= t_idx) if reverse else (s_idx <= t_idx)

    def to_col(row):
        return jnp.sum(jnp.where(eye, row, 0.0), axis=1, keepdims=True)

    lf_col = to_col(lf_row)
    li_col = to_col(li_row)
    a_col = jnp.sum(jnp.where(visible, lf_row, 0.0), axis=1, keepdims=True)
    visible_t = (t_idx >= s_idx) if reverse else (t_idx <= s_idx)
    a_row = jnp.sum(jnp.where(visible_t, lf_col, 0.0), axis=0, keepdims=True)
    g = jnp.sum(lf_row, axis=1, keepdims=True)

    dmat = jnp.where(visible, a_col - a_row + li_row, -jnp.inf)
    inter = a_col + m_st
    m = jnp.maximum(inter, jnp.max(dmat, axis=1, keepdims=True))
    w_inter = jnp.exp(inter - m)
    q16 = q.astype(BF16)
    qk = (lax.dot_general(q16, k.astype(BF16), (((1,), (1,)), ((), ())), preferred_element_type=F32)
          * jnp.exp(dmat - m))
    num = (w_inter * jnp.dot(q16, c_ref[...].astype(BF16), preferred_element_type=F32)
           + jnp.dot(qk.astype(BF16), v, preferred_element_type=F32))
    den = w_inter * jnp.sum(q * n_row, axis=1, keepdims=True) + jnp.sum(qk, axis=1, keepdims=True)
    h_out = num / jnp.maximum(jnp.abs(den), jnp.exp(-m))

    r_col = g - a_col + li_col
    m_new = jnp.maximum(g + m_st, jnp.max(r_col, axis=0, keepdims=True))
    w_old = jnp.exp(g + m_st - m_new)
    kw = k * jnp.exp(r_col - m_new)
    c_ref[...] = w_old * c_ref[...] + jnp.dot(kw.T.astype(BF16), v, preferred_element_type=F32)
    n_new = w_old * n_row + jnp.sum(kw, axis=0, keepdims=True)
    return h_out, n_new, m_new


def _mlstm_kernel(q_ref, k_ref, v_ref, gate_ref, bias_ref, mo_ref, gn_ref, o_ref,
                  hf_ref, hb_ref, cf_ref, cb_ref, *, seq, chunk, post_rows):
    nc = seq // chunk
    cf_ref[...] = jnp.zeros_like(cf_ref)
    cb_ref[...] = jnp.zeros_like(cb_ref)
    bias = bias_ref[0]

    def body(i, carry):
        n_f, m_f, n_b, m_b = carry
        sf = pl.multiple_of(i * chunk, chunk)
        sb = pl.multiple_of((nc - 1 - i) * chunk, chunk)
        gates_f = gate_ref[0, 0, :, pl.ds(sf, chunk)] + bias
        gates_b = gate_ref[0, 0, :, pl.ds(sb, chunk)] + bias
        h_f, n_f, m_f = _mlstm_chunk(q_ref, k_ref, v_ref, gates_f, cf_ref, n_f, m_f, sf, chunk, 0, 1, False)
        hf_ref[pl.ds(sf, chunk), :] = h_f
        h_b, n_b, m_b = _mlstm_chunk(q_ref, k_ref, v_ref, gates_b, cb_ref, n_b, m_b, sb, chunk, 2, 3, True)
        hb_ref[pl.ds(sb, chunk), :] = h_b
        return n_f, m_f, n_b, m_b

    n0 = jnp.zeros((1, ML_QK), F32)
    m0 = jnp.full((1, 1), NEG_INIT, F32)
    lax.fori_loop(0, nc, body, (n0, m0, n0, m0))

    gn = gn_ref[...]

    def post(i, carry):
        r = pl.multiple_of(i * post_rows, post_rows)
        hm = hf_ref[pl.ds(r, post_rows), :] + hb_ref[pl.ds(r, post_rows), :]
        y = _rms(hm, gn)
        o_ref[0, pl.ds(r, post_rows), :] = (y * jax.nn.sigmoid(mo_ref[0, pl.ds(r, post_rows), :])).astype(o_ref.dtype)
        return carry

    lax.fori_loop(0, seq // post_rows, post, 0)


def _mlstm(z3, gates_t, gate_bias, g_mlstm):
    b, s, _ = z3.shape
    chunk = Tiles.mlstm_chunk
    vmem = (2 * s * (2 * ML_QK + ML_QK + 2 * ML_V) * 4 + 2 * s * ML_V * 2 + 2 * s * ML_V * 4
            + 2 * ML_QK * ML_V * 4 + 16 * chunk * chunk * 4 + (2 << 20))
    return pl.pallas_call(
        functools.partial(_mlstm_kernel, seq=s, chunk=chunk, post_rows=Tiles.mlstm_post_rows),
        grid=(b, ML_HEADS),
        in_specs=[
            pl.BlockSpec((1, s, ML_QK), lambda bi, h: (bi, 0, EV_MQ // ML_QK + h)),
            pl.BlockSpec((1, s, ML_QK), lambda bi, h: (bi, 0, EV_MK // ML_QK + h)),
            pl.BlockSpec((1, s, ML_V), lambda bi, h: (bi, 0, EV_MV // ML_V + h)),
            pl.BlockSpec((1, 1, 4, s), lambda bi, h: (bi, h, 0, 0)),
            pl.BlockSpec((1, 4, 1), lambda bi, h: (h, 0, 0)),
            pl.BlockSpec((1, s, ML_V), lambda bi, h: (bi, 0, EV_MO // ML_V + h)),
            pl.BlockSpec((1, ML_V), lambda bi, h: (0, h)),
        ],
        out_specs=pl.BlockSpec((1, s, ML_V), lambda bi, h: (bi, 0, h)),
        out_shape=jax.ShapeDtypeStruct((b, s, ML_HEADS * ML_V), BF16),
        scratch_shapes=[
            pltpu.VMEM((s, ML_V), F32),
            pltpu.VMEM((s, ML_V), F32),
            pltpu.VMEM((ML_QK, ML_V), F32),
            pltpu.VMEM((ML_QK, ML_V), F32),
        ],
        compiler_params=_compiler_params(("parallel", "parallel"), vmem),
        name="mlstm",
    )(z3, z3, z3, gates_t, gate_bias, z3, g_mlstm.reshape(1, -1))


def _outproj_kernel(*refs, n_lhs):
    x_ref = refs[0]
    a_refs = refs[1:1 + n_lhs]
    w_refs = refs[1 + n_lhs:1 + 2 * n_lhs]
    o_ref = refs[1 + 2 * n_lhs]
    acc = x_ref[...]
    for a_ref, w_ref in zip(a_refs, w_refs):
        acc = acc + jnp.dot(a_ref[...], w_ref[...], preferred_element_type=F32)
    o_ref[...] = acc


def _outproj(x, lhs, weights):
    n, d = x.shape
    tm = Tiles.out_rows
    row = lambda i: (i, 0)
    fixed = lambda i: (0, 0)
    k_total = sum(a.shape[1] for a in lhs)
    vmem = 2 * 2 * tm * d * 4 + 2 * tm * k_total * 2 + 2 * k_total * d * 2 + tm * d * 4
    return pl.pallas_call(
        functools.partial(_outproj_kernel, n_lhs=len(lhs)),
        grid=(n // tm,),
        in_specs=([pl.BlockSpec((tm, d), row)]
                  + [pl.BlockSpec((tm, a.shape[1]), row) for a in lhs]
                  + [pl.BlockSpec(w.shape, fixed) for w in weights]),
        out_specs=pl.BlockSpec((tm, d), row),
        out_shape=jax.ShapeDtypeStruct((n, d), F32),
        compiler_params=_compiler_params(("parallel",), vmem),
        name="outproj",
    )(x, *lhs, *[w.astype(BF16) for w in weights])


def _diff_attn_kernel(slope_ref, q1_ref, q2_ref, k1_ref, k2_ref, vt_ref, pq_ref, pk_ref, lam_ref, gs_ref, o_ref,
                      vta_ref, *, n_chunks, tk, lam_init):
    h = pl.program_id(1)
    slope = slope_ref[h] * LOG2_E
    q1 = q1_ref[0]
    q2 = q2_ref[0]
    tq = q1.shape[0]
    dv = 2 * DA_HEAD
    rows = vta_ref.shape[0]

    @pl.when(pl.program_id(2) == 0)
    def _():
        vta_ref[:dv, :] = vt_ref[0, 0]
        vta_ref[dv:, :] = jnp.ones((rows - dv, vta_ref.shape[1]), BF16)
    pos_q = slope * pq_ref[0].astype(F32)
    nt = (((1,), (1,)), ((), ()))

    def scores(c):
        rows_c = slice(c * tk, (c + 1) * tk)
        dist = jnp.abs(slope * pk_ref[0, rows_c, :].astype(F32) - pos_q)
        s1 = lax.dot_general(k1_ref[0, rows_c, :], q1, nt, preferred_element_type=F32) - dist
        s2 = lax.dot_general(k2_ref[0, rows_c, :], q2, nt, preferred_element_type=F32) - dist
        return s1, s2

    def softmax_step(s, vt, m, acc):
        m_new = jnp.maximum(m, jnp.max(s, axis=0, keepdims=True))
        alpha = jnp.exp2(m - m_new)
        p = jnp.exp2(s - m_new).astype(BF16)
        return m_new, alpha * acc + jnp.dot(vt, p, preferred_element_type=F32)

    m1 = m2 = jnp.full((1, tq), NEG_INIT, F32)
    acc1 = acc2 = jnp.zeros((rows, tq), F32)
    s1, s2 = scores(0)
    for c in range(n_chunks):
        nxt = scores(c + 1) if c + 1 < n_chunks else (None, None)
        vt = vta_ref[:, c * tk:(c + 1) * tk]
        m1, acc1 = softmax_step(s1, vt, m1, acc1)
        m2, acc2 = softmax_step(s2, vt, m2, acc2)
        s1, s2 = nxt

    lam_vec = lam_ref[...]
    lam = (jnp.exp(jnp.sum(lam_vec[0:1] * lam_vec[1:2], axis=1, keepdims=True))
           - jnp.exp(jnp.sum(lam_vec[2:3] * lam_vec[3:4], axis=1, keepdims=True)) + lam_init)
    o_t = acc1[:dv] / acc1[dv:dv + 1] - lam * (acc2[:dv] / acc2[dv:dv + 1])
    o_ref[0] = (_rms(o_t.T, gs_ref[...]) * (1.0 - lam_init)).astype(o_ref.dtype)


def _diff_attention(zq, vt, pos_col, pos_row, slopes, lam_vec, g_sub, lam_init):
    b, s, _ = zq.shape
    tq, tk = Tiles.attn_q, Tiles.attn_k
    dv = 2 * DA_HEAD
    rows = dv + BF16_SUBLANES
    k_base = DA_HEADS * 2
    vmem = (2 * 2 * tq * DA_HEAD * 2 + 3 * s * (2 * DA_HEAD + rows) * 2 + 2 * s * LANES * 4 + 2 * tq * dv * 2
            + 14 * tq * tk * 4)
    return pl.pallas_call(
        functools.partial(_diff_attn_kernel, n_chunks=s // tk, tk=tk, lam_init=lam_init),
        grid=(b, DA_HEADS, s // tq),
        in_specs=[
            pl.BlockSpec(memory_space=pltpu.SMEM),
            pl.BlockSpec((1, tq, DA_HEAD), lambda bi, h, i: (bi, i, 2 * h)),
            pl.BlockSpec((1, tq, DA_HEAD), lambda bi, h, i: (bi, i, 2 * h + 1)),
            pl.BlockSpec((1, s, DA_HEAD), lambda bi, h, i: (bi, 0, k_base + 2 * h)),
            pl.BlockSpec((1, s, DA_HEAD), lambda bi, h, i: (bi, 0, k_base + 2 * h + 1)),
            pl.BlockSpec((1, 1, dv, s), lambda bi, h, i: (bi, h, 0, 0)),
            pl.BlockSpec((1, 1, tq), lambda bi, h, i: (bi, 0, i)),
            pl.BlockSpec((1, s, 1), lambda bi, h, i: (bi, 0, 0)),
            pl.BlockSpec((4, DA_HEAD), lambda bi, h, i: (0, 0)),
            pl.BlockSpec((1, dv), lambda bi, h, i: (0, 0)),
        ],
        out_specs=pl.BlockSpec((1, tq, dv), lambda bi, h, i: (bi, i, h)),
        out_shape=jax.ShapeDtypeStruct((b, s, DA_HEADS * dv), BF16),
        scratch_shapes=[pltpu.VMEM((rows, s), BF16)],
        compiler_params=_compiler_params(("parallel", "parallel", "arbitrary"), vmem),
        name="diff_attn",
    )(slopes, zq, zq, zq, zq, vt, pos_row, pos_col, lam_vec, g_sub.reshape(1, dv))


def _pack_even_w_in(w_in):
    d = w_in.shape[0]
    sizes = (MLA_Q_RANK, MLA_KV_RANK, MLA_ROPE, ML_HEADS * ML_QK, ML_HEADS * ML_QK, ML_HEADS * ML_V,
             ML_HEADS * ML_V, 4 * ML_HEADS)
    c_q, c_kv, k_r, m_q, m_k, m_v, m_o, m_g = jnp.split(w_in, np.cumsum(sizes)[:-1].tolist(), axis=1)
    pad = lambda w, width: jnp.concatenate([w, jnp.zeros((d, width - w.shape[1]), w.dtype)], axis=1)
    packed = jnp.concatenate([c_q, c_kv, m_v, m_o, m_q, m_k, pad(k_r, LANES), pad(m_g, LANES)], axis=1)
    assert packed.shape[1] == EV_COLS
    return packed


def _pack_w_uq(w_uq):
    r = w_uq.shape[0]
    w = w_uq.reshape(r, MLA_HEADS, MLA_NOPE + MLA_ROPE)
    w = jnp.concatenate([w, jnp.zeros((r, MLA_HEADS, MLA_QK_PAD - MLA_NOPE - MLA_ROPE), w.dtype)], axis=2)
    return w.reshape(r, MLA_HEADS * MLA_QK_PAD).astype(BF16)


def _split_w_ukv(w_ukv):
    r = w_ukv.shape[0]
    w = w_ukv.reshape(r, MLA_HEADS, MLA_NOPE + MLA_V)
    wk = w[:, :, :MLA_NOPE].reshape(r, MLA_HEADS * MLA_NOPE)
    wv_t = w[:, :, MLA_NOPE:].reshape(r, MLA_HEADS * MLA_V).T
    return wk.astype(BF16), wv_t.astype(BF16)


def _rope_tables(positions):
    inv_freq = ROPE_THETA ** (-jnp.arange(0, MLA_ROPE, 2, dtype=F32) / MLA_ROPE)
    ang = positions.astype(F32).reshape(-1, 1) * inv_freq
    cos, sin = jnp.cos(ang), jnp.sin(ang)
    zeros = jnp.zeros((ang.shape[0], LANES - MLA_ROPE), F32)
    return jnp.concatenate([cos, cos, zeros], axis=1), jnp.concatenate([-sin, sin, zeros], axis=1)


def _even_mixer(x, b, s, positions, norm, w_in, g_cq, w_uq, g_ckv, w_ukv, b_gates, g_mlstm, w_o):
    n = b * s
    col_scale = jnp.ones((EV_COLS,), F32).at[EV_MK:EV_MK + ML_HEADS * ML_QK].set(ML_QK ** -0.5)
    z = _normproj(x, norm, _pack_even_w_in(w_in), col_scale, F32, Tiles.proj_col_steps_even)

    cos_t, sin_t = _rope_tables(positions)
    wk, wv_t = _split_w_ukv(w_ukv)
    q, k, vt = _mla_prep(z, b, s, g_cq, g_ckv, _pack_w_uq(w_uq), wk, wv_t, cos_t, sin_t)
    a_out = _mla_attention(q.reshape(b, s, -1), k.reshape(b, s, -1), vt)

    z3 = z.reshape(b, s, EV_COLS)
    gates_t = z3[:, :, EV_MG:EV_MG + 4 * ML_HEADS].reshape(b, s, 4, ML_HEADS).transpose(0, 3, 2, 1)
    gate_bias = b_gates.astype(F32).reshape(4, ML_HEADS).T.reshape(ML_HEADS, 4, 1)
    m_out = _mlstm(z3, gates_t, gate_bias, g_mlstm)

    split = MLA_HEADS * MLA_V
    return _outproj(x, [a_out.reshape(n, -1), m_out.reshape(n, -1)], [w_o[:split], w_o[split:]])


def _odd_mixer(x, b, s, positions, norm, w_in, lam_q1, lam_k1, lam_q2, lam_k2, g_sub, w_o, lam_init):
    n = b * s
    width = DA_HEADS * 2 * DA_HEAD
    col_scale = jnp.ones((3 * width,), F32).at[:width].set(LOG2_E * DA_HEAD ** -0.5)
    zq = _normproj(x, norm, w_in, col_scale, BF16, Tiles.proj_col_steps_odd).reshape(b, s, 3 * width)
    slopes = jnp.asarray([2.0 ** (-8.0 * (h + 1) / DA_HEADS) for h in range(DA_HEADS)], dtype=F32)
    lam_vec = jnp.stack([lam_q1, lam_k1, lam_q2, lam_k2]).astype(F32)
    vt = zq[:, :, 2 * width:].reshape(b, s, DA_HEADS, 2 * DA_HEAD).transpose(0, 2, 3, 1)
    o = _diff_attention(zq, vt, positions.reshape(b, s, 1), positions.reshape(b, 1, s), slopes, lam_vec, g_sub,
                        lam_init)
    return _outproj(x, [o.reshape(n, width)], [w_o])


def kernel(x, positions, l0_ffn1_norm, l0_ffn1_w_gu, l0_ffn1_w_down, l0_mix_norm, l0_w_in, l0_g_cq, l0_w_uq, l0_g_ckv, l0_w_ukv, l0_b_gates, l0_g_mlstm, l0_w_o, l0_ffn2_norm, l0_ffn2_w_gu, l0_ffn2_w_down, l1_ffn1_norm, l1_ffn1_w_gu, l1_ffn1_w_down, l1_mix_norm, l1_w_in, l1_lam_q1, l1_lam_k1, l1_lam_q2, l1_lam_k2, l1_g_sub, l1_w_o, l1_ffn2_norm, l1_ffn2_w_gu, l1_ffn2_w_down, final_norm):
    b, s, d = x.shape
    h = x.reshape(b * s, d)
    h = _ffn(h, l0_ffn1_norm, l0_ffn1_w_gu, l0_ffn1_w_down)
    h = _even_mixer(h, b, s, positions, l0_mix_norm, l0_w_in, l0_g_cq, l0_w_uq, l0_g_ckv, l0_w_ukv, l0_b_gates,
                    l0_g_mlstm, l0_w_o)
    h = _ffn(h, l0_ffn2_norm, l0_ffn2_w_gu, l0_ffn2_w_down)
    h = _ffn(h, l1_ffn1_norm, l1_ffn1_w_gu, l1_ffn1_w_down)
    lam_init = 0.8 - 0.6 * math.exp(-0.3 * 1)
    h = _odd_mixer(h, b, s, positions, l1_mix_norm, l1_w_in, l1_lam_q1, l1_lam_k1, l1_lam_q2, l1_lam_k2, l1_g_sub,
                   l1_w_o, lam_init)
    h = _ffn(h, l1_ffn2_norm, l1_ffn2_w_gu, l1_ffn2_w_down, final_gain=final_norm)
    return h.reshape(b, s, d)
```

```python
import functools
import math

import jax
import jax.numpy as jnp
import numpy as np
from jax import lax
from jax.experimental import pallas as pl
from jax.experimental.pallas import tpu as pltpu

F32 = jnp.float32
BF16 = jnp.bfloat16

D_MODEL = 2048
MLA_HEADS = 8
MLA_Q_RANK = 512
MLA_KV_RANK = 512
MLA_NOPE = 128
MLA_ROPE = 64
MLA_V = 128
ROPE_THETA = 10000.0
ML_HEADS = 4
ML_QK = 128
ML_V = 256
DA_HEADS = 8
DA_HEAD = 128
D_FF = 5632
EPS = 1e-6
NEG_INIT = -1e30
LOG2_E = math.log2(math.e)

LANES = 128
BF16_SUBLANES = 16
MLA_QK_PAD = 256
V7X_VMEM_BYTES = 64 * 1024 * 1024

EV_CQ = 0
EV_CKV = EV_CQ + MLA_Q_RANK
EV_MV = EV_CKV + MLA_KV_RANK
EV_MO = EV_MV + ML_HEADS * ML_V
EV_MQ = EV_MO + ML_HEADS * ML_V
EV_MK = EV_MQ + ML_HEADS * ML_QK
EV_KR = EV_MK + ML_HEADS * ML_QK
EV_MG = EV_KR + LANES
EV_COLS = EV_MG + LANES
assert EV_MV % ML_V == 0 and EV_MO % ML_V == 0 and EV_MQ % ML_QK == 0 and EV_MK % ML_QK == 0


class Tiles:
    ffn_rows = 1024
    ffn_cols = 512
    proj_rows = 512
    proj_col_steps_even = 2
    proj_col_steps_odd = 3
    prep_rows = 512
    out_rows = 512
    mla_q = 2048
    mla_k = 512
    mla_groups = 1
    attn_q = 1024
    attn_k = 512
    mlstm_chunk = 256
    mlstm_post_rows = 512


def _compiler_params(semantics, vmem_bytes):
    return pltpu.CompilerParams(dimension_semantics=semantics,
                                vmem_limit_bytes=int(min(vmem_bytes, V7X_VMEM_BYTES - (6 << 20))))


def _rms(x, gain):
    y = x * lax.rsqrt(jnp.mean(x * x, axis=-1, keepdims=True) + EPS)
    return y * gain


def _ffn_kernel(*refs, n_ff_steps, final):
    if final:
        x_ref, g_ref, wg_ref, wu_ref, wd_ref, gf_ref, o_ref, xn_ref = refs
    else:
        x_ref, g_ref, wg_ref, wu_ref, wd_ref, o_ref, xn_ref = refs
    j = pl.program_id(1)

    @pl.when(j == 0)
    def _():
        x = x_ref[...]
        xn_ref[...] = _rms(x, g_ref[...]).astype(BF16)
        o_ref[...] = x

    xn = xn_ref[...]
    gate = jnp.dot(xn, wg_ref[...], preferred_element_type=F32)
    up = jnp.dot(xn, wu_ref[...], preferred_element_type=F32)
    act = (0.5 * (gate * jax.nn.sigmoid(gate)) * up).astype(BF16)
    o_ref[...] += jnp.dot(act, wd_ref[...], preferred_element_type=F32)

    if final:
        @pl.when(j == n_ff_steps - 1)
        def _():
            o_ref[...] = _rms(o_ref[...], gf_ref[...])


def _ffn(x, gain, w_gu, w_down, final_gain=None):
    n, d = x.shape
    tm, tf = Tiles.ffn_rows, Tiles.ffn_cols
    nf = D_FF // tf
    wgu = w_gu.astype(BF16)
    wd = w_down.astype(BF16)
    final = final_gain is not None
    in_specs = [
        pl.BlockSpec((tm, d), lambda i, j: (i, 0)),
        pl.BlockSpec((1, d), lambda i, j: (0, 0)),
        pl.BlockSpec((d, tf), lambda i, j: (0, j)),
        pl.BlockSpec((d, tf), lambda i, j: (0, j + nf)),
        pl.BlockSpec((tf, d), lambda i, j: (j, 0)),
    ]
    args = [x, gain.reshape(1, d), wgu, wgu, wd]
    if final:
        in_specs.append(pl.BlockSpec((1, d), lambda i, j: (0, 0)))
        args.append(final_gain.reshape(1, d))
    vmem = (2 * 2 * tm * d * 4 + tm * d * 2 + 2 * 3 * d * tf * 2 + 4 * tm * tf * 4 + tm * d * 4)
    return pl.pallas_call(
        functools.partial(_ffn_kernel, n_ff_steps=nf, final=final),
        grid=(n // tm, nf),
        in_specs=in_specs,
        out_specs=pl.BlockSpec((tm, d), lambda i, j: (i, 0)),
        out_shape=jax.ShapeDtypeStruct((n, d), F32),
        scratch_shapes=[pltpu.VMEM((tm, d), BF16)],
        compiler_params=_compiler_params(("parallel", "arbitrary"), vmem),
        name="ffn",
    )(*args)


def _normproj_kernel(x_ref, g_ref, w_ref, cs_ref, o_ref, xn_ref):
    @pl.when(pl.program_id(1) == 0)
    def _():
        xn_ref[...] = _rms(x_ref[...], g_ref[...]).astype(BF16)

    acc = jnp.dot(xn_ref[...], w_ref[...], preferred_element_type=F32)
    o_ref[...] = (acc * cs_ref[...]).astype(o_ref.dtype)


def _normproj(x, gain, w, col_scale, out_dtype, col_steps):
    n, d = x.shape
    cols = w.shape[1]
    tm, tn = Tiles.proj_rows, cols // col_steps
    assert tn * col_steps == cols and tn % LANES == 0
    out_bytes = jnp.dtype(out_dtype).itemsize
    vmem = 2 * tm * d * 4 + tm * d * 2 + 2 * d * tn * 2 + 2 * tm * tn * out_bytes + 2 * tm * tn * 4
    return pl.pallas_call(
        _normproj_kernel,
        grid=(n // tm, col_steps),
        in_specs=[
            pl.BlockSpec((tm, d), lambda i, j: (i, 0)),
            pl.BlockSpec((1, d), lambda i, j: (0, 0)),
            pl.BlockSpec((d, tn), lambda i, j: (0, j)),
            pl.BlockSpec((1, tn), lambda i, j: (0, j)),
        ],
        out_specs=pl.BlockSpec((tm, tn), lambda i, j: (i, j)),
        out_shape=jax.ShapeDtypeStruct((n, cols), out_dtype),
        scratch_shapes=[pltpu.VMEM((tm, d), BF16)],
        compiler_params=_compiler_params(("parallel", "arbitrary"), vmem),
        name="normproj",
    )(x, gain.reshape(1, d), w.astype(BF16), col_scale.reshape(1, cols))


def _rope_tile(x, cos_t, sin_t):
    lane = lax.broadcasted_iota(jnp.int32, x.shape, 1)
    half = MLA_ROPE // 2
    partner = jnp.where(lane % MLA_ROPE < half, pltpu.roll(x, LANES - half, 1), pltpu.roll(x, half, 1))
    return x * cos_t + partner * sin_t


def _mla_prep_kernel(cq_ref, ckv_ref, kr_ref, gq_ref, gkv_ref, wq_ref, wk_ref, wvt_ref, cos_ref, sin_ref,
                     q_ref, k_ref, vt_ref):
    cos_t = cos_ref[...]
    sin_t = sin_ref[...]
    cqn = _rms(cq_ref[...], gq_ref[...]).astype(BF16)
    ckvn_f32 = _rms(ckv_ref[...], gkv_ref[...])
    ckvn = ckvn_f32.astype(BF16)
    scale = LOG2_E * (MLA_NOPE + MLA_ROPE) ** -0.5
    q = jnp.dot(cqn, wq_ref[...], preferred_element_type=F32)
    kn = jnp.dot(ckvn, wk_ref[...], preferred_element_type=F32)
    v_t = jnp.dot(wvt_ref[...], ckvn_f32.T.astype(BF16), preferred_element_type=F32)
    ones = jnp.ones((BF16_SUBLANES, v_t.shape[1]), BF16)
    lane = lax.broadcasted_iota(jnp.int32, cos_t.shape, 1)
    k_rope = jnp.where(lane < MLA_ROPE, _rope_tile(kr_ref[...], cos_t, sin_t), 0.0).astype(BF16)
    for h in range(MLA_HEADS):
        base = h * MLA_QK_PAD
        q_ref[:, base:base + MLA_NOPE] = (q[:, base:base + MLA_NOPE] * scale).astype(BF16)
        q_rope = _rope_tile(q[:, base + MLA_NOPE:base + MLA_QK_PAD], cos_t, sin_t)
        q_ref[:, base + MLA_NOPE:base + MLA_QK_PAD] = (q_rope * scale).astype(BF16)
        k_ref[:, base:base + MLA_NOPE] = kn[:, h * MLA_NOPE:(h + 1) * MLA_NOPE].astype(BF16)
        k_ref[:, base + MLA_NOPE:base + MLA_QK_PAD] = k_rope
        vt_ref[0, h, :MLA_V, :] = v_t[h * MLA_V:(h + 1) * MLA_V].astype(BF16)
        vt_ref[0, h, MLA_V:, :] = ones


def _mla_prep(z, b, s, g_cq, g_ckv, wq, wk, wv_t, cos_t, sin_t):
    n = z.shape[0]
    tm = Tiles.prep_rows
    tiles_per_seq = s // tm
    qk_cols = MLA_HEADS * MLA_QK_PAD
    v_cols = MLA_HEADS * MLA_V
    vt_rows = MLA_V + BF16_SUBLANES
    row = lambda i: (i, 0)
    fixed = lambda i: (0, 0)
    vmem = (2 * tm * (MLA_Q_RANK + MLA_KV_RANK + 3 * LANES) * 4 + 2 * MLA_Q_RANK * (qk_cols + 2 * v_cols) * 2
            + 2 * tm * (2 * qk_cols + v_cols) * 2 + 3 * tm * qk_cols * 4)
    return pl.pallas_call(
        _mla_prep_kernel,
        grid=(n // tm,),
        in_specs=[
            pl.BlockSpec((tm, MLA_Q_RANK), lambda i: (i, EV_CQ // MLA_Q_RANK)),
            pl.BlockSpec((tm, MLA_KV_RANK), lambda i: (i, EV_CKV // MLA_KV_RANK)),
            pl.BlockSpec((tm, LANES), lambda i: (i, EV_KR // LANES)),
            pl.BlockSpec((1, MLA_Q_RANK), fixed),
            pl.BlockSpec((1, MLA_KV_RANK), fixed),
            pl.BlockSpec((MLA_Q_RANK, qk_cols), fixed),
            pl.BlockSpec((MLA_KV_RANK, v_cols), fixed),
            pl.BlockSpec((v_cols, MLA_KV_RANK), fixed),
            pl.BlockSpec((tm, LANES), row),
            pl.BlockSpec((tm, LANES), row),
        ],
        out_specs=[
            pl.BlockSpec((tm, qk_cols), row),
            pl.BlockSpec((tm, qk_cols), row),
            pl.BlockSpec((1, MLA_HEADS, vt_rows, tm), lambda i: (i // tiles_per_seq, 0, 0, i % tiles_per_seq)),
        ],
        out_shape=[
            jax.ShapeDtypeStruct((n, qk_cols), BF16),
            jax.ShapeDtypeStruct((n, qk_cols), BF16),
            jax.ShapeDtypeStruct((b, MLA_HEADS, vt_rows, s), BF16),
        ],
        compiler_params=_compiler_params(("parallel",), vmem),
        name="mla_prep",
    )(z, z, z, g_cq.reshape(1, -1), g_ckv.reshape(1, -1), wq, wk, wv_t, cos_t, sin_t)


def _mla_attn_kernel(q_ref, k_ref, vt_ref, o_ref, *, n_chunks, tk, n_groups):
    tq = q_ref.shape[1]
    rows = vt_ref.shape[2]
    th = tq // n_groups
    qs = [q_ref[0, g * th:(g + 1) * th, :] for g in range(n_groups)]

    def scores(g, c):
        k = k_ref[0, c * tk:(c + 1) * tk, :]
        return lax.dot_general(k, qs[g], (((1,), (1,)), ((), ())), preferred_element_type=F32)

    m = [jnp.full((1, th), NEG_INIT, F32) for _ in range(n_groups)]
    acc = [jnp.zeros((rows, th), F32) for _ in range(n_groups)]
    s = [scores(g, 0) for g in range(n_groups)]
    for c in range(n_chunks):
        vt = vt_ref[0, 0, :, c * tk:(c + 1) * tk]
        for g in range(n_groups):
            s_next = scores(g, c + 1) if c + 1 < n_chunks else None
            m_new = jnp.maximum(m[g], jnp.max(s[g], axis=0, keepdims=True))
            alpha = jnp.exp2(m[g] - m_new)
            p = jnp.exp2(s[g] - m_new).astype(BF16)
            acc[g] = alpha * acc[g] + jnp.dot(vt, p, preferred_element_type=F32)
            m[g], s[g] = m_new, s_next
    for g in range(n_groups):
        o_t = acc[g][:MLA_V] / acc[g][MLA_V:MLA_V + 1]
        o_ref[0, g * th:(g + 1) * th, :] = o_t.T.astype(o_ref.dtype)


def _mla_attention(q, k, vt):
    b, s, _ = q.shape
    tq, tk = Tiles.mla_q, Tiles.mla_k
    rows = vt.shape[2]
    vmem = (2 * tq * MLA_QK_PAD * 2 + 2 * s * (MLA_QK_PAD + rows) * 2 + 2 * tq * MLA_V * 2
            + 6 * tq * tk * 4)
    return pl.pallas_call(
        functools.partial(_mla_attn_kernel, n_chunks=s // tk, tk=tk, n_groups=Tiles.mla_groups),
        grid=(b, MLA_HEADS, s // tq),
        in_specs=[
            pl.BlockSpec((1, tq, MLA_QK_PAD), lambda bi, h, i: (bi, i, h)),
            pl.BlockSpec((1, s, MLA_QK_PAD), lambda bi, h, i: (bi, 0, h)),
            pl.BlockSpec((1, 1, rows, s), lambda bi, h, i: (bi, h, 0, 0)),
        ],
        out_specs=pl.BlockSpec((1, tq, MLA_V), lambda bi, h, i: (bi, i, h)),
        out_shape=jax.ShapeDtypeStruct((b, s, MLA_HEADS * MLA_V), BF16),
        compiler_params=_compiler_params(("parallel", "parallel", "arbitrary"), vmem),
        name="mla_attn",
    )(q, k, vt)


def _log_sigmoid(x):
    return jnp.minimum(x, 0.0) - jnp.log(1.0 + jnp.exp(-jnp.abs(x)))


def _mlstm_chunk(q_ref, k_ref, v_ref, gates, c_ref, n_row, m_st, start, chunk, li_idx, lf_idx, reverse):
    L = chunk
    q = q_ref[0, pl.ds(start, L), :]
    k = k_ref[0, pl.ds(start, L), :]
    v = v_ref[0, pl.ds(start, L), :].astype(BF16)
    li_row = gates[li_idx:li_idx + 1, :]
    lf_row = _log_sigmoid(gates[lf_idx:lf_idx + 1, :])
    t_idx = lax.broadcasted_iota(jnp.int32, (L, L), 0)
    s_idx = lax.broadcasted_iota(jnp.int32, (L, L), 1)
    eye = t_idx == s_idx
    visible = (s_idx >= t_idx) if reverse else (s_idx <= t_idx)

    def to_col(row):
        return jnp.sum(jnp.where(eye, row, 0.0), axis=1, keepdims=True)

    lf_col = to_col(lf_row)
    li_col = to_col(li_row)
    a_col = jnp.sum(jnp.where(visible, lf_row, 0.0), axis=1, keepdims=True)
    visible_t = (t_idx >= s_idx) if reverse else (t_idx <= s_idx)
    a_row = jnp.sum(jnp.where(visible_t, lf_col, 0.0), axis=0, keepdims=True)
    g = jnp.sum(lf_row, axis=1, keepdims=True)

    dmat = jnp.where(visible, a_col - a_row + li_row, -jnp.inf)
    inter = a_col + m_st
    m = jnp.maximum(inter, jnp.max(dmat, axis=1, keepdims=True))
    w_inter = jnp.exp(inter - m)
    q16 = q.astype(BF16)
    qk = (lax.dot_general(q16, k.astype(BF16), (((1,), (1,)), ((), ())), preferred_element_type=F32)
          * jnp.exp(dmat - m))
    num = (w_inter * jnp.dot(q16, c_ref[...].astype(BF16), preferred_element_type=F32)
           + jnp.dot(qk.astype(BF16), v, preferred_element_type=F32))
    den = w_inter * jnp.sum(q * n_row, axis=1, keepdims=True) + jnp.sum(qk, axis=1, keepdims=True)
    h_out = num / jnp.maximum(jnp.abs(den), jnp.exp(-m))

    r_col = g - a_col + li_col
    m_new = jnp.maximum(g + m_st, jnp.max(r_col, axis=0, keepdims=True))
    w_old = jnp.exp(g + m_st - m_new)
    kw = k * jnp.exp(r_col - m_new)
    c_ref[...] = w_old * c_ref[...] + jnp.dot(kw.T.astype(BF16), v, preferred_element_type=F32)
    n_new = w_old * n_row + jnp.sum(kw, axis=0, keepdims=True)
    return h_out, n_new, m_new


def _mlstm_kernel(q_ref, k_ref, v_ref, gate_ref, bias_ref, mo_ref, gn_ref, o_ref,
                  hf_ref, hb_ref, cf_ref, cb_ref, *, seq, chunk, post_rows):
    nc = seq // chunk
    cf_ref[...] = jnp.zeros_like(cf_ref)
    cb_ref[...] = jnp.zeros_like(cb_ref)
    bias = bias_ref[0]

    def body(i, carry):
        n_f, m_f, n_b, m_b = carry
        sf = pl.multiple_of(i * chunk, chunk)
        sb = pl.multiple_of((nc - 1 - i) * chunk, chunk)
        gates_f = gate_ref[0, 0, :, pl.ds(sf, chunk)] + bias
        gates_b = gate_ref[0, 0, :, pl.ds(sb, chunk)] + bias
        h_f, n_f, m_f = _mlstm_chunk(q_ref, k_ref, v_ref, gates_f, cf_ref, n_f, m_f, sf, chunk, 0, 1, False)
        hf_ref[pl.ds(sf, chunk), :] = h_f
        h_b, n_b, m_b = _mlstm_chunk(q_ref, k_ref, v_ref, gates_b, cb_ref, n_b, m_b, sb, chunk, 2, 3, True)
        hb_ref[pl.ds(sb, chunk), :] = h_b
        return n_f, m_f, n_b, m_b

    n0 = jnp.zeros((1, ML_QK), F32)
    m0 = jnp.full((1, 1), NEG_INIT, F32)
    lax.fori_loop(0, nc, body, (n0, m0, n0, m0))

    gn = gn_ref[...]

    def post(i, carry):
        r = pl.multiple_of(i * post_rows, post_rows)
        hm = hf_ref[pl.ds(r, post_rows), :] + hb_ref[pl.ds(r, post_rows), :]
        y = _rms(hm, gn)
        o_ref[0, pl.ds(r, post_rows), :] = (y * jax.nn.sigmoid(mo_ref[0, pl.ds(r, post_rows), :])).astype(o_ref.dtype)
        return carry

    lax.fori_loop(0, seq // post_rows, post, 0)


def _mlstm(z3, gates_t, gate_bias, g_mlstm):
    b, s, _ = z3.shape
    chunk = Tiles.mlstm_chunk
    vmem = (2 * s * (2 * ML_QK + ML_QK + 2 * ML_V) * 4 + 2 * s * ML_V * 2 + 2 * s * ML_V * 4
            + 2 * ML_QK * ML_V * 4 + 16 * chunk * chunk * 4 + (2 << 20))
    return pl.pallas_call(
        functools.partial(_mlstm_kernel, seq=s, chunk=chunk, post_rows=Tiles.mlstm_post_rows),
        grid=(b, ML_HEADS),
        in_specs=[
            pl.BlockSpec((1, s, ML_QK), lambda bi, h: (bi, 0, EV_MQ // ML_QK + h)),
            pl.BlockSpec((1, s, ML_QK), lambda bi, h: (bi, 0, EV_MK // ML_QK + h)),
            pl.BlockSpec((1, s, ML_V), lambda bi, h: (bi, 0, EV_MV // ML_V + h)),
            pl.BlockSpec((1, 1, 4, s), lambda bi, h: (bi, h, 0, 0)),
            pl.BlockSpec((1, 4, 1), lambda bi, h: (h, 0, 0)),
            pl.BlockSpec((1, s, ML_V), lambda bi, h: (bi, 0, EV_MO // ML_V + h)),
            pl.BlockSpec((1, ML_V), lambda bi, h: (0, h)),
        ],
        out_specs=pl.BlockSpec((1, s, ML_V), lambda bi, h: (bi, 0, h)),
        out_shape=jax.ShapeDtypeStruct((b, s, ML_HEADS * ML_V), BF16),
        scratch_shapes=[
            pltpu.VMEM((s, ML_V), F32),
            pltpu.VMEM((s, ML_V), F32),
            pltpu.VMEM((ML_QK, ML_V), F32),
            pltpu.VMEM((ML_QK, ML_V), F32),
        ],
        compiler_params=_compiler_params(("parallel", "parallel"), vmem),
        name="mlstm",
    )(z3, z3, z3, gates_t, gate_bias, z3, g_mlstm.reshape(1, -1))


def _outproj_kernel(*refs, n_lhs):
    x_ref = refs[0]
    a_refs = refs[1:1 + n_lhs]
    w_refs = refs[1 + n_lhs:1 + 2 * n_lhs]
    o_ref = refs[1 + 2 * n_lhs]
    acc = x_ref[...]
    for a_ref, w_ref in zip(a_refs, w_refs):
        acc = acc + jnp.dot(a_ref[...], w_ref[...], preferred_element_type=F32)
    o_ref[...] = acc


def _outproj(x, lhs, weights):
    n, d = x.shape
    tm = Tiles.out_rows
    row = lambda i: (i, 0)
    fixed = lambda i: (0, 0)
    k_total = sum(a.shape[1] for a in lhs)
    vmem = 2 * 2 * tm * d * 4 + 2 * tm * k_total * 2 + 2 * k_total * d * 2 + tm * d * 4
    return pl.pallas_call(
        functools.partial(_outproj_kernel, n_lhs=len(lhs)),
        grid=(n // tm,),
        in_specs=([pl.BlockSpec((tm, d), row)]
                  + [pl.BlockSpec((tm, a.shape[1]), row) for a in lhs]
                  + [pl.BlockSpec(w.shape, fixed) for w in weights]),
        out_specs=pl.BlockSpec((tm, d), row),
        out_shape=jax.ShapeDtypeStruct((n, d), F32),
        compiler_params=_compiler_params(("parallel",), vmem),
        name="outproj",
    )(x, *lhs, *[w.astype(BF16) for w in weights])


def _diff_attn_kernel(slope_ref, q1_ref, q2_ref, k1_ref, k2_ref, vt_ref, pq_ref, pk_ref, lam_ref, gs_ref, o_ref,
                      vta_ref, *, n_chunks, tk, lam_init):
    h = pl.program_id(1)
    slope = slope_ref[h] * LOG2_E
    q1 = q1_ref[0]
    q2 = q2_ref[0]
    tq = q1.shape[0]
    dv = 2 * DA_HEAD
    rows = vta_ref.shape[0]

    @pl.when(pl.program_id(2) == 0)
    def _():
        vta_ref[:dv, :] = vt_ref[0, 0]
        vta_ref[dv:, :] = jnp.ones((rows - dv, vta_ref.shape[1]), BF16)
    pos_q = slope * pq_ref[0].astype(F32)
    nt = (((1,), (1,)), ((), ()))

    def scores(c):
        rows_c = slice(c * tk, (c + 1) * tk)
        dist = jnp.abs(slope * pk_ref[0, rows_c, :].astype(F32) - pos_q)
        s1 = lax.dot_general(k1_ref[0, rows_c, :], q1, nt, preferred_element_type=F32) - dist
        s2 = lax.dot_general(k2_ref[0, rows_c, :], q2, nt, preferred_element_type=F32) - dist
        return s1, s2

    def softmax_step(s, vt, m, acc):
        m_new = jnp.maximum(m, jnp.max(s, axis=0, keepdims=True))
        alpha = jnp.exp2(m - m_new)
        p = jnp.exp2(s - m_new).astype(BF16)
        return m_new, alpha * acc + jnp.dot(vt, p, preferred_element_type=F32)

    m1 = m2 = jnp.full((1, tq), NEG_INIT, F32)
    acc1 = acc2 = jnp.zeros((rows, tq), F32)
    s1, s2 = scores(0)
    for c in range(n_chunks):
        nxt = scores(c + 1) if c + 1 < n_chunks else (None, None)
        vt = vta_ref[:, c * tk:(c + 1) * tk]
        m1, acc1 = softmax_step(s1, vt, m1, acc1)
        m2, acc2 = softmax_step(s2, vt, m2, acc2)
        s1, s2 = nxt

    lam_vec = lam_ref[...]
    lam = (jnp.exp(jnp.sum(lam_vec[0:1] * lam_vec[1:2], axis=1, keepdims=True))
           - jnp.exp(jnp.sum(lam_vec[2:3] * lam_vec[3:4], axis=1, keepdims=True)) + lam_init)
    o_t = acc1[:dv] / acc1[dv:dv + 1] - lam * (acc2[:dv] / acc2[dv:dv + 1])
    o_ref[0] = (_rms(o_t.T, gs_ref[...]) * (1.0 - lam_init)).astype(o_ref.dtype)


def _diff_attention(zq, vt, pos_col, pos_row, slopes, lam_vec, g_sub, lam_init):
    b, s, _ = zq.shape
    tq, tk = Tiles.attn_q, Tiles.attn_k
    dv = 2 * DA_HEAD
    rows = dv + BF16_SUBLANES
    k_base = DA_HEADS * 2
    vmem = (2 * 2 * tq * DA_HEAD * 2 + 3 * s * (2 * DA_HEAD + rows) * 2 + 2 * s * LANES * 4 + 2 * tq * dv * 2
            + 14 * tq * tk * 4)
    return pl.pallas_call(
        functools.partial(_diff_attn_kernel, n_chunks=s // tk, tk=tk, lam_init=lam_init),
        grid=(b, DA_HEADS, s // tq),
        in_specs=[
            pl.BlockSpec(memory_space=pltpu.SMEM),
            pl.BlockSpec((1, tq, DA_HEAD), lambda bi, h, i: (bi, i, 2 * h)),
            pl.BlockSpec((1, tq, DA_HEAD), lambda bi, h, i: (bi, i, 2 * h + 1)),
            pl.BlockSpec((1, s, DA_HEAD), lambda bi, h, i: (bi, 0, k_base + 2 * h)),
            pl.BlockSpec((1, s, DA_HEAD), lambda bi, h, i: (bi, 0, k_base + 2 * h + 1)),
            pl.BlockSpec((1, 1, dv, s), lambda bi, h, i: (bi, h, 0, 0)),
            pl.BlockSpec((1, 1, tq), lambda bi, h, i: (bi, 0, i)),
            pl.BlockSpec((1, s, 1), lambda bi, h, i: (bi, 0, 0)),
            pl.BlockSpec((4, DA_HEAD), lambda bi, h, i: (0, 0)),
            pl.BlockSpec((1, dv), lambda bi, h, i: (0, 0)),
        ],
        out_specs=pl.BlockSpec((1, tq, dv), lambda bi, h, i: (bi, i, h)),
        out_shape=jax.ShapeDtypeStruct((b, s, DA_HEADS * dv), BF16),
        scratch_shapes=[pltpu.VMEM((rows, s), BF16)],
        compiler_params=_compiler_params(("parallel", "parallel", "arbitrary"), vmem),
        name="diff_attn",
    )(slopes, zq, zq, zq, zq, vt, pos_row, pos_col, lam_vec, g_sub.reshape(1, dv))


def _pack_even_w_in(w_in):
    d = w_in.shape[0]
    sizes = (MLA_Q_RANK, MLA_KV_RANK, MLA_ROPE, ML_HEADS * ML_QK, ML_HEADS * ML_QK, ML_HEADS * ML_V,
             ML_HEADS * ML_V, 4 * ML_HEADS)
    c_q, c_kv, k_r, m_q, m_k, m_v, m_o, m_g = jnp.split(w_in, np.cumsum(sizes)[:-1].tolist(), axis=1)
    pad = lambda w, width: jnp.concatenate([w, jnp.zeros((d, width - w.shape[1]), w.dtype)], axis=1)
    packed = jnp.concatenate([c_q, c_kv, m_v, m_o, m_q, m_k, pad(k_r, LANES), pad(m_g, LANES)], axis=1)
    assert packed.shape[1] == EV_COLS
    return packed


def _pack_w_uq(w_uq):
    r = w_uq.shape[0]
    w = w_uq.reshape(r, MLA_HEADS, MLA_NOPE + MLA_ROPE)
    w = jnp.concatenate([w, jnp.zeros((r, MLA_HEADS, MLA_QK_PAD - MLA_NOPE - MLA_ROPE), w.dtype)], axis=2)
    return w.reshape(r, MLA_HEADS * MLA_QK_PAD).astype(BF16)


def _split_w_ukv(w_ukv):
    r = w_ukv.shape[0]
    w = w_ukv.reshape(r, MLA_HEADS, MLA_NOPE + MLA_V)
    wk = w[:, :, :MLA_NOPE].reshape(r, MLA_HEADS * MLA_NOPE)
    wv_t = w[:, :, MLA_NOPE:].reshape(r, MLA_HEADS * MLA_V).T
    return wk.astype(BF16), wv_t.astype(BF16)


def _rope_tables(positions):
    inv_freq = ROPE_THETA ** (-jnp.arange(0, MLA_ROPE, 2, dtype=F32) / MLA_ROPE)
    ang = positions.astype(F32).reshape(-1, 1) * inv_freq
    cos, sin = jnp.cos(ang), jnp.sin(ang)
    zeros = jnp.zeros((ang.shape[0], LANES - MLA_ROPE), F32)
    return jnp.concatenate([cos, cos, zeros], axis=1), jnp.concatenate([-sin, sin, zeros], axis=1)


def _even_mixer(x, b, s, positions, norm, w_in, g_cq, w_uq, g_ckv, w_ukv, b_gates, g_mlstm, w_o):
    n = b * s
    col_scale = jnp.ones((EV_COLS,), F32).at[EV_MK:EV_MK + ML_HEADS * ML_QK].set(ML_QK ** -0.5)
    z = _normproj(x, norm, _pack_even_w_in(w_in), col_scale, F32, Tiles.proj_col_steps_even)

    cos_t, sin_t = _rope_tables(positions)
    wk, wv_t = _split_w_ukv(w_ukv)
    q, k, vt = _mla_prep(z, b, s, g_cq, g_ckv, _pack_w_uq(w_uq), wk, wv_t, cos_t, sin_t)
    a_out = _mla_attention(q.reshape(b, s, -1), k.reshape(b, s, -1), vt)

    z3 = z.reshape(b, s, EV_COLS)
    gates_t = z3[:, :, EV_MG:EV_MG + 4 * ML_HEADS].reshape(b, s, 4, ML_HEADS).transpose(0, 3, 2, 1)
    gate_bias = b_gates.astype(F32).reshape(4, ML_HEADS).T.reshape(ML_HEADS, 4, 1)
    m_out = _mlstm(z3, gates_t, gate_bias, g_mlstm)

    split = MLA_HEADS * MLA_V
    return _outproj(x, [a_out.reshape(n, -1), m_out.reshape(n, -1)], [w_o[:split], w_o[split:]])


def _odd_mixer(x, b, s, positions, norm, w_in, lam_q1, lam_k1, lam_q2, lam_k2, g_sub, w_o, lam_init):
    n = b * s
    width = DA_HEADS * 2 * DA_HEAD
    col_scale = jnp.ones((3 * width,), F32).at[:width].set(LOG2_E * DA_HEAD ** -0.5)
    zq = _normproj(x, norm, w_in, col_scale, BF16, Tiles.proj_col_steps_odd).reshape(b, s, 3 * width)
    slopes = jnp.asarray([2.0 ** (-8.0 * (h + 1) / DA_HEADS) for h in range(DA_HEADS)], dtype=F32)
    lam_vec = jnp.stack([lam_q1, lam_k1, lam_q2, lam_k2]).astype(F32)
    vt = zq[:, :, 2 * width:].reshape(b, s, DA_HEADS, 2 * DA_HEAD).transpose(0, 2, 3, 1)
    o = _diff_attention(zq, vt, positions.reshape(b, s, 1), positions.reshape(b, 1, s), slopes, lam_vec, g_sub,
                        lam_init)
    return _outproj(x, [o.reshape(n, width)], [w_o])


def kernel(x, positions, l0_ffn1_norm, l0_ffn1_w_gu, l0_ffn1_w_down, l0_mix_norm, l0_w_in, l0_g_cq, l0_w_uq, l0_g_ckv, l0_w_ukv, l0_b_gates, l0_g_mlstm, l0_w_o, l0_ffn2_norm, l0_ffn2_w_gu, l0_ffn2_w_down, l1_ffn1_norm, l1_ffn1_w_gu, l1_ffn1_w_down, l1_mix_norm, l1_w_in, l1_lam_q1, l1_lam_k1, l1_lam_q2, l1_lam_k2, l1_g_sub, l1_w_o, l1_ffn2_norm, l1_ffn2_w_gu, l1_ffn2_w_down, final_norm):
    b, s, d = x.shape
    h = x.reshape(b * s, d)
    h = _ffn(h, l0_ffn1_norm, l0_ffn1_w_gu, l0_ffn1_w_down)
    h = _even_mixer(h, b, s, positions, l0_mix_norm, l0_w_in, l0_g_cq, l0_w_uq, l0_g_ckv, l0_w_ukv, l0_b_gates,
                    l0_g_mlstm, l0_w_o)
    h = _ffn(h, l0_ffn2_norm, l0_ffn2_w_gu, l0_ffn2_w_down)
    h = _ffn(h, l1_ffn1_norm, l1_ffn1_w_gu, l1_ffn1_w_down)
    lam_init = 0.8 - 0.6 * math.exp(-0.3 * 1)
    h = _odd_mixer(h, b, s, positions, l1_mix_norm, l1_w_in, l1_lam_q1, l1_lam_k1, l1_lam_q2, l1_lam_k2, l1_g_sub,
                   l1_w_o, lam_init)
    h = _ffn(h, l1_ffn2_norm, l1_ffn2_w_gu, l1_ffn2_w_down, final_gain=final_norm)
    return h.reshape(b, s, d)
```

```python
import functools
import math

import jax
import jax.numpy as jnp
import numpy as np
from jax import lax
from jax.experimental import pallas as pl
from jax.experimental.pallas import tpu as pltpu

F32 = jnp.float32
BF16 = jnp.bfloat16

D_MODEL = 2048
MLA_HEADS = 8
MLA_Q_RANK = 512
MLA_KV_RANK = 512
MLA_NOPE = 128
MLA_ROPE = 64
MLA_V = 128
ROPE_THETA = 10000.0
ML_HEADS = 4
ML_QK = 128
ML_V = 256
DA_HEADS = 8
DA_HEAD = 128
D_FF = 5632
EPS = 1e-6
NEG_INIT = -1e30
LOG2_E = math.log2(math.e)

LANES = 128
BF16_SUBLANES = 16
MLA_QK_PAD = 256
V7X_VMEM_BYTES = 64 * 1024 * 1024

EV_CQ = 0
EV_CKV = EV_CQ + MLA_Q_RANK
EV_MV = EV_CKV + MLA_KV_RANK
EV_MO = EV_MV + ML_HEADS * ML_V
EV_MQ = EV_MO + ML_HEADS * ML_V
EV_MK = EV_MQ + ML_HEADS * ML_QK
EV_KR = EV_MK + ML_HEADS * ML_QK
EV_MG = EV_KR + LANES
EV_COLS = EV_MG + LANES
assert EV_MV % ML_V == 0 and EV_MO % ML_V == 0 and EV_MQ % ML_QK == 0 and EV_MK % ML_QK == 0


class Tiles:
    ffn_rows = 1024
    ffn_cols = 512
    proj_rows = 512
    proj_col_steps_even = 2
    proj_col_steps_odd = 3
    prep_rows = 512
    out_rows = 512
    mla_q = 2048
    mla_k = 512
    attn_q = 1024
    attn_k = 512
    mlstm_chunk = 256
    mlstm_post_rows = 512


def _compiler_params(semantics, vmem_bytes):
    return pltpu.CompilerParams(dimension_semantics=semantics,
                                vmem_limit_bytes=int(min(vmem_bytes, V7X_VMEM_BYTES - (6 << 20))))


def _rms(x, gain):
    y = x * lax.rsqrt(jnp.mean(x * x, axis=-1, keepdims=True) + EPS)
    return y * gain


def _ffn_kernel(*refs, n_ff_steps, final):
    if final:
        x_ref, g_ref, wg_ref, wu_ref, wd_ref, gf_ref, o_ref, xn_ref = refs
    else:
        x_ref, g_ref, wg_ref, wu_ref, wd_ref, o_ref, xn_ref = refs
    j = pl.program_id(1)

    @pl.when(j == 0)
    def _():
        x = x_ref[...]
        xn_ref[...] = _rms(x, g_ref[...]).astype(BF16)
        o_ref[...] = x

    xn = xn_ref[...]
    gate = jnp.dot(xn, wg_ref[...], preferred_element_type=F32)
    up = jnp.dot(xn, wu_ref[...], preferred_element_type=F32)
    act = (0.5 * (gate * jax.nn.sigmoid(gate)) * up).astype(BF16)
    o_ref[...] += jnp.dot(act, wd_ref[...], preferred_element_type=F32)

    if final:
        @pl.when(j == n_ff_steps - 1)
        def _():
            o_ref[...] = _rms(o_ref[...], gf_ref[...])


def _ffn(x, gain, w_gu, w_down, final_gain=None):
    n, d = x.shape
    tm, tf = Tiles.ffn_rows, Tiles.ffn_cols
    nf = D_FF // tf
    wgu = w_gu.astype(BF16)
    wd = w_down.astype(BF16)
    final = final_gain is not None
    in_specs = [
        pl.BlockSpec((tm, d), lambda i, j: (i, 0)),
        pl.BlockSpec((1, d), lambda i, j: (0, 0)),
        pl.BlockSpec((d, tf), lambda i, j: (0, j)),
        pl.BlockSpec((d, tf), lambda i, j: (0, j + nf)),
        pl.BlockSpec((tf, d), lambda i, j: (j, 0)),
    ]
    args = [x, gain.reshape(1, d), wgu, wgu, wd]
    if final:
        in_specs.append(pl.BlockSpec((1, d), lambda i, j: (0, 0)))
        args.append(final_gain.reshape(1, d))
    vmem = (2 * 2 * tm * d * 4 + tm * d * 2 + 2 * 3 * d * tf * 2 + 4 * tm * tf * 4 + tm * d * 4)
    return pl.pallas_call(
        functools.partial(_ffn_kernel, n_ff_steps=nf, final=final),
        grid=(n // tm, nf),
        in_specs=in_specs,
        out_specs=pl.BlockSpec((tm, d), lambda i, j: (i, 0)),
        out_shape=jax.ShapeDtypeStruct((n, d), F32),
        scratch_shapes=[pltpu.VMEM((tm, d), BF16)],
        compiler_params=_compiler_params(("parallel", "arbitrary"), vmem),
        name="ffn",
    )(*args)


def _normproj_kernel(x_ref, g_ref, w_ref, cs_ref, o_ref, xn_ref):
    @pl.when(pl.program_id(1) == 0)
    def _():
        xn_ref[...] = _rms(x_ref[...], g_ref[...]).astype(BF16)

    acc = jnp.dot(xn_ref[...], w_ref[...], preferred_element_type=F32)
    o_ref[...] = (acc * cs_ref[...]).astype(o_ref.dtype)


def _normproj(x, gain, w, col_scale, out_dtype, col_steps):
    n, d = x.shape
    cols = w.shape[1]
    tm, tn = Tiles.proj_rows, cols // col_steps
    assert tn * col_steps == cols and tn % LANES == 0
    out_bytes = jnp.dtype(out_dtype).itemsize
    vmem = 2 * tm * d * 4 + tm * d * 2 + 2 * d * tn * 2 + 2 * tm * tn * out_bytes + 2 * tm * tn * 4
    return pl.pallas_call(
        _normproj_kernel,
        grid=(n // tm, col_steps),
        in_specs=[
            pl.BlockSpec((tm, d), lambda i, j: (i, 0)),
            pl.BlockSpec((1, d), lambda i, j: (0, 0)),
            pl.BlockSpec((d, tn), lambda i, j: (0, j)),
            pl.BlockSpec((1, tn), lambda i, j: (0, j)),
        ],
        out_specs=pl.BlockSpec((tm, tn), lambda i, j: (i, j)),
        out_shape=jax.ShapeDtypeStruct((n, cols), out_dtype),
        scratch_shapes=[pltpu.VMEM((tm, d), BF16)],
        compiler_params=_compiler_params(("parallel", "arbitrary"), vmem),
        name="normproj",
    )(x, gain.reshape(1, d), w.astype(BF16), col_scale.reshape(1, cols))


def _rope_tile(x, cos_t, sin_t):
    lane = lax.broadcasted_iota(jnp.int32, x.shape, 1)
    half = MLA_ROPE // 2
    partner = jnp.where(lane % MLA_ROPE < half, pltpu.roll(x, LANES - half, 1), pltpu.roll(x, half, 1))
    return x * cos_t + partner * sin_t


def _mla_prep_kernel(cq_ref, ckv_ref, kr_ref, gq_ref, gkv_ref, wq_ref, wk_ref, wvt_ref, cos_ref, sin_ref,
                     q_ref, k_ref, vt_ref):
    cos_t = cos_ref[...]
    sin_t = sin_ref[...]
    cqn = _rms(cq_ref[...], gq_ref[...]).astype(BF16)
    ckvn_f32 = _rms(ckv_ref[...], gkv_ref[...])
    ckvn = ckvn_f32.astype(BF16)
    scale = LOG2_E * (MLA_NOPE + MLA_ROPE) ** -0.5
    q = jnp.dot(cqn, wq_ref[...], preferred_element_type=F32)
    kn = jnp.dot(ckvn, wk_ref[...], preferred_element_type=F32)
    v_t = jnp.dot(wvt_ref[...], ckvn_f32.T.astype(BF16), preferred_element_type=F32)
    ones = jnp.ones((BF16_SUBLANES, v_t.shape[1]), BF16)
    lane = lax.broadcasted_iota(jnp.int32, cos_t.shape, 1)
    k_rope = jnp.where(lane < MLA_ROPE, _rope_tile(kr_ref[...], cos_t, sin_t), 0.0).astype(BF16)
    for h in range(MLA_HEADS):
        base = h * MLA_QK_PAD
        q_ref[:, base:base + MLA_NOPE] = (q[:, base:base + MLA_NOPE] * scale).astype(BF16)
        q_rope = _rope_tile(q[:, base + MLA_NOPE:base + MLA_QK_PAD], cos_t, sin_t)
        q_ref[:, base + MLA_NOPE:base + MLA_QK_PAD] = (q_rope * scale).astype(BF16)
        k_ref[:, base:base + MLA_NOPE] = kn[:, h * MLA_NOPE:(h + 1) * MLA_NOPE].astype(BF16)
        k_ref[:, base + MLA_NOPE:base + MLA_QK_PAD] = k_rope
        vt_ref[0, h, :MLA_V, :] = v_t[h * MLA_V:(h + 1) * MLA_V].astype(BF16)
        vt_ref[0, h, MLA_V:, :] = ones


def _mla_prep(z, b, s, g_cq, g_ckv, wq, wk, wv_t, cos_t, sin_t):
    n = z.shape[0]
    tm = Tiles.prep_rows
    tiles_per_seq = s // tm
    qk_cols = MLA_HEADS * MLA_QK_PAD
    v_cols = MLA_HEADS * MLA_V
    vt_rows = MLA_V + BF16_SUBLANES
    row = lambda i: (i, 0)
    fixed = lambda i: (0, 0)
    vmem = (2 * tm * (MLA_Q_RANK + MLA_KV_RANK + 3 * LANES) * 4 + 2 * MLA_Q_RANK * (qk_cols + 2 * v_cols) * 2
            + 2 * tm * (2 * qk_cols + v_cols) * 2 + 3 * tm * qk_cols * 4)
    return pl.pallas_call(
        _mla_prep_kernel,
        grid=(n // tm,),
        in_specs=[
            pl.BlockSpec((tm, MLA_Q_RANK), lambda i: (i, EV_CQ // MLA_Q_RANK)),
            pl.BlockSpec((tm, MLA_KV_RANK), lambda i: (i, EV_CKV // MLA_KV_RANK)),
            pl.BlockSpec((tm, LANES), lambda i: (i, EV_KR // LANES)),
            pl.BlockSpec((1, MLA_Q_RANK), fixed),
            pl.BlockSpec((1, MLA_KV_RANK), fixed),
            pl.BlockSpec((MLA_Q_RANK, qk_cols), fixed),
            pl.BlockSpec((MLA_KV_RANK, v_cols), fixed),
            pl.BlockSpec((v_cols, MLA_KV_RANK), fixed),
            pl.BlockSpec((tm, LANES), row),
            pl.BlockSpec((tm, LANES), row),
        ],
        out_specs=[
            pl.BlockSpec((tm, qk_cols), row),
            pl.BlockSpec((tm, qk_cols), row),
            pl.BlockSpec((1, MLA_HEADS, vt_rows, tm), lambda i: (i // tiles_per_seq, 0, 0, i % tiles_per_seq)),
        ],
        out_shape=[
            jax.ShapeDtypeStruct((n, qk_cols), BF16),
            jax.ShapeDtypeStruct((n, qk_cols), BF16),
            jax.ShapeDtypeStruct((b, MLA_HEADS, vt_rows, s), BF16),
        ],
        compiler_params=_compiler_params(("parallel",), vmem),
        name="mla_prep",
    )(z, z, z, g_cq.reshape(1, -1), g_ckv.reshape(1, -1), wq, wk, wv_t, cos_t, sin_t)


def _mla_attn_kernel(q_ref, k_ref, vt_ref, o_ref, *, n_chunks, tk):
    q = q_ref[0]
    tq = q.shape[0]
    rows = vt_ref.shape[2]

    def scores(c):
        k = k_ref[0, c * tk:(c + 1) * tk, :]
        return lax.dot_general(k, q, (((1,), (1,)), ((), ())), preferred_element_type=F32)

    m = jnp.full((1, tq), NEG_INIT, F32)
    acc = jnp.zeros((rows, tq), F32)
    s = scores(0)
    for c in range(n_chunks):
        s_next = scores(c + 1) if c + 1 < n_chunks else None
        m_new = jnp.maximum(m, jnp.max(s, axis=0, keepdims=True))
        alpha = jnp.exp2(m - m_new)
        p = jnp.exp2(s - m_new).astype(BF16)
        acc = alpha * acc + jnp.dot(vt_ref[0, 0, :, c * tk:(c + 1) * tk], p, preferred_element_type=F32)
        m, s = m_new, s_next
    o_t = acc[:MLA_V] / acc[MLA_V:MLA_V + 1]
    o_ref[0] = o_t.T.astype(o_ref.dtype)


def _mla_attention(q, k, vt):
    b, s, _ = q.shape
    tq, tk = Tiles.mla_q, Tiles.mla_k
    rows = vt.shape[2]
    vmem = (2 * tq * MLA_QK_PAD * 2 + 2 * s * (MLA_QK_PAD + rows) * 2 + 2 * tq * MLA_V * 2
            + 6 * tq * tk * 4)
    return pl.pallas_call(
        functools.partial(_mla_attn_kernel, n_chunks=s // tk, tk=tk),
        grid=(b, MLA_HEADS, s // tq),
        in_specs=[
            pl.BlockSpec((1, tq, MLA_QK_PAD), lambda bi, h, i: (bi, i, h)),
            pl.BlockSpec((1, s, MLA_QK_PAD), lambda bi, h, i: (bi, 0, h)),
            pl.BlockSpec((1, 1, rows, s), lambda bi, h, i: (bi, h, 0, 0)),
        ],
        out_specs=pl.BlockSpec((1, tq, MLA_V), lambda bi, h, i: (bi, i, h)),
        out_shape=jax.ShapeDtypeStruct((b, s, MLA_HEADS * MLA_V), BF16),
        compiler_params=_compiler_params(("parallel", "parallel", "arbitrary"), vmem),
        name="mla_attn",
    )(q, k, vt)


def _log_sigmoid(x):
    return jnp.minimum(x, 0.0) - jnp.log(1.0 + jnp.exp(-jnp.abs(x)))


def _mlstm_masks(chunk):
    t_idx = lax.broadcasted_iota(jnp.int32, (chunk, chunk), 0)
    s_idx = lax.broadcasted_iota(jnp.int32, (chunk, chunk), 1)
    lower, upper = s_idx <= t_idx, s_idx >= t_idx
    one_hot = lambda m: jnp.where(m, 1.0, 0.0).astype(F32)
    blocked = lambda m: jnp.where(m, 0.0, NEG_INIT).astype(F32)
    return {"eye": one_hot(s_idx == t_idx), "lower": one_hot(lower), "upper": one_hot(upper),
            "neg_lower": blocked(lower), "neg_upper": blocked(upper)}


def _mlstm_chunk(q_ref, k_ref, v_ref, gates, masks, c_ref, n_row, m_st, start, chunk, li_idx, lf_idx, reverse):
    L = chunk
    q = q_ref[0, pl.ds(start, L), :]
    k = k_ref[0, pl.ds(start, L), :]
    v = v_ref[0, pl.ds(start, L), :].astype(BF16)
    li_row = gates[li_idx:li_idx + 1, :]
    lf_row = _log_sigmoid(gates[lf_idx:lf_idx + 1, :])
    li_col = jnp.sum(masks["eye"] * li_row, axis=1, keepdims=True)
    lf_col = jnp.sum(masks["eye"] * lf_row, axis=1, keepdims=True)
    visible, visible_t, blocked = ((masks["upper"], masks["lower"], masks["neg_upper"]) if reverse
                                   else (masks["lower"], masks["upper"], masks["neg_lower"]))
    a_col = jnp.sum(visible * lf_row, axis=1, keepdims=True)
    a_row = jnp.sum(visible_t * lf_col, axis=0, keepdims=True)
    g = jnp.sum(lf_row, axis=1, keepdims=True)

    dmat = (a_col - a_row) + li_row + blocked
    inter = a_col + m_st
    m = jnp.maximum(inter, jnp.max(dmat, axis=1, keepdims=True))
    w_inter = jnp.exp(inter - m)
    q16 = q.astype(BF16)
    qk = (lax.dot_general(q16, k.astype(BF16), (((1,), (1,)), ((), ())), preferred_element_type=F32)
          * jnp.exp(dmat - m))
    num = (w_inter * jnp.dot(q16, c_ref[...].astype(BF16), preferred_element_type=F32)
           + jnp.dot(qk.astype(BF16), v, preferred_element_type=F32))
    den = w_inter * jnp.sum(q * n_row, axis=1, keepdims=True) + jnp.sum(qk, axis=1, keepdims=True)
    h_out = num / jnp.maximum(jnp.abs(den), jnp.exp(-m))

    r_col = g - a_col + li_col
    m_new = jnp.maximum(g + m_st, jnp.max(r_col, axis=0, keepdims=True))
    w_old = jnp.exp(g + m_st - m_new)
    kw = k * jnp.exp(r_col - m_new)
    c_ref[...] = w_old * c_ref[...] + jnp.dot(kw.T.astype(BF16), v, preferred_element_type=F32)
    n_new = w_old * n_row + jnp.sum(kw, axis=0, keepdims=True)
    return h_out, n_new, m_new


def _mlstm_kernel(q_ref, k_ref, v_ref, gate_ref, bias_ref, mo_ref, gn_ref, o_ref,
                  hf_ref, hb_ref, cf_ref, cb_ref, *, seq, chunk, post_rows):
    nc = seq // chunk
    cf_ref[...] = jnp.zeros_like(cf_ref)
    cb_ref[...] = jnp.zeros_like(cb_ref)
    bias = bias_ref[0]
    masks = _mlstm_masks(chunk)

    def body(i, carry):
        n_f, m_f, n_b, m_b = carry
        sf = pl.multiple_of(i * chunk, chunk)
        sb = pl.multiple_of((nc - 1 - i) * chunk, chunk)
        h_f, n_f, m_f = _mlstm_chunk(q_ref, k_ref, v_ref, gate_ref[0, 0, :, pl.ds(sf, chunk)] + bias, masks,
                                     cf_ref, n_f, m_f, sf, chunk, 0, 1, False)
        hf_ref[pl.ds(sf, chunk), :] = h_f
        h_b, n_b, m_b = _mlstm_chunk(q_ref, k_ref, v_ref, gate_ref[0, 0, :, pl.ds(sb, chunk)] + bias, masks,
                                     cb_ref, n_b, m_b, sb, chunk, 2, 3, True)
        hb_ref[pl.ds(sb, chunk), :] = h_b
        return n_f, m_f, n_b, m_b

    n0 = jnp.zeros((1, ML_QK), F32)
    m0 = jnp.full((1, 1), NEG_INIT, F32)
    lax.fori_loop(0, nc, body, (n0, m0, n0, m0))

    gn = gn_ref[...]

    def post(i, carry):
        r = pl.multiple_of(i * post_rows, post_rows)
        hm = hf_ref[pl.ds(r, post_rows), :] + hb_ref[pl.ds(r, post_rows), :]
        y = _rms(hm, gn)
        o_ref[0, pl.ds(r, post_rows), :] = (y * jax.nn.sigmoid(mo_ref[0, pl.ds(r, post_rows), :])).astype(o_ref.dtype)
        return carry

    lax.fori_loop(0, seq // post_rows, post, 0)


def _mlstm(z3, gates_t, gate_bias, g_mlstm):
    b, s, _ = z3.shape
    chunk = Tiles.mlstm_chunk
    vmem = (2 * s * (2 * ML_QK + 2 * ML_V + LANES) * 4 + 2 * s * ML_V * 2 + 2 * s * ML_V * 4
            + 2 * ML_QK * ML_V * 4 + 24 * chunk * chunk * 4 + (2 << 20))
    return pl.pallas_call(
        functools.partial(_mlstm_kernel, seq=s, chunk=chunk, post_rows=Tiles.mlstm_post_rows),
        grid=(b, ML_HEADS),
        in_specs=[
            pl.BlockSpec((1, s, ML_QK), lambda bi, h: (bi, 0, EV_MQ // ML_QK + h)),
            pl.BlockSpec((1, s, ML_QK), lambda bi, h: (bi, 0, EV_MK // ML_QK + h)),
            pl.BlockSpec((1, s, ML_V), lambda bi, h: (bi, 0, EV_MV // ML_V + h)),
            pl.BlockSpec((1, 1, 4, s), lambda bi, h: (bi, h, 0, 0)),
            pl.BlockSpec((1, 4, 1), lambda bi, h: (h, 0, 0)),
            pl.BlockSpec((1, s, ML_V), lambda bi, h: (bi, 0, EV_MO // ML_V + h)),
            pl.BlockSpec((1, ML_V), lambda bi, h: (0, h)),
        ],
        out_specs=pl.BlockSpec((1, s, ML_V), lambda bi, h: (bi, 0, h)),
        out_shape=jax.ShapeDtypeStruct((b, s, ML_HEADS * ML_V), BF16),
        scratch_shapes=[
            pltpu.VMEM((s, ML_V), F32),
            pltpu.VMEM((s, ML_V), F32),
            pltpu.VMEM((ML_QK, ML_V), F32),
            pltpu.VMEM((ML_QK, ML_V), F32),
        ],
        compiler_params=_compiler_params(("parallel", "parallel"), vmem),
        name="mlstm",
    )(z3, z3, z3, gates_t, gate_bias, z3, g_mlstm.reshape(1, -1))


def _outproj_kernel(*refs, n_lhs):
    x_ref = refs[0]
    a_refs = refs[1:1 + n_lhs]
    w_refs = refs[1 + n_lhs:1 + 2 * n_lhs]
    o_ref = refs[1 + 2 * n_lhs]
    acc = x_ref[...]
    for a_ref, w_ref in zip(a_refs, w_refs):
        acc = acc + jnp.dot(a_ref[...], w_ref[...], preferred_element_type=F32)
    o_ref[...] = acc


def _outproj(x, lhs, weights):
    n, d = x.shape
    tm = Tiles.out_rows
    row = lambda i: (i, 0)
    fixed = lambda i: (0, 0)
    k_total = sum(a.shape[1] for a in lhs)
    vmem = 2 * 2 * tm * d * 4 + 2 * tm * k_total * 2 + 2 * k_total * d * 2 + tm * d * 4
    return pl.pallas_call(
        functools.partial(_outproj_kernel, n_lhs=len(lhs)),
        grid=(n // tm,),
        in_specs=([pl.BlockSpec((tm, d), row)]
                  + [pl.BlockSpec((tm, a.shape[1]), row) for a in lhs]
                  + [pl.BlockSpec(w.shape, fixed) for w in weights]),
        out_specs=pl.BlockSpec((tm, d), row),
        out_shape=jax.ShapeDtypeStruct((n, d), F32),
        compiler_params=_compiler_params(("parallel",), vmem),
        name="outproj",
    )(x, *lhs, *[w.astype(BF16) for w in weights])


def _diff_attn_kernel(slope_ref, q1_ref, q2_ref, k1_ref, k2_ref, vt_ref, pq_ref, pk_ref, lam_ref, gs_ref, o_ref,
                      vta_ref, *, n_chunks, tk, lam_init):
    h = pl.program_id(1)
    slope = slope_ref[h] * LOG2_E
    q1 = q1_ref[0]
    q2 = q2_ref[0]
    tq = q1.shape[0]
    dv = 2 * DA_HEAD
    rows = vta_ref.shape[0]

    @pl.when(pl.program_id(2) == 0)
    def _():
        vta_ref[:dv, :] = vt_ref[0, 0]
        vta_ref[dv:, :] = jnp.ones((rows - dv, vta_ref.shape[1]), BF16)
    pos_q = slope * pq_ref[0].astype(F32)
    nt = (((1,), (1,)), ((), ()))

    def scores(c):
        rows_c = slice(c * tk, (c + 1) * tk)
        dist = jnp.abs(slope * pk_ref[0, rows_c, :].astype(F32) - pos_q)
        s1 = lax.dot_general(k1_ref[0, rows_c, :], q1, nt, preferred_element_type=F32) - dist
        s2 = lax.dot_general(k2_ref[0, rows_c, :], q2, nt, preferred_element_type=F32) - dist
        return s1, s2

    def softmax_step(s, vt, m, acc):
        m_new = jnp.maximum(m, jnp.max(s, axis=0, keepdims=True))
        alpha = jnp.exp2(m - m_new)
        p = jnp.exp2(s - m_new).astype(BF16)
        return m_new, alpha * acc + jnp.dot(vt, p, preferred_element_type=F32)

    m1 = m2 = jnp.full((1, tq), NEG_INIT, F32)
    acc1 = acc2 = jnp.zeros((rows, tq), F32)
    s1, s2 = scores(0)
    for c in range(n_chunks):
        nxt = scores(c + 1) if c + 1 < n_chunks else (None, None)
        vt = vta_ref[:, c * tk:(c + 1) * tk]
        m1, acc1 = softmax_step(s1, vt, m1, acc1)
        m2, acc2 = softmax_step(s2, vt, m2, acc2)
        s1, s2 = nxt

    lam_vec = lam_ref[...]
    lam = (jnp.exp(jnp.sum(lam_vec[0:1] * lam_vec[1:2], axis=1, keepdims=True))
           - jnp.exp(jnp.sum(lam_vec[2:3] * lam_vec[3:4], axis=1, keepdims=True)) + lam_init)
    o_t = acc1[:dv] / acc1[dv:dv + 1] - lam * (acc2[:dv] / acc2[dv:dv + 1])
    o_ref[0] = (_rms(o_t.T, gs_ref[...]) * (1.0 - lam_init)).astype(o_ref.dtype)


def _diff_attention(zq, vt, pos_col, pos_row, slopes, lam_vec, g_sub, lam_init):
    b, s, _ = zq.shape
    tq, tk = Tiles.attn_q, Tiles.attn_k
    dv = 2 * DA_HEAD
    rows = dv + BF16_SUBLANES
    k_base = DA_HEADS * 2
    vmem = (2 * 2 * tq * DA_HEAD * 2 + 3 * s * (2 * DA_HEAD + rows) * 2 + 2 * s * LANES * 4 + 2 * tq * dv * 2
            + 14 * tq * tk * 4)
    return pl.pallas_call(
        functools.partial(_diff_attn_kernel, n_chunks=s // tk, tk=tk, lam_init=lam_init),
        grid=(b, DA_HEADS, s // tq),
        in_specs=[
            pl.BlockSpec(memory_space=pltpu.SMEM),
            pl.BlockSpec((1, tq, DA_HEAD), lambda bi, h, i: (bi, i, 2 * h)),
            pl.BlockSpec((1, tq, DA_HEAD), lambda bi, h, i: (bi, i, 2 * h + 1)),
            pl.BlockSpec((1, s, DA_HEAD), lambda bi, h, i: (bi, 0, k_base + 2 * h)),
            pl.BlockSpec((1, s, DA_HEAD), lambda bi, h, i: (bi, 0, k_base + 2 * h + 1)),
            pl.BlockSpec((1, 1, dv, s), lambda bi, h, i: (bi, h, 0, 0)),
            pl.BlockSpec((1, 1, tq), lambda bi, h, i: (bi, 0, i)),
            pl.BlockSpec((1, s, 1), lambda bi, h, i: (bi, 0, 0)),
            pl.BlockSpec((4, DA_HEAD), lambda bi, h, i: (0, 0)),
            pl.BlockSpec((1, dv), lambda bi, h, i: (0, 0)),
        ],
        out_specs=pl.BlockSpec((1, tq, dv), lambda bi, h, i: (bi, i, h)),
        out_shape=jax.ShapeDtypeStruct((b, s, DA_HEADS * dv), BF16),
        scratch_shapes=[pltpu.VMEM((rows, s), BF16)],
        compiler_params=_compiler_params(("parallel", "parallel", "arbitrary"), vmem),
        name="diff_attn",
    )(slopes, zq, zq, zq, zq, vt, pos_row, pos_col, lam_vec, g_sub.reshape(1, dv))


def _pack_even_w_in(w_in):
    d = w_in.shape[0]
    sizes = (MLA_Q_RANK, MLA_KV_RANK, MLA_ROPE, ML_HEADS * ML_QK, ML_HEADS * ML_QK, ML_HEADS * ML_V,
             ML_HEADS * ML_V, 4 * ML_HEADS)
    c_q, c_kv, k_r, m_q, m_k, m_v, m_o, m_g = jnp.split(w_in, np.cumsum(sizes)[:-1].tolist(), axis=1)
    pad = lambda w, width: jnp.concatenate([w, jnp.zeros((d, width - w.shape[1]), w.dtype)], axis=1)
    packed = jnp.concatenate([c_q, c_kv, m_v, m_o, m_q, m_k, pad(k_r, LANES), pad(m_g, LANES)], axis=1)
    assert packed.shape[1] == EV_COLS
    return packed


def _pack_w_uq(w_uq):
    r = w_uq.shape[0]
    w = w_uq.reshape(r, MLA_HEADS, MLA_NOPE + MLA_ROPE)
    w = jnp.concatenate([w, jnp.zeros((r, MLA_HEADS, MLA_QK_PAD - MLA_NOPE - MLA_ROPE), w.dtype)], axis=2)
    return w.reshape(r, MLA_HEADS * MLA_QK_PAD).astype(BF16)


def _split_w_ukv(w_ukv):
    r = w_ukv.shape[0]
    w = w_ukv.reshape(r, MLA_HEADS, MLA_NOPE + MLA_V)
    wk = w[:, :, :MLA_NOPE].reshape(r, MLA_HEADS * MLA_NOPE)
    wv_t = w[:, :, MLA_NOPE:].reshape(r, MLA_HEADS * MLA_V).T
    return wk.astype(BF16), wv_t.astype(BF16)


def _rope_tables(positions):
    inv_freq = ROPE_THETA ** (-jnp.arange(0, MLA_ROPE, 2, dtype=F32) / MLA_ROPE)
    ang = positions.astype(F32).reshape(-1, 1) * inv_freq
    cos, sin = jnp.cos(ang), jnp.sin(ang)
    zeros = jnp.zeros((ang.shape[0], LANES - MLA_ROPE), F32)
    return jnp.concatenate([cos, cos, zeros], axis=1), jnp.concatenate([-sin, sin, zeros], axis=1)


def _even_mixer(x, b, s, positions, norm, w_in, g_cq, w_uq, g_ckv, w_ukv, b_gates, g_mlstm, w_o):
    n = b * s
    col_scale = jnp.ones((EV_COLS,), F32).at[EV_MK:EV_MK + ML_HEADS * ML_QK].set(ML_QK ** -0.5)
    z = _normproj(x, norm, _pack_even_w_in(w_in), col_scale, F32, Tiles.proj_col_steps_even)

    cos_t, sin_t = _rope_tables(positions)
    wk, wv_t = _split_w_ukv(w_ukv)
    q, k, vt = _mla_prep(z, b, s, g_cq, g_ckv, _pack_w_uq(w_uq), wk, wv_t, cos_t, sin_t)
    a_out = _mla_attention(q.reshape(b, s, -1), k.reshape(b, s, -1), vt)

    z3 = z.reshape(b, s, EV_COLS)
    gates_t = z3[:, :, EV_MG:EV_MG + 4 * ML_HEADS].reshape(b, s, 4, ML_HEADS).transpose(0, 3, 2, 1)
    gate_bias = b_gates.astype(F32).reshape(4, ML_HEADS).T.reshape(ML_HEADS, 4, 1)
    m_out = _mlstm(z3, gates_t, gate_bias, g_mlstm)

    split = MLA_HEADS * MLA_V
    return _outproj(x, [a_out.reshape(n, -1), m_out.reshape(n, -1)], [w_o[:split], w_o[split:]])


def _odd_mixer(x, b, s, positions, norm, w_in, lam_q1, lam_k1, lam_q2, lam_k2, g_sub, w_o, lam_init):
    n = b * s
    width = DA_HEADS * 2 * DA_HEAD
    col_scale = jnp.ones((3 * width,), F32).at[:width].set(LOG2_E * DA_HEAD ** -0.5)
    zq = _normproj(x, norm, w_in, col_scale, BF16, Tiles.proj_col_steps_odd).reshape(b, s, 3 * width)
    slopes = jnp.asarray([2.0 ** (-8.0 * (h + 1) / DA_HEADS) for h in range(DA_HEADS)], dtype=F32)
    lam_vec = jnp.stack([lam_q1, lam_k1, lam_q2, lam_k2]).astype(F32)
    vt = zq[:, :, 2 * width:].reshape(b, s, DA_HEADS, 2 * DA_HEAD).transpose(0, 2, 3, 1)
    pos = positions - positions[:, :1]
    o = _diff_attention(zq, vt, pos.reshape(b, s, 1), pos.reshape(b, 1, s), slopes, lam_vec, g_sub, lam_init)
    return _outproj(x, [o.reshape(n, width)], [w_o])


def kernel(x, positions, l0_ffn1_norm, l0_ffn1_w_gu, l0_ffn1_w_down, l0_mix_norm, l0_w_in, l0_g_cq, l0_w_uq, l0_g_ckv, l0_w_ukv, l0_b_gates, l0_g_mlstm, l0_w_o, l0_ffn2_norm, l0_ffn2_w_gu, l0_ffn2_w_down, l1_ffn1_norm, l1_ffn1_w_gu, l1_ffn1_w_down, l1_mix_norm, l1_w_in, l1_lam_q1, l1_lam_k1, l1_lam_q2, l1_lam_k2, l1_g_sub, l1_w_o, l1_ffn2_norm, l1_ffn2_w_gu, l1_ffn2_w_down, final_norm):
    b, s, d = x.shape
    h = x.reshape(b * s, d)
    h = _ffn(h, l0_ffn1_norm, l0_ffn1_w_gu, l0_ffn1_w_down)
    h = _even_mixer(h, b, s, positions, l0_mix_norm, l0_w_in, l0_g_cq, l0_w_uq, l0_g_ckv, l0_w_ukv, l0_b_gates,
                    l0_g_mlstm, l0_w_o)
    h = _ffn(h, l0_ffn2_norm, l0_ffn2_w_gu, l0_ffn2_w_down)
    h = _ffn(h, l1_ffn1_norm, l1_ffn1_w_gu, l1_ffn1_w_down)
    lam_init = 0.8 - 0.6 * math.exp(-0.3 * 1)
    h = _odd_mixer(h, b, s, positions, l1_mix_norm, l1_w_in, l1_lam_q1, l1_lam_k1, l1_lam_q2, l1_lam_k2, l1_g_sub,
                   l1_w_o, lam_init)
    h = _ffn(h, l1_ffn2_norm, l1_ffn2_w_gu, l1_ffn2_w_down, final_gain=final_norm)
    return h.reshape(b, s, d)
```

```python
import functools
import math

import jax
import jax.numpy as jnp
import numpy as np
from jax import lax
from jax.experimental import pallas as pl
from jax.experimental.pallas import tpu as pltpu

F32 = jnp.float32
BF16 = jnp.bfloat16

D_MODEL = 2048
MLA_HEADS = 8
MLA_Q_RANK = 512
MLA_KV_RANK = 512
MLA_NOPE = 128
MLA_ROPE = 64
MLA_V = 128
ROPE_THETA = 10000.0
ML_HEADS = 4
ML_QK = 128
ML_V = 256
DA_HEADS = 8
DA_HEAD = 128
D_FF = 5632
EPS = 1e-6
NEG_INIT = -1e30
LOG2_E = math.log2(math.e)

LANES = 128
BF16_SUBLANES = 16
MLA_QK_PAD = 256
V7X_VMEM_BYTES = 64 * 1024 * 1024

EV_CQ = 0
EV_CKV = EV_CQ + MLA_Q_RANK
EV_MV = EV_CKV + MLA_KV_RANK
EV_MO = EV_MV + ML_HEADS * ML_V
EV_MQ = EV_MO + ML_HEADS * ML_V
EV_MK = EV_MQ + ML_HEADS * ML_QK
EV_KR = EV_MK + ML_HEADS * ML_QK
EV_MG = EV_KR + LANES
EV_COLS = EV_MG + LANES
assert EV_MV % ML_V == 0 and EV_MO % ML_V == 0 and EV_MQ % ML_QK == 0 and EV_MK % ML_QK == 0


class Tiles:
    ffn_rows = 1024
    ffn_cols = 512
    proj_rows = 512
    proj_col_steps_even = 2
    prep_rows = 512
    out_rows = 512
    mla_q = 2048
    mla_k = 512
    attn_q = 1024
    attn_k = 512
    mlstm_chunk = 256
    mlstm_post_rows = 512


def _compiler_params(semantics, vmem_bytes):
    return pltpu.CompilerParams(dimension_semantics=semantics,
                                vmem_limit_bytes=int(min(vmem_bytes, V7X_VMEM_BYTES - (6 << 20))))


def _rms(x, gain):
    y = x * lax.rsqrt(jnp.mean(x * x, axis=-1, keepdims=True) + EPS)
    return y * gain


def _ffn_kernel(*refs, n_ff_steps, final):
    if final:
        x_ref, g_ref, wg_ref, wu_ref, wd_ref, gf_ref, o_ref, xn_ref = refs
    else:
        x_ref, g_ref, wg_ref, wu_ref, wd_ref, o_ref, xn_ref = refs
    j = pl.program_id(1)

    @pl.when(j == 0)
    def _():
        x = x_ref[...]
        xn_ref[...] = _rms(x, g_ref[...]).astype(BF16)
        o_ref[...] = x

    xn = xn_ref[...]
    gate = jnp.dot(xn, wg_ref[...], preferred_element_type=F32)
    up = jnp.dot(xn, wu_ref[...], preferred_element_type=F32)
    act = (0.5 * (gate * jax.nn.sigmoid(gate)) * up).astype(BF16)
    o_ref[...] += jnp.dot(act, wd_ref[...], preferred_element_type=F32)

    if final:
        @pl.when(j == n_ff_steps - 1)
        def _():
            o_ref[...] = _rms(o_ref[...], gf_ref[...])


def _ffn(x, gain, w_gu, w_down, final_gain=None):
    n, d = x.shape
    tm, tf = Tiles.ffn_rows, Tiles.ffn_cols
    nf = D_FF // tf
    wgu = w_gu.astype(BF16)
    wd = w_down.astype(BF16)
    final = final_gain is not None
    in_specs = [
        pl.BlockSpec((tm, d), lambda i, j: (i, 0)),
        pl.BlockSpec((1, d), lambda i, j: (0, 0)),
        pl.BlockSpec((d, tf), lambda i, j: (0, j)),
        pl.BlockSpec((d, tf), lambda i, j: (0, j + nf)),
        pl.BlockSpec((tf, d), lambda i, j: (j, 0)),
    ]
    args = [x, gain.reshape(1, d), wgu, wgu, wd]
    if final:
        in_specs.append(pl.BlockSpec((1, d), lambda i, j: (0, 0)))
        args.append(final_gain.reshape(1, d))
    vmem = (2 * 2 * tm * d * 4 + tm * d * 2 + 2 * 3 * d * tf * 2 + 4 * tm * tf * 4 + tm * d * 4)
    return pl.pallas_call(
        functools.partial(_ffn_kernel, n_ff_steps=nf, final=final),
        grid=(n // tm, nf),
        in_specs=in_specs,
        out_specs=pl.BlockSpec((tm, d), lambda i, j: (i, 0)),
        out_shape=jax.ShapeDtypeStruct((n, d), F32),
        scratch_shapes=[pltpu.VMEM((tm, d), BF16)],
        compiler_params=_compiler_params(("parallel", "arbitrary"), vmem),
        name="ffn",
    )(*args)


def _normproj_kernel(x_ref, g_ref, w_ref, cs_ref, o_ref, xn_ref):
    @pl.when(pl.program_id(1) == 0)
    def _():
        xn_ref[...] = _rms(x_ref[...], g_ref[...]).astype(BF16)

    acc = jnp.dot(xn_ref[...], w_ref[...], preferred_element_type=F32)
    o_ref[...] = (acc * cs_ref[...]).astype(o_ref.dtype)


def _normproj(x, gain, w, col_scale, out_dtype, col_steps):
    n, d = x.shape
    cols = w.shape[1]
    tm, tn = Tiles.proj_rows, cols // col_steps
    assert tn * col_steps == cols and tn % LANES == 0
    out_bytes = jnp.dtype(out_dtype).itemsize
    vmem = 2 * tm * d * 4 + tm * d * 2 + 2 * d * tn * 2 + 2 * tm * tn * out_bytes + 2 * tm * tn * 4
    return pl.pallas_call(
        _normproj_kernel,
        grid=(n // tm, col_steps),
        in_specs=[
            pl.BlockSpec((tm, d), lambda i, j: (i, 0)),
            pl.BlockSpec((1, d), lambda i, j: (0, 0)),
            pl.BlockSpec((d, tn), lambda i, j: (0, j)),
            pl.BlockSpec((1, tn), lambda i, j: (0, j)),
        ],
        out_specs=pl.BlockSpec((tm, tn), lambda i, j: (i, j)),
        out_shape=jax.ShapeDtypeStruct((n, cols), out_dtype),
        scratch_shapes=[pltpu.VMEM((tm, d), BF16)],
        compiler_params=_compiler_params(("parallel", "arbitrary"), vmem),
        name="normproj",
    )(x, gain.reshape(1, d), w.astype(BF16), col_scale.reshape(1, cols))


def _qkv_proj_kernel(x_ref, g_ref, w_ref, o_ref, vt_ref, xn_ref, *, q_scale, heads, head_width):
    j = pl.program_id(1)

    @pl.when(j == 0)
    def _():
        xn_ref[...] = _rms(x_ref[...], g_ref[...]).astype(BF16)

    def project():
        return jnp.dot(xn_ref[...], w_ref[...], preferred_element_type=F32)

    @pl.when(j == 0)
    def _():
        o_ref[...] = (project() * q_scale).astype(o_ref.dtype)

    @pl.when(j == 1)
    def _():
        o_ref[...] = project().astype(o_ref.dtype)

    @pl.when(j == 2)
    def _():
        acc = project()
        for h in range(heads):
            vt_ref[0, h] = acc[:, h * head_width:(h + 1) * head_width].T.astype(vt_ref.dtype)


def _qkv_proj(x, b, s, gain, w, q_scale, heads, head_width):
    n, d = x.shape
    width = heads * head_width
    assert w.shape == (d, 3 * width)
    tm = Tiles.proj_rows
    tiles_per_seq = s // tm
    vmem = 2 * tm * d * 4 + tm * d * 2 + 2 * d * width * 2 + 4 * tm * width * 2 + 3 * tm * width * 4
    return pl.pallas_call(
        functools.partial(_qkv_proj_kernel, q_scale=q_scale, heads=heads, head_width=head_width),
        grid=(n // tm, 3),
        in_specs=[
            pl.BlockSpec((tm, d), lambda i, j: (i, 0)),
            pl.BlockSpec((1, d), lambda i, j: (0, 0)),
            pl.BlockSpec((d, width), lambda i, j: (0, j)),
        ],
        out_specs=[
            pl.BlockSpec((tm, width), lambda i, j: (i, jnp.minimum(j, 1))),
            pl.BlockSpec((1, heads, head_width, tm), lambda i, j: (i // tiles_per_seq, 0, 0, i % tiles_per_seq)),
        ],
        out_shape=[
            jax.ShapeDtypeStruct((n, 2 * width), BF16),
            jax.ShapeDtypeStruct((b, heads, head_width, s), BF16),
        ],
        scratch_shapes=[pltpu.VMEM((tm, d), BF16)],
        compiler_params=_compiler_params(("parallel", "arbitrary"), vmem),
        name="qkv_proj",
    )(x, gain.reshape(1, d), w.astype(BF16))


def _rope_tile(x, cos_t, sin_t):
    lane = lax.broadcasted_iota(jnp.int32, x.shape, 1)
    half = MLA_ROPE // 2
    partner = jnp.where(lane % MLA_ROPE < half, pltpu.roll(x, LANES - half, 1), pltpu.roll(x, half, 1))
    return x * cos_t + partner * sin_t


def _mla_prep_kernel(cq_ref, ckv_ref, kr_ref, gq_ref, gkv_ref, wq_ref, wk_ref, wvt_ref, cos_ref, sin_ref,
                     q_ref, k_ref, vt_ref):
    cos_t = cos_ref[...]
    sin_t = sin_ref[...]
    cqn = _rms(cq_ref[...], gq_ref[...]).astype(BF16)
    ckvn_f32 = _rms(ckv_ref[...], gkv_ref[...])
    ckvn = ckvn_f32.astype(BF16)
    scale = LOG2_E * (MLA_NOPE + MLA_ROPE) ** -0.5
    q = jnp.dot(cqn, wq_ref[...], preferred_element_type=F32)
    kn = jnp.dot(ckvn, wk_ref[...], preferred_element_type=F32)
    v_t = jnp.dot(wvt_ref[...], ckvn_f32.T.astype(BF16), preferred_element_type=F32)
    ones = jnp.ones((BF16_SUBLANES, v_t.shape[1]), BF16)
    lane = lax.broadcasted_iota(jnp.int32, cos_t.shape, 1)
    k_rope = jnp.where(lane < MLA_ROPE, _rope_tile(kr_ref[...], cos_t, sin_t), 0.0).astype(BF16)
    for h in range(MLA_HEADS):
        base = h * MLA_QK_PAD
        q_ref[:, base:base + MLA_NOPE] = (q[:, base:base + MLA_NOPE] * scale).astype(BF16)
        q_rope = _rope_tile(q[:, base + MLA_NOPE:base + MLA_QK_PAD], cos_t, sin_t)
        q_ref[:, base + MLA_NOPE:base + MLA_QK_PAD] = (q_rope * scale).astype(BF16)
        k_ref[:, base:base + MLA_NOPE] = kn[:, h * MLA_NOPE:(h + 1) * MLA_NOPE].astype(BF16)
        k_ref[:, base + MLA_NOPE:base + MLA_QK_PAD] = k_rope
        vt_ref[0, h, :MLA_V, :] = v_t[h * MLA_V:(h + 1) * MLA_V].astype(BF16)
        vt_ref[0, h, MLA_V:, :] = ones


def _mla_prep(z, b, s, g_cq, g_ckv, wq, wk, wv_t, cos_t, sin_t):
    n = z.shape[0]
    tm = Tiles.prep_rows
    tiles_per_seq = s // tm
    qk_cols = MLA_HEADS * MLA_QK_PAD
    v_cols = MLA_HEADS * MLA_V
    vt_rows = MLA_V + BF16_SUBLANES
    row = lambda i: (i, 0)
    fixed = lambda i: (0, 0)
    vmem = (2 * tm * (MLA_Q_RANK + MLA_KV_RANK + 3 * LANES) * 4 + 2 * MLA_Q_RANK * (qk_cols + 2 * v_cols) * 2
            + 2 * tm * (2 * qk_cols + v_cols) * 2 + 3 * tm * qk_cols * 4)
    return pl.pallas_call(
        _mla_prep_kernel,
        grid=(n // tm,),
        in_specs=[
            pl.BlockSpec((tm, MLA_Q_RANK), lambda i: (i, EV_CQ // MLA_Q_RANK)),
            pl.BlockSpec((tm, MLA_KV_RANK), lambda i: (i, EV_CKV // MLA_KV_RANK)),
            pl.BlockSpec((tm, LANES), lambda i: (i, EV_KR // LANES)),
            pl.BlockSpec((1, MLA_Q_RANK), fixed),
            pl.BlockSpec((1, MLA_KV_RANK), fixed),
            pl.BlockSpec((MLA_Q_RANK, qk_cols), fixed),
            pl.BlockSpec((MLA_KV_RANK, v_cols), fixed),
            pl.BlockSpec((v_cols, MLA_KV_RANK), fixed),
            pl.BlockSpec((tm, LANES), row),
            pl.BlockSpec((tm, LANES), row),
        ],
        out_specs=[
            pl.BlockSpec((tm, qk_cols), row),
            pl.BlockSpec((tm, qk_cols), row),
            pl.BlockSpec((1, MLA_HEADS, vt_rows, tm), lambda i: (i // tiles_per_seq, 0, 0, i % tiles_per_seq)),
        ],
        out_shape=[
            jax.ShapeDtypeStruct((n, qk_cols), BF16),
            jax.ShapeDtypeStruct((n, qk_cols), BF16),
            jax.ShapeDtypeStruct((b, MLA_HEADS, vt_rows, s), BF16),
        ],
        compiler_params=_compiler_params(("parallel",), vmem),
        name="mla_prep",
    )(z, z, z, g_cq.reshape(1, -1), g_ckv.reshape(1, -1), wq, wk, wv_t, cos_t, sin_t)


def _mla_attn_kernel(q_ref, k_ref, vt_ref, o_ref, *, n_chunks, tk):
    q = q_ref[0]
    tq = q.shape[0]
    rows = vt_ref.shape[2]

    def scores(c):
        k = k_ref[0, c * tk:(c + 1) * tk, :]
        return lax.dot_general(k, q, (((1,), (1,)), ((), ())), preferred_element_type=F32)

    m = jnp.full((1, tq), NEG_INIT, F32)
    acc = jnp.zeros((rows, tq), F32)
    s = scores(0)
    for c in range(n_chunks):
        s_next = scores(c + 1) if c + 1 < n_chunks else None
        m_new = jnp.maximum(m, jnp.max(s, axis=0, keepdims=True))
        alpha = jnp.exp2(m - m_new)
        p = jnp.exp2(s - m_new).astype(BF16)
        acc = alpha * acc + jnp.dot(vt_ref[0, 0, :, c * tk:(c + 1) * tk], p, preferred_element_type=F32)
        m, s = m_new, s_next
    o_t = acc[:MLA_V] * (1.0 / acc[MLA_V:MLA_V + 1])
    o_ref[0] = o_t.T.astype(o_ref.dtype)


def _mla_attention(q, k, vt):
    b, s, _ = q.shape
    tq, tk = Tiles.mla_q, Tiles.mla_k
    rows = vt.shape[2]
    vmem = (2 * tq * MLA_QK_PAD * 2 + 2 * s * (MLA_QK_PAD + rows) * 2 + 2 * tq * MLA_V * 2
            + 6 * tq * tk * 4)
    return pl.pallas_call(
        functools.partial(_mla_attn_kernel, n_chunks=s // tk, tk=tk),
        grid=(b, MLA_HEADS, s // tq),
        in_specs=[
            pl.BlockSpec((1, tq, MLA_QK_PAD), lambda bi, h, i: (bi, i, h)),
            pl.BlockSpec((1, s, MLA_QK_PAD), lambda bi, h, i: (bi, 0, h)),
            pl.BlockSpec((1, 1, rows, s), lambda bi, h, i: (bi, h, 0, 0)),
        ],
        out_specs=pl.BlockSpec((1, tq, MLA_V), lambda bi, h, i: (bi, i, h)),
        out_shape=jax.ShapeDtypeStruct((b, s, MLA_HEADS * MLA_V), BF16),
        compiler_params=_compiler_params(("parallel", "parallel", "arbitrary"), vmem),
        name="mla_attn",
    )(q, k, vt)


def _log_sigmoid(x):
    return jnp.minimum(x, 0.0) - jnp.log(1.0 + jnp.exp(-jnp.abs(x)))


def _mlstm_masks(chunk):
    t_idx = lax.broadcasted_iota(jnp.int32, (chunk, chunk), 0)
    s_idx = lax.broadcasted_iota(jnp.int32, (chunk, chunk), 1)
    lower, upper = s_idx <= t_idx, s_idx >= t_idx
    one_hot = lambda m: jnp.where(m, 1.0, 0.0).astype(F32)
    blocked = lambda m: jnp.where(m, 0.0, NEG_INIT).astype(F32)
    return {"eye": one_hot(s_idx == t_idx), "lower": one_hot(lower), "upper": one_hot(upper),
            "neg_lower": blocked(lower), "neg_upper": blocked(upper)}


def _mlstm_chunk(q_ref, k_ref, v_ref, gates, masks, c_ref, n_row, m_st, start, chunk, li_idx, lf_idx, reverse):
    L = chunk
    q = q_ref[0, pl.ds(start, L), :]
    k = k_ref[0, pl.ds(start, L), :]
    v = v_ref[0, pl.ds(start, L), :].astype(BF16)
    li_row = gates[li_idx:li_idx + 1, :]
    lf_row = _log_sigmoid(gates[lf_idx:lf_idx + 1, :])
    li_col = jnp.sum(masks["eye"] * li_row, axis=1, keepdims=True)
    lf_col = jnp.sum(masks["eye"] * lf_row, axis=1, keepdims=True)
    visible, visible_t, blocked = ((masks["upper"], masks["lower"], masks["neg_upper"]) if reverse
                                   else (masks["lower"], masks["upper"], masks["neg_lower"]))
    a_col = jnp.sum(visible * lf_row, axis=1, keepdims=True)
    a_row = jnp.sum(visible_t * lf_col, axis=0, keepdims=True)
    g = jnp.sum(lf_row, axis=1, keepdims=True)

    dmat = (a_col - a_row) + li_row + blocked
    inter = a_col + m_st
    m = jnp.maximum(inter, jnp.max(dmat, axis=1, keepdims=True))
    w_inter = jnp.exp(inter - m)
    q16 = q.astype(BF16)
    qk = (lax.dot_general(q16, k.astype(BF16), (((1,), (1,)), ((), ())), preferred_element_type=F32)
          * jnp.exp(dmat - m))
    num = (w_inter * jnp.dot(q16, c_ref[...].astype(BF16), preferred_element_type=F32)
           + jnp.dot(qk.astype(BF16), v, preferred_element_type=F32))
    den = w_inter * jnp.sum(q * n_row, axis=1, keepdims=True) + jnp.sum(qk, axis=1, keepdims=True)
    h_out = num / jnp.maximum(jnp.abs(den), jnp.exp(-m))

    r_col = g - a_col + li_col
    m_new = jnp.maximum(g + m_st, jnp.max(r_col, axis=0, keepdims=True))
    w_old = jnp.exp(g + m_st - m_new)
    kw = k * jnp.exp(r_col - m_new)
    c_ref[...] = w_old * c_ref[...] + jnp.dot(kw.T.astype(BF16), v, preferred_element_type=F32)
    n_new = w_old * n_row + jnp.sum(kw, axis=0, keepdims=True)
    return h_out, n_new, m_new


def _mlstm_kernel(q_ref, k_ref, v_ref, gate_ref, bias_ref, mo_ref, gn_ref, o_ref,
                  hf_ref, hb_ref, cf_ref, cb_ref, *, seq, chunk, post_rows):
    nc = seq // chunk
    cf_ref[...] = jnp.zeros_like(cf_ref)
    cb_ref[...] = jnp.zeros_like(cb_ref)
    bias = bias_ref[0]
    masks = _mlstm_masks(chunk)

    def body(i, carry):
        n_f, m_f, n_b, m_b = carry
        sf = pl.multiple_of(i * chunk, chunk)
        sb = pl.multiple_of((nc - 1 - i) * chunk, chunk)
        h_f, n_f, m_f = _mlstm_chunk(q_ref, k_ref, v_ref, gate_ref[0, 0, :, pl.ds(sf, chunk)] + bias, masks,
                                     cf_ref, n_f, m_f, sf, chunk, 0, 1, False)
        hf_ref[pl.ds(sf, chunk), :] = h_f
        h_b, n_b, m_b = _mlstm_chunk(q_ref, k_ref, v_ref, gate_ref[0, 0, :, pl.ds(sb, chunk)] + bias, masks,
                                     cb_ref, n_b, m_b, sb, chunk, 2, 3, True)
        hb_ref[pl.ds(sb, chunk), :] = h_b
        return n_f, m_f, n_b, m_b

    n0 = jnp.zeros((1, ML_QK), F32)
    m0 = jnp.full((1, 1), NEG_INIT, F32)
    lax.fori_loop(0, nc, body, (n0, m0, n0, m0))

    gn = gn_ref[...]

    def post(i, carry):
        r = pl.multiple_of(i * post_rows, post_rows)
        hm = hf_ref[pl.ds(r, post_rows), :] + hb_ref[pl.ds(r, post_rows), :]
        y = _rms(hm, gn)
        o_ref[0, pl.ds(r, post_rows), :] = (y * jax.nn.sigmoid(mo_ref[0, pl.ds(r, post_rows), :])).astype(o_ref.dtype)
        return carry

    lax.fori_loop(0, seq // post_rows, post, 0)


def _mlstm(z3, gates_t, gate_bias, g_mlstm):
    b, s, _ = z3.shape
    chunk = Tiles.mlstm_chunk
    vmem = (2 * s * (2 * ML_QK + 2 * ML_V + LANES) * 4 + 2 * s * ML_V * 2 + 2 * s * ML_V * 4
            + 2 * ML_QK * ML_V * 4 + 24 * chunk * chunk * 4 + (2 << 20))
    return pl.pallas_call(
        functools.partial(_mlstm_kernel, seq=s, chunk=chunk, post_rows=Tiles.mlstm_post_rows),
        grid=(b, ML_HEADS),
        in_specs=[
            pl.BlockSpec((1, s, ML_QK), lambda bi, h: (bi, 0, EV_MQ // ML_QK + h)),
            pl.BlockSpec((1, s, ML_QK), lambda bi, h: (bi, 0, EV_MK // ML_QK + h)),
            pl.BlockSpec((1, s, ML_V), lambda bi, h: (bi, 0, EV_MV // ML_V + h)),
            pl.BlockSpec((1, 1, 4, s), lambda bi, h: (bi, h, 0, 0)),
            pl.BlockSpec((1, 4, 1), lambda bi, h: (h, 0, 0)),
            pl.BlockSpec((1, s, ML_V), lambda bi, h: (bi, 0, EV_MO // ML_V + h)),
            pl.BlockSpec((1, ML_V), lambda bi, h: (0, h)),
        ],
        out_specs=pl.BlockSpec((1, s, ML_V), lambda bi, h: (bi, 0, h)),
        out_shape=jax.ShapeDtypeStruct((b, s, ML_HEADS * ML_V), BF16),
        scratch_shapes=[
            pltpu.VMEM((s, ML_V), F32),
            pltpu.VMEM((s, ML_V), F32),
            pltpu.VMEM((ML_QK, ML_V), F32),
            pltpu.VMEM((ML_QK, ML_V), F32),
        ],
        compiler_params=_compiler_params(("parallel", "parallel"), vmem),
        name="mlstm",
    )(z3, z3, z3, gates_t, gate_bias, z3, g_mlstm.reshape(1, -1))


def _outproj_kernel(*refs, n_lhs):
    x_ref = refs[0]
    a_refs = refs[1:1 + n_lhs]
    w_refs = refs[1 + n_lhs:1 + 2 * n_lhs]
    o_ref = refs[1 + 2 * n_lhs]
    acc = x_ref[...]
    for a_ref, w_ref in zip(a_refs, w_refs):
        acc = acc + jnp.dot(a_ref[...], w_ref[...], preferred_element_type=F32)
    o_ref[...] = acc


def _outproj(x, lhs, weights):
    n, d = x.shape
    tm = Tiles.out_rows
    row = lambda i: (i, 0)
    fixed = lambda i: (0, 0)
    k_total = sum(a.shape[1] for a in lhs)
    vmem = 2 * 2 * tm * d * 4 + 2 * tm * k_total * 2 + 2 * k_total * d * 2 + tm * d * 4
    return pl.pallas_call(
        functools.partial(_outproj_kernel, n_lhs=len(lhs)),
        grid=(n // tm,),
        in_specs=([pl.BlockSpec((tm, d), row)]
                  + [pl.BlockSpec((tm, a.shape[1]), row) for a in lhs]
                  + [pl.BlockSpec(w.shape, fixed) for w in weights]),
        out_specs=pl.BlockSpec((tm, d), row),
        out_shape=jax.ShapeDtypeStruct((n, d), F32),
        compiler_params=_compiler_params(("parallel",), vmem),
        name="outproj",
    )(x, *lhs, *[w.astype(BF16) for w in weights])


def _diff_attn_kernel(slope_ref, q1_ref, q2_ref, k1_ref, k2_ref, vt_ref, pq_ref, pk_ref, lam_ref, gs_ref, o_ref,
                      vta_ref, *, n_chunks, tk, lam_init):
    h = pl.program_id(1)
    slope = slope_ref[h] * LOG2_E
    q1 = q1_ref[0]
    q2 = q2_ref[0]
    tq = q1.shape[0]
    dv = 2 * DA_HEAD
    rows = vta_ref.shape[0]

    @pl.when(pl.program_id(2) == 0)
    def _():
        vta_ref[:dv, :] = vt_ref[0, 0]
        vta_ref[dv:, :] = jnp.ones((rows - dv, vta_ref.shape[1]), BF16)
    pos_q = slope * pq_ref[0].astype(F32)
    nt = (((1,), (1,)), ((), ()))

    def scores(c):
        rows_c = slice(c * tk, (c + 1) * tk)
        dist = jnp.abs(slope * pk_ref[0, rows_c, :].astype(F32) - pos_q)
        s1 = lax.dot_general(k1_ref[0, rows_c, :], q1, nt, preferred_element_type=F32) - dist
        s2 = lax.dot_general(k2_ref[0, rows_c, :], q2, nt, preferred_element_type=F32) - dist
        return s1, s2

    def softmax_step(s, vt, m, acc):
        m_new = jnp.maximum(m, jnp.max(s, axis=0, keepdims=True))
        alpha = jnp.exp2(m - m_new)
        p = jnp.exp2(s - m_new).astype(BF16)
        return m_new, alpha * acc + jnp.dot(vt, p, preferred_element_type=F32)

    m1 = m2 = jnp.full((1, tq), NEG_INIT, F32)
    acc1 = acc2 = jnp.zeros((rows, tq), F32)
    s1, s2 = scores(0)
    for c in range(n_chunks):
        nxt = scores(c + 1) if c + 1 < n_chunks else (None, None)
        vt = vta_ref[:, c * tk:(c + 1) * tk]
        m1, acc1 = softmax_step(s1, vt, m1, acc1)
        m2, acc2 = softmax_step(s2, vt, m2, acc2)
        s1, s2 = nxt

    lam_vec = lam_ref[...]
    lam = (jnp.exp(jnp.sum(lam_vec[0:1] * lam_vec[1:2], axis=1, keepdims=True))
           - jnp.exp(jnp.sum(lam_vec[2:3] * lam_vec[3:4], axis=1, keepdims=True)) + lam_init)
    w1 = 1.0 / acc1[dv:dv + 1]
    w2 = lam / acc2[dv:dv + 1]
    o_t = acc1[:dv] * w1 - acc2[:dv] * w2
    o_ref[0] = (_rms(o_t.T, gs_ref[...]) * (1.0 - lam_init)).astype(o_ref.dtype)


def _diff_attention(zq, vt, pos_col, pos_row, slopes, lam_vec, g_sub, lam_init):
    b, s, _ = zq.shape
    tq, tk = Tiles.attn_q, Tiles.attn_k
    dv = 2 * DA_HEAD
    rows = dv + BF16_SUBLANES
    k_base = DA_HEADS * 2
    vmem = (2 * 2 * tq * DA_HEAD * 2 + 3 * s * (2 * DA_HEAD + rows) * 2 + 2 * s * LANES * 4 + 2 * tq * dv * 2
            + 14 * tq * tk * 4)
    return pl.pallas_call(
        functools.partial(_diff_attn_kernel, n_chunks=s // tk, tk=tk, lam_init=lam_init),
        grid=(b, DA_HEADS, s // tq),
        in_specs=[
            pl.BlockSpec(memory_space=pltpu.SMEM),
            pl.BlockSpec((1, tq, DA_HEAD), lambda bi, h, i: (bi, i, 2 * h)),
            pl.BlockSpec((1, tq, DA_HEAD), lambda bi, h, i: (bi, i, 2 * h + 1)),
            pl.BlockSpec((1, s, DA_HEAD), lambda bi, h, i: (bi, 0, k_base + 2 * h)),
            pl.BlockSpec((1, s, DA_HEAD), lambda bi, h, i: (bi, 0, k_base + 2 * h + 1)),
            pl.BlockSpec((1, 1, dv, s), lambda bi, h, i: (bi, h, 0, 0)),
            pl.BlockSpec((1, 1, tq), lambda bi, h, i: (bi, 0, i)),
            pl.BlockSpec((1, s, 1), lambda bi, h, i: (bi, 0, 0)),
            pl.BlockSpec((4, DA_HEAD), lambda bi, h, i: (0, 0)),
            pl.BlockSpec((1, dv), lambda bi, h, i: (0, 0)),
        ],
        out_specs=pl.BlockSpec((1, tq, dv), lambda bi, h, i: (bi, i, h)),
        out_shape=jax.ShapeDtypeStruct((b, s, DA_HEADS * dv), BF16),
        scratch_shapes=[pltpu.VMEM((rows, s), BF16)],
        compiler_params=_compiler_params(("parallel", "parallel", "arbitrary"), vmem),
        name="diff_attn",
    )(slopes, zq, zq, zq, zq, vt, pos_row, pos_col, lam_vec, g_sub.reshape(1, dv))


def _pack_even_w_in(w_in):
    d = w_in.shape[0]
    sizes = (MLA_Q_RANK, MLA_KV_RANK, MLA_ROPE, ML_HEADS * ML_QK, ML_HEADS * ML_QK, ML_HEADS * ML_V,
             ML_HEADS * ML_V, 4 * ML_HEADS)
    c_q, c_kv, k_r, m_q, m_k, m_v, m_o, m_g = jnp.split(w_in, np.cumsum(sizes)[:-1].tolist(), axis=1)
    pad = lambda w, width: jnp.concatenate([w, jnp.zeros((d, width - w.shape[1]), w.dtype)], axis=1)
    packed = jnp.concatenate([c_q, c_kv, m_v, m_o, m_q, m_k, pad(k_r, LANES), pad(m_g, LANES)], axis=1)
    assert packed.shape[1] == EV_COLS
    return packed


def _pack_w_uq(w_uq):
    r = w_uq.shape[0]
    w = w_uq.reshape(r, MLA_HEADS, MLA_NOPE + MLA_ROPE)
    w = jnp.concatenate([w, jnp.zeros((r, MLA_HEADS, MLA_QK_PAD - MLA_NOPE - MLA_ROPE), w.dtype)], axis=2)
    return w.reshape(r, MLA_HEADS * MLA_QK_PAD).astype(BF16)


def _split_w_ukv(w_ukv):
    r = w_ukv.shape[0]
    w = w_ukv.reshape(r, MLA_HEADS, MLA_NOPE + MLA_V)
    wk = w[:, :, :MLA_NOPE].reshape(r, MLA_HEADS * MLA_NOPE)
    wv_t = w[:, :, MLA_NOPE:].reshape(r, MLA_HEADS * MLA_V).T
    return wk.astype(BF16), wv_t.astype(BF16)


def _rope_tables(positions):
    inv_freq = ROPE_THETA ** (-jnp.arange(0, MLA_ROPE, 2, dtype=F32) / MLA_ROPE)
    ang = positions.astype(F32).reshape(-1, 1) * inv_freq
    cos, sin = jnp.cos(ang), jnp.sin(ang)
    zeros = jnp.zeros((ang.shape[0], LANES - MLA_ROPE), F32)
    return jnp.concatenate([cos, cos, zeros], axis=1), jnp.concatenate([-sin, sin, zeros], axis=1)


def _even_mixer(x, b, s, positions, norm, w_in, g_cq, w_uq, g_ckv, w_ukv, b_gates, g_mlstm, w_o):
    n = b * s
    col_scale = jnp.ones((EV_COLS,), F32).at[EV_MK:EV_MK + ML_HEADS * ML_QK].set(ML_QK ** -0.5)
    z = _normproj(x, norm, _pack_even_w_in(w_in.astype(BF16)), col_scale, F32, Tiles.proj_col_steps_even)

    cos_t, sin_t = _rope_tables(positions)
    wk, wv_t = _split_w_ukv(w_ukv)
    q, k, vt = _mla_prep(z, b, s, g_cq, g_ckv, _pack_w_uq(w_uq), wk, wv_t, cos_t, sin_t)
    a_out = _mla_attention(q.reshape(b, s, -1), k.reshape(b, s, -1), vt)

    z3 = z.reshape(b, s, EV_COLS)
    gates_t = z3[:, :, EV_MG:EV_MG + 4 * ML_HEADS].reshape(b, s, 4, ML_HEADS).transpose(0, 3, 2, 1)
    gate_bias = b_gates.astype(F32).reshape(4, ML_HEADS).T.reshape(ML_HEADS, 4, 1)
    m_out = _mlstm(z3, gates_t, gate_bias, g_mlstm)

    split = MLA_HEADS * MLA_V
    return _outproj(x, [a_out.reshape(n, -1), m_out.reshape(n, -1)], [w_o[:split], w_o[split:]])


def _odd_mixer(x, b, s, positions, norm, w_in, lam_q1, lam_k1, lam_q2, lam_k2, g_sub, w_o, lam_init):
    n = b * s
    width = DA_HEADS * 2 * DA_HEAD
    zq, vt = _qkv_proj(x, b, s, norm, w_in, LOG2_E * DA_HEAD ** -0.5, DA_HEADS, 2 * DA_HEAD)
    zq = zq.reshape(b, s, 2 * width)
    slopes = jnp.asarray([2.0 ** (-8.0 * (h + 1) / DA_HEADS) for h in range(DA_HEADS)], dtype=F32)
    lam_vec = jnp.stack([lam_q1, lam_k1, lam_q2, lam_k2]).astype(F32)
    pos = positions - positions[:, :1]
    o = _diff_attention(zq, vt, pos.reshape(b, s, 1), pos.reshape(b, 1, s), slopes, lam_vec, g_sub, lam_init)
    return _outproj(x, [o.reshape(n, width)], [w_o])


def kernel(x, positions, l0_ffn1_norm, l0_ffn1_w_gu, l0_ffn1_w_down, l0_mix_norm, l0_w_in, l0_g_cq, l0_w_uq, l0_g_ckv, l0_w_ukv, l0_b_gates, l0_g_mlstm, l0_w_o, l0_ffn2_norm, l0_ffn2_w_gu, l0_ffn2_w_down, l1_ffn1_norm, l1_ffn1_w_gu, l1_ffn1_w_down, l1_mix_norm, l1_w_in, l1_lam_q1, l1_lam_k1, l1_lam_q2, l1_lam_k2, l1_g_sub, l1_w_o, l1_ffn2_norm, l1_ffn2_w_gu, l1_ffn2_w_down, final_norm):
    b, s, d = x.shape
    h = x.reshape(b * s, d)
    h = _ffn(h, l0_ffn1_norm, l0_ffn1_w_gu, l0_ffn1_w_down)
    h = _even_mixer(h, b, s, positions, l0_mix_norm, l0_w_in, l0_g_cq, l0_w_uq, l0_g_ckv, l0_w_ukv, l0_b_gates,
                    l0_g_mlstm, l0_w_o)
    h = _ffn(h, l0_ffn2_norm, l0_ffn2_w_gu, l0_ffn2_w_down)
    h = _ffn(h, l1_ffn1_norm, l1_ffn1_w_gu, l1_ffn1_w_down)
    lam_init = 0.8 - 0.6 * math.exp(-0.3 * 1)
    h = _odd_mixer(h, b, s, positions, l1_mix_norm, l1_w_in, l1_lam_q1, l1_lam_k1, l1_lam_q2, l1_lam_k2, l1_g_sub,
                   l1_w_o, lam_init)
    h = _ffn(h, l1_ffn2_norm, l1_ffn2_w_gu, l1_ffn2_w_down, final_gain=final_norm)
    return h.reshape(b, s, d)
```

```python
import functools
import math

import jax
import jax.numpy as jnp
import numpy as np
from jax import lax
from jax.experimental import pallas as pl
from jax.experimental.pallas import tpu as pltpu

F32 = jnp.float32
BF16 = jnp.bfloat16

D_MODEL = 2048
MLA_HEADS = 8
MLA_Q_RANK = 512
MLA_KV_RANK = 512
MLA_NOPE = 128
MLA_ROPE = 64
MLA_V = 128
ROPE_THETA = 10000.0
ML_HEADS = 4
ML_QK = 128
ML_V = 256
DA_HEADS = 8
DA_HEAD = 128
D_FF = 5632
EPS = 1e-6
NEG_INIT = -1e30
LOG2_E = math.log2(math.e)

LANES = 128
BF16_SUBLANES = 16
MLA_QK_PAD = 256
V7X_VMEM_BYTES = 64 * 1024 * 1024

EV_CQ = 0
EV_CKV = EV_CQ + MLA_Q_RANK
EV_MV = EV_CKV + MLA_KV_RANK
EV_MO = EV_MV + ML_HEADS * ML_V
EV_MQ = EV_MO + ML_HEADS * ML_V
EV_MK = EV_MQ + ML_HEADS * ML_QK
EV_KR = EV_MK + ML_HEADS * ML_QK
EV_MG = EV_KR + LANES
EV_COLS = EV_MG + LANES
assert EV_MV % ML_V == 0 and EV_MO % ML_V == 0 and EV_MQ % ML_QK == 0 and EV_MK % ML_QK == 0


class Tiles:
    ffn_rows = 1024
    ffn_cols = 512
    proj_rows = 512
    proj_col_steps_even = 2
    prep_rows = 512
    out_rows = 512
    mla_q = 2048
    mla_k = 512
    attn_q = 1024
    attn_k = 512
    mlstm_chunk = 256
    mlstm_post_rows = 512


def _compiler_params(semantics, vmem_bytes):
    return pltpu.CompilerParams(dimension_semantics=semantics,
                                vmem_limit_bytes=int(min(vmem_bytes, V7X_VMEM_BYTES - (6 << 20))))


def _rms(x, gain):
    y = x * lax.rsqrt(jnp.mean(x * x, axis=-1, keepdims=True) + EPS)
    return y * gain


def _ffn_kernel(*refs, n_ff_steps, final):
    if final:
        x_ref, g_ref, wg_ref, wu_ref, wd_ref, gf_ref, o_ref, xn_ref = refs
    else:
        x_ref, g_ref, wg_ref, wu_ref, wd_ref, o_ref, xn_ref = refs
    j = pl.program_id(1)

    def branch(xn):
        gate = jnp.dot(xn, wg_ref[...], preferred_element_type=F32)
        up = jnp.dot(xn, wu_ref[...], preferred_element_type=F32)
        act = (0.5 * (gate * jax.nn.sigmoid(gate)) * up).astype(BF16)
        return jnp.dot(act, wd_ref[...], preferred_element_type=F32)

    @pl.when(j == 0)
    def _():
        x = x_ref[...]
        xn = _rms(x, g_ref[...]).astype(BF16)
        xn_ref[...] = xn
        o_ref[...] = x + branch(xn)

    @pl.when(j > 0)
    def _():
        o_ref[...] += branch(xn_ref[...])

    if final:
        @pl.when(j == n_ff_steps - 1)
        def _():
            o_ref[...] = _rms(o_ref[...], gf_ref[...])


def _ffn(x, gain, w_gu, w_down, final_gain=None):
    n, d = x.shape
    tm, tf = Tiles.ffn_rows, Tiles.ffn_cols
    nf = D_FF // tf
    wgu = w_gu.astype(BF16)
    wd = w_down.astype(BF16)
    final = final_gain is not None
    in_specs = [
        pl.BlockSpec((tm, d), lambda i, j: (i, 0)),
        pl.BlockSpec((1, d), lambda i, j: (0, 0)),
        pl.BlockSpec((d, tf), lambda i, j: (0, j)),
        pl.BlockSpec((d, tf), lambda i, j: (0, j + nf)),
        pl.BlockSpec((tf, d), lambda i, j: (j, 0)),
    ]
    args = [x, gain.reshape(1, d), wgu, wgu, wd]
    if final:
        in_specs.append(pl.BlockSpec((1, d), lambda i, j: (0, 0)))
        args.append(final_gain.reshape(1, d))
    vmem = (2 * 2 * tm * d * 4 + tm * d * 2 + 2 * 3 * d * tf * 2 + 4 * tm * tf * 4 + tm * d * 4)
    return pl.pallas_call(
        functools.partial(_ffn_kernel, n_ff_steps=nf, final=final),
        grid=(n // tm, nf),
        in_specs=in_specs,
        out_specs=pl.BlockSpec((tm, d), lambda i, j: (i, 0)),
        out_shape=jax.ShapeDtypeStruct((n, d), F32),
        scratch_shapes=[pltpu.VMEM((tm, d), BF16)],
        compiler_params=_compiler_params(("parallel", "arbitrary"), vmem),
        name="ffn",
    )(*args)


def _normproj_kernel(x_ref, g_ref, w_ref, cs_ref, o_ref, xn_ref):
    def project(xn):
        acc = jnp.dot(xn, w_ref[...], preferred_element_type=F32)
        o_ref[...] = (acc * cs_ref[...]).astype(o_ref.dtype)

    @pl.when(pl.program_id(1) == 0)
    def _():
        xn = _rms(x_ref[...], g_ref[...]).astype(BF16)
        xn_ref[...] = xn
        project(xn)

    @pl.when(pl.program_id(1) > 0)
    def _():
        project(xn_ref[...])


def _normproj(x, gain, w, col_scale, out_dtype, col_steps):
    n, d = x.shape
    cols = w.shape[1]
    tm, tn = Tiles.proj_rows, cols // col_steps
    assert tn * col_steps == cols and tn % LANES == 0
    out_bytes = jnp.dtype(out_dtype).itemsize
    vmem = 2 * tm * d * 4 + tm * d * 2 + 2 * d * tn * 2 + 2 * tm * tn * out_bytes + 2 * tm * tn * 4
    return pl.pallas_call(
        _normproj_kernel,
        grid=(n // tm, col_steps),
        in_specs=[
            pl.BlockSpec((tm, d), lambda i, j: (i, 0)),
            pl.BlockSpec((1, d), lambda i, j: (0, 0)),
            pl.BlockSpec((d, tn), lambda i, j: (0, j)),
            pl.BlockSpec((1, tn), lambda i, j: (0, j)),
        ],
        out_specs=pl.BlockSpec((tm, tn), lambda i, j: (i, j)),
        out_shape=jax.ShapeDtypeStruct((n, cols), out_dtype),
        scratch_shapes=[pltpu.VMEM((tm, d), BF16)],
        compiler_params=_compiler_params(("parallel", "arbitrary"), vmem),
        name="normproj",
    )(x, gain.reshape(1, d), w.astype(BF16), col_scale.reshape(1, cols))


def _qkv_proj_kernel(x_ref, g_ref, w_ref, o_ref, vt_ref, xn_ref, *, q_scale, heads, head_width):
    j = pl.program_id(1)

    @pl.when(j == 0)
    def _():
        xn_ref[...] = _rms(x_ref[...], g_ref[...]).astype(BF16)

    def project():
        return jnp.dot(xn_ref[...], w_ref[...], preferred_element_type=F32)

    @pl.when(j == 0)
    def _():
        o_ref[...] = (project() * q_scale).astype(o_ref.dtype)

    @pl.when(j == 1)
    def _():
        o_ref[...] = project().astype(o_ref.dtype)

    @pl.when(j == 2)
    def _():
        acc = project()
        for h in range(heads):
            vt_ref[0, h] = acc[:, h * head_width:(h + 1) * head_width].T.astype(vt_ref.dtype)


def _qkv_proj(x, b, s, gain, w, q_scale, heads, head_width):
    n, d = x.shape
    width = heads * head_width
    assert w.shape == (d, 3 * width)
    tm = Tiles.proj_rows
    tiles_per_seq = s // tm
    vmem = 2 * tm * d * 4 + tm * d * 2 + 2 * d * width * 2 + 4 * tm * width * 2 + 3 * tm * width * 4
    return pl.pallas_call(
        functools.partial(_qkv_proj_kernel, q_scale=q_scale, heads=heads, head_width=head_width),
        grid=(n // tm, 3),
        in_specs=[
            pl.BlockSpec((tm, d), lambda i, j: (i, 0)),
            pl.BlockSpec((1, d), lambda i, j: (0, 0)),
            pl.BlockSpec((d, width), lambda i, j: (0, j)),
        ],
        out_specs=[
            pl.BlockSpec((tm, width), lambda i, j: (i, jnp.minimum(j, 1))),
            pl.BlockSpec((1, heads, head_width, tm), lambda i, j: (i // tiles_per_seq, 0, 0, i % tiles_per_seq)),
        ],
        out_shape=[
            jax.ShapeDtypeStruct((n, 2 * width), BF16),
            jax.ShapeDtypeStruct((b, heads, head_width, s), BF16),
        ],
        scratch_shapes=[pltpu.VMEM((tm, d), BF16)],
        compiler_params=_compiler_params(("parallel", "arbitrary"), vmem),
        name="qkv_proj",
    )(x, gain.reshape(1, d), w.astype(BF16))


def _rope_tile(x, cos_t, sin_t):
    lane = lax.broadcasted_iota(jnp.int32, x.shape, 1)
    half = MLA_ROPE // 2
    partner = jnp.where(lane % MLA_ROPE < half, pltpu.roll(x, LANES - half, 1), pltpu.roll(x, half, 1))
    return x * cos_t + partner * sin_t


def _mla_prep_kernel(cq_ref, ckv_ref, kr_ref, gq_ref, gkv_ref, wq_ref, wk_ref, wvt_ref, cos_ref, sin_ref,
                     q_ref, k_ref, vt_ref):
    cos_t = cos_ref[...]
    sin_t = sin_ref[...]
    cqn = _rms(cq_ref[...], gq_ref[...]).astype(BF16)
    ckvn_f32 = _rms(ckv_ref[...], gkv_ref[...])
    ckvn = ckvn_f32.astype(BF16)
    scale = LOG2_E * (MLA_NOPE + MLA_ROPE) ** -0.5
    q = jnp.dot(cqn, wq_ref[...], preferred_element_type=F32)
    kn = jnp.dot(ckvn, wk_ref[...], preferred_element_type=F32)
    v_t = jnp.dot(wvt_ref[...], ckvn_f32.T.astype(BF16), preferred_element_type=F32)
    ones = jnp.ones((BF16_SUBLANES, v_t.shape[1]), BF16)
    lane = lax.broadcasted_iota(jnp.int32, cos_t.shape, 1)
    k_rope = jnp.where(lane < MLA_ROPE, _rope_tile(kr_ref[...], cos_t, sin_t), 0.0).astype(BF16)
    for h in range(MLA_HEADS):
        base = h * MLA_QK_PAD
        q_ref[:, base:base + MLA_NOPE] = (q[:, base:base + MLA_NOPE] * scale).astype(BF16)
        q_rope = _rope_tile(q[:, base + MLA_NOPE:base + MLA_QK_PAD], cos_t, sin_t)
        q_ref[:, base + MLA_NOPE:base + MLA_QK_PAD] = (q_rope * scale).astype(BF16)
        k_ref[:, base:base + MLA_NOPE] = kn[:, h * MLA_NOPE:(h + 1) * MLA_NOPE].astype(BF16)
        k_ref[:, base + MLA_NOPE:base + MLA_QK_PAD] = k_rope
        vt_ref[0, h, :MLA_V, :] = v_t[h * MLA_V:(h + 1) * MLA_V].astype(BF16)
        vt_ref[0, h, MLA_V:, :] = ones


def _mla_prep(z, b, s, g_cq, g_ckv, wq, wk, wv_t, cos_t, sin_t):
    n = z.shape[0]
    tm = Tiles.prep_rows
    tiles_per_seq = s // tm
    qk_cols = MLA_HEADS * MLA_QK_PAD
    v_cols = MLA_HEADS * MLA_V
    vt_rows = MLA_V + BF16_SUBLANES
    row = lambda i: (i, 0)
    fixed = lambda i: (0, 0)
    vmem = (2 * tm * (MLA_Q_RANK + MLA_KV_RANK + 3 * LANES) * 4 + 2 * MLA_Q_RANK * (qk_cols + 2 * v_cols) * 2
            + 2 * tm * (2 * qk_cols + v_cols) * 2 + 3 * tm * qk_cols * 4)
    return pl.pallas_call(
        _mla_prep_kernel,
        grid=(n // tm,),
        in_specs=[
            pl.BlockSpec((tm, MLA_Q_RANK), lambda i: (i, EV_CQ // MLA_Q_RANK)),
            pl.BlockSpec((tm, MLA_KV_RANK), lambda i: (i, EV_CKV // MLA_KV_RANK)),
            pl.BlockSpec((tm, LANES), lambda i: (i, EV_KR // LANES)),
            pl.BlockSpec((1, MLA_Q_RANK), fixed),
            pl.BlockSpec((1, MLA_KV_RANK), fixed),
            pl.BlockSpec((MLA_Q_RANK, qk_cols), fixed),
            pl.BlockSpec((MLA_KV_RANK, v_cols), fixed),
            pl.BlockSpec((v_cols, MLA_KV_RANK), fixed),
            pl.BlockSpec((tm, LANES), row),
            pl.BlockSpec((tm, LANES), row),
        ],
        out_specs=[
            pl.BlockSpec((tm, qk_cols), row),
            pl.BlockSpec((tm, qk_cols), row),
            pl.BlockSpec((1, MLA_HEADS, vt_rows, tm), lambda i: (i // tiles_per_seq, 0, 0, i % tiles_per_seq)),
        ],
        out_shape=[
            jax.ShapeDtypeStruct((n, qk_cols), BF16),
            jax.ShapeDtypeStruct((n, qk_cols), BF16),
            jax.ShapeDtypeStruct((b, MLA_HEADS, vt_rows, s), BF16),
        ],
        compiler_params=_compiler_params(("parallel",), vmem),
        name="mla_prep",
    )(z, z, z, g_cq.reshape(1, -1), g_ckv.reshape(1, -1), wq, wk, wv_t, cos_t, sin_t)


def _mla_attn_kernel(q_ref, k_ref, vt_ref, o_ref, *, n_chunks, tk):
    q = q_ref[0]
    tq = q.shape[0]
    rows = vt_ref.shape[2]

    def scores(c):
        k = k_ref[0, c * tk:(c + 1) * tk, :]
        return lax.dot_general(k, q, (((1,), (1,)), ((), ())), preferred_element_type=F32)

    m = jnp.full((1, tq), NEG_INIT, F32)
    acc = jnp.zeros((rows, tq), F32)
    s = scores(0)
    for c in range(n_chunks):
        s_next = scores(c + 1) if c + 1 < n_chunks else None
        m_new = jnp.maximum(m, jnp.max(s, axis=0, keepdims=True))
        alpha = jnp.exp2(m - m_new)
        p = jnp.exp2(s - m_new).astype(BF16)
        acc = alpha * acc + jnp.dot(vt_ref[0, 0, :, c * tk:(c + 1) * tk], p, preferred_element_type=F32)
        m, s = m_new, s_next
    o_t = acc[:MLA_V] * (1.0 / acc[MLA_V:MLA_V + 1])
    o_ref[0] = o_t.T.astype(o_ref.dtype)


def _mla_attention(q, k, vt):
    b, s, _ = q.shape
    tq, tk = Tiles.mla_q, Tiles.mla_k
    rows = vt.shape[2]
    vmem = (2 * tq * MLA_QK_PAD * 2 + 2 * s * (MLA_QK_PAD + rows) * 2 + 2 * tq * MLA_V * 2
            + 6 * tq * tk * 4)
    return pl.pallas_call(
        functools.partial(_mla_attn_kernel, n_chunks=s // tk, tk=tk),
        grid=(b, MLA_HEADS, s // tq),
        in_specs=[
            pl.BlockSpec((1, tq, MLA_QK_PAD), lambda bi, h, i: (bi, i, h)),
            pl.BlockSpec((1, s, MLA_QK_PAD), lambda bi, h, i: (bi, 0, h)),
            pl.BlockSpec((1, 1, rows, s), lambda bi, h, i: (bi, h, 0, 0)),
        ],
        out_specs=pl.BlockSpec((1, tq, MLA_V), lambda bi, h, i: (bi, i, h)),
        out_shape=jax.ShapeDtypeStruct((b, s, MLA_HEADS * MLA_V), BF16),
        compiler_params=_compiler_params(("parallel", "parallel", "arbitrary"), vmem),
        name="mla_attn",
    )(q, k, vt)


def _log_sigmoid(x):
    return jnp.minimum(x, 0.0) - jnp.log(1.0 + jnp.exp(-jnp.abs(x)))


def _mlstm_masks(chunk):
    t_idx = lax.broadcasted_iota(jnp.int32, (chunk, chunk), 0)
    s_idx = lax.broadcasted_iota(jnp.int32, (chunk, chunk), 1)
    lower, upper = s_idx <= t_idx, s_idx >= t_idx
    one_hot = lambda m: jnp.where(m, 1.0, 0.0).astype(F32)
    blocked = lambda m: jnp.where(m, 0.0, NEG_INIT).astype(F32)
    return {"eye": one_hot(s_idx == t_idx), "lower": one_hot(lower), "upper": one_hot(upper),
            "neg_lower": blocked(lower), "neg_upper": blocked(upper)}


def _mlstm_chunk(q_ref, k_ref, v_ref, gates, masks, c_ref, n_row, m_st, start, chunk, li_idx, lf_idx, reverse):
    L = chunk
    q = q_ref[0, pl.ds(start, L), :]
    k = k_ref[0, pl.ds(start, L), :]
    v = v_ref[0, pl.ds(start, L), :].astype(BF16)
    li_row = gates[li_idx:li_idx + 1, :]
    lf_row = _log_sigmoid(gates[lf_idx:lf_idx + 1, :])
    li_col = jnp.sum(masks["eye"] * li_row, axis=1, keepdims=True)
    lf_col = jnp.sum(masks["eye"] * lf_row, axis=1, keepdims=True)
    visible, visible_t, blocked = ((masks["upper"], masks["lower"], masks["neg_upper"]) if reverse
                                   else (masks["lower"], masks["upper"], masks["neg_lower"]))
    a_col = jnp.sum(visible * lf_row, axis=1, keepdims=True)
    a_row = jnp.sum(visible_t * lf_col, axis=0, keepdims=True)
    g = jnp.sum(lf_row, axis=1, keepdims=True)

    dmat = (a_col - a_row) + li_row + blocked
    inter = a_col + m_st
    m = jnp.maximum(inter, jnp.max(dmat, axis=1, keepdims=True))
    w_inter = jnp.exp(inter - m)
    q16 = q.astype(BF16)
    qk = (lax.dot_general(q16, k.astype(BF16), (((1,), (1,)), ((), ())), preferred_element_type=F32)
          * jnp.exp(dmat - m))
    num = (w_inter * jnp.dot(q16, c_ref[...].astype(BF16), preferred_element_type=F32)
           + jnp.dot(qk.astype(BF16), v, preferred_element_type=F32))
    den = w_inter * jnp.sum(q * n_row, axis=1, keepdims=True) + jnp.sum(qk, axis=1, keepdims=True)
    h_out = num / jnp.maximum(jnp.abs(den), jnp.exp(-m))

    r_col = g - a_col + li_col
    m_new = jnp.maximum(g + m_st, jnp.max(r_col, axis=0, keepdims=True))
    w_old = jnp.exp(g + m_st - m_new)
    kw = k * jnp.exp(r_col - m_new)
    c_ref[...] = w_old * c_ref[...] + jnp.dot(kw.T.astype(BF16), v, preferred_element_type=F32)
    n_new = w_old * n_row + jnp.sum(kw, axis=0, keepdims=True)
    return h_out, n_new, m_new


def _mlstm_kernel(q_ref, k_ref, v_ref, gate_ref, bias_ref, mo_ref, gn_ref, o_ref,
                  hf_ref, hb_ref, cf_ref, cb_ref, *, seq, chunk, post_rows):
    nc = seq // chunk
    cf_ref[...] = jnp.zeros_like(cf_ref)
    cb_ref[...] = jnp.zeros_like(cb_ref)
    bias = bias_ref[0]
    masks = _mlstm_masks(chunk)

    def body(i, carry):
        n_f, m_f, n_b, m_b = carry
        sf = pl.multiple_of(i * chunk, chunk)
        sb = pl.multiple_of((nc - 1 - i) * chunk, chunk)
        h_f, n_f, m_f = _mlstm_chunk(q_ref, k_ref, v_ref, gate_ref[0, 0, :, pl.ds(sf, chunk)] + bias, masks,
                                     cf_ref, n_f, m_f, sf, chunk, 0, 1, False)
        hf_ref[pl.ds(sf, chunk), :] = h_f
        h_b, n_b, m_b = _mlstm_chunk(q_ref, k_ref, v_ref, gate_ref[0, 0, :, pl.ds(sb, chunk)] + bias, masks,
                                     cb_ref, n_b, m_b, sb, chunk, 2, 3, True)
        hb_ref[pl.ds(sb, chunk), :] = h_b
        return n_f, m_f, n_b, m_b

    n0 = jnp.zeros((1, ML_QK), F32)
    m0 = jnp.full((1, 1), NEG_INIT, F32)
    lax.fori_loop(0, nc, body, (n0, m0, n0, m0))

    gn = gn_ref[...]

    def post(i, carry):
        r = pl.multiple_of(i * post_rows, post_rows)
        hm = hf_ref[pl.ds(r, post_rows), :] + hb_ref[pl.ds(r, post_rows), :]
        y = _rms(hm, gn)
        o_ref[0, pl.ds(r, post_rows), :] = (y * jax.nn.sigmoid(mo_ref[0, pl.ds(r, post_rows), :])).astype(o_ref.dtype)
        return carry

    lax.fori_loop(0, seq // post_rows, post, 0)


def _mlstm(z3, gates_t, gate_bias, g_mlstm):
    b, s, _ = z3.shape
    chunk = Tiles.mlstm_chunk
    vmem = (2 * s * (2 * ML_QK + 2 * ML_V + LANES) * 4 + 2 * s * ML_V * 2 + 2 * s * ML_V * 4
            + 2 * ML_QK * ML_V * 4 + 24 * chunk * chunk * 4 + (2 << 20))
    return pl.pallas_call(
        functools.partial(_mlstm_kernel, seq=s, chunk=chunk, post_rows=Tiles.mlstm_post_rows),
        grid=(b, ML_HEADS),
        in_specs=[
            pl.BlockSpec((1, s, ML_QK), lambda bi, h: (bi, 0, EV_MQ // ML_QK + h)),
            pl.BlockSpec((1, s, ML_QK), lambda bi, h: (bi, 0, EV_MK // ML_QK + h)),
            pl.BlockSpec((1, s, ML_V), lambda bi, h: (bi, 0, EV_MV // ML_V + h)),
            pl.BlockSpec((1, 1, 4, s), lambda bi, h: (bi, h, 0, 0)),
            pl.BlockSpec((1, 4, 1), lambda bi, h: (h, 0, 0)),
            pl.BlockSpec((1, s, ML_V), lambda bi, h: (bi, 0, EV_MO // ML_V + h)),
            pl.BlockSpec((1, ML_V), lambda bi, h: (0, h)),
        ],
        out_specs=pl.BlockSpec((1, s, ML_V), lambda bi, h: (bi, 0, h)),
        out_shape=jax.ShapeDtypeStruct((b, s, ML_HEADS * ML_V), BF16),
        scratch_shapes=[
            pltpu.VMEM((s, ML_V), F32),
            pltpu.VMEM((s, ML_V), F32),
            pltpu.VMEM((ML_QK, ML_V), F32),
            pltpu.VMEM((ML_QK, ML_V), F32),
        ],
        compiler_params=_compiler_params(("parallel", "parallel"), vmem),
        name="mlstm",
    )(z3, z3, z3, gates_t, gate_bias, z3, g_mlstm.reshape(1, -1))


def _outproj_kernel(*refs, n_lhs):
    x_ref = refs[0]
    a_refs = refs[1:1 + n_lhs]
    w_refs = refs[1 + n_lhs:1 + 2 * n_lhs]
    o_ref = refs[1 + 2 * n_lhs]
    acc = x_ref[...]
    for a_ref, w_ref in zip(a_refs, w_refs):
        acc = acc + jnp.dot(a_ref[...], w_ref[...], preferred_element_type=F32)
    o_ref[...] = acc


def _outproj(x, lhs, weights):
    n, d = x.shape
    tm = Tiles.out_rows
    row = lambda i: (i, 0)
    fixed = lambda i: (0, 0)
    k_total = sum(a.shape[1] for a in lhs)
    vmem = 2 * 2 * tm * d * 4 + 2 * tm * k_total * 2 + 2 * k_total * d * 2 + tm * d * 4
    return pl.pallas_call(
        functools.partial(_outproj_kernel, n_lhs=len(lhs)),
        grid=(n // tm,),
        in_specs=([pl.BlockSpec((tm, d), row)]
                  + [pl.BlockSpec((tm, a.shape[1]), row) for a in lhs]
                  + [pl.BlockSpec(w.shape, fixed) for w in weights]),
        out_specs=pl.BlockSpec((tm, d), row),
        out_shape=jax.ShapeDtypeStruct((n, d), F32),
        compiler_params=_compiler_params(("parallel",), vmem),
        name="outproj",
    )(x, *lhs, *[w.astype(BF16) for w in weights])


def _diff_attn_kernel(slope_ref, q1_ref, q2_ref, k1_ref, k2_ref, vt_ref, pq_ref, pk_ref, lam_ref, gs_ref, o_ref,
                      vta_ref, *, n_chunks, tk, lam_init):
    h = pl.program_id(1)
    slope = slope_ref[h] * LOG2_E
    q1 = q1_ref[0]
    q2 = q2_ref[0]
    tq = q1.shape[0]
    dv = 2 * DA_HEAD
    rows = vta_ref.shape[0]

    @pl.when(pl.program_id(2) == 0)
    def _():
        vta_ref[:dv, :] = vt_ref[0, 0]
        vta_ref[dv:, :] = jnp.ones((rows - dv, vta_ref.shape[1]), BF16)
    pos_q = slope * pq_ref[0].astype(F32)
    nt = (((1,), (1,)), ((), ()))

    def scores(c):
        rows_c = slice(c * tk, (c + 1) * tk)
        dist = jnp.abs(slope * pk_ref[0, rows_c, :].astype(F32) - pos_q)
        s1 = lax.dot_general(k1_ref[0, rows_c, :], q1, nt, preferred_element_type=F32) - dist
        s2 = lax.dot_general(k2_ref[0, rows_c, :], q2, nt, preferred_element_type=F32) - dist
        return s1, s2

    def softmax_step(s, vt, m, acc):
        m_new = jnp.maximum(m, jnp.max(s, axis=0, keepdims=True))
        alpha = jnp.exp2(m - m_new)
        p = jnp.exp2(s - m_new).astype(BF16)
        return m_new, alpha * acc + jnp.dot(vt, p, preferred_element_type=F32)

    m1 = m2 = jnp.full((1, tq), NEG_INIT, F32)
    acc1 = acc2 = jnp.zeros((rows, tq), F32)
    s1, s2 = scores(0)
    for c in range(n_chunks):
        nxt = scores(c + 1) if c + 1 < n_chunks else (None, None)
        vt = vta_ref[:, c * tk:(c + 1) * tk]
        m1, acc1 = softmax_step(s1, vt, m1, acc1)
        m2, acc2 = softmax_step(s2, vt, m2, acc2)
        s1, s2 = nxt

    lam_vec = lam_ref[...]
    lam = (jnp.exp(jnp.sum(lam_vec[0:1] * lam_vec[1:2], axis=1, keepdims=True))
           - jnp.exp(jnp.sum(lam_vec[2:3] * lam_vec[3:4], axis=1, keepdims=True)) + lam_init)
    w1 = 1.0 / acc1[dv:dv + 1]
    w2 = lam / acc2[dv:dv + 1]
    o_t = acc1[:dv] * w1 - acc2[:dv] * w2
    o_ref[0] = (_rms(o_t.T, gs_ref[...]) * (1.0 - lam_init)).astype(o_ref.dtype)


def _diff_attention(zq, vt, pos_col, pos_row, slopes, lam_vec, g_sub, lam_init):
    b, s, _ = zq.shape
    tq, tk = Tiles.attn_q, Tiles.attn_k
    dv = 2 * DA_HEAD
    rows = dv + BF16_SUBLANES
    k_base = DA_HEADS * 2
    vmem = (2 * 2 * tq * DA_HEAD * 2 + 3 * s * (2 * DA_HEAD + rows) * 2 + 2 * s * LANES * 4 + 2 * tq * dv * 2
            + 14 * tq * tk * 4)
    return pl.pallas_call(
        functools.partial(_diff_attn_kernel, n_chunks=s // tk, tk=tk, lam_init=lam_init),
        grid=(b, DA_HEADS, s // tq),
        in_specs=[
            pl.BlockSpec(memory_space=pltpu.SMEM),
            pl.BlockSpec((1, tq, DA_HEAD), lambda bi, h, i: (bi, i, 2 * h)),
            pl.BlockSpec((1, tq, DA_HEAD), lambda bi, h, i: (bi, i, 2 * h + 1)),
            pl.BlockSpec((1, s, DA_HEAD), lambda bi, h, i: (bi, 0, k_base + 2 * h)),
            pl.BlockSpec((1, s, DA_HEAD), lambda bi, h, i: (bi, 0, k_base + 2 * h + 1)),
            pl.BlockSpec((1, 1, dv, s), lambda bi, h, i: (bi, h, 0, 0)),
            pl.BlockSpec((1, 1, tq), lambda bi, h, i: (bi, 0, i)),
            pl.BlockSpec((1, s, 1), lambda bi, h, i: (bi, 0, 0)),
            pl.BlockSpec((4, DA_HEAD), lambda bi, h, i: (0, 0)),
            pl.BlockSpec((1, dv), lambda bi, h, i: (0, 0)),
        ],
        out_specs=pl.BlockSpec((1, tq, dv), lambda bi, h, i: (bi, i, h)),
        out_shape=jax.ShapeDtypeStruct((b, s, DA_HEADS * dv), BF16),
        scratch_shapes=[pltpu.VMEM((rows, s), BF16)],
        compiler_params=_compiler_params(("parallel", "parallel", "arbitrary"), vmem),
        name="diff_attn",
    )(slopes, zq, zq, zq, zq, vt, pos_row, pos_col, lam_vec, g_sub.reshape(1, dv))


def _pack_even_w_in(w_in):
    d = w_in.shape[0]
    sizes = (MLA_Q_RANK, MLA_KV_RANK, MLA_ROPE, ML_HEADS * ML_QK, ML_HEADS * ML_QK, ML_HEADS * ML_V,
             ML_HEADS * ML_V, 4 * ML_HEADS)
    c_q, c_kv, k_r, m_q, m_k, m_v, m_o, m_g = jnp.split(w_in, np.cumsum(sizes)[:-1].tolist(), axis=1)
    pad = lambda w, width: jnp.concatenate([w, jnp.zeros((d, width - w.shape[1]), w.dtype)], axis=1)
    packed = jnp.concatenate([c_q, c_kv, m_v, m_o, m_q, m_k, pad(k_r, LANES), pad(m_g, LANES)], axis=1)
    assert packed.shape[1] == EV_COLS
    return packed


def _pack_w_uq(w_uq):
    r = w_uq.shape[0]
    w = w_uq.reshape(r, MLA_HEADS, MLA_NOPE + MLA_ROPE)
    w = jnp.concatenate([w, jnp.zeros((r, MLA_HEADS, MLA_QK_PAD - MLA_NOPE - MLA_ROPE), w.dtype)], axis=2)
    return w.reshape(r, MLA_HEADS * MLA_QK_PAD).astype(BF16)


def _split_w_ukv(w_ukv):
    r = w_ukv.shape[0]
    w = w_ukv.reshape(r, MLA_HEADS, MLA_NOPE + MLA_V)
    wk = w[:, :, :MLA_NOPE].reshape(r, MLA_HEADS * MLA_NOPE)
    wv_t = w[:, :, MLA_NOPE:].reshape(r, MLA_HEADS * MLA_V).T
    return wk.astype(BF16), wv_t.astype(BF16)


def _rope_tables(positions):
    inv_freq = ROPE_THETA ** (-jnp.arange(0, MLA_ROPE, 2, dtype=F32) / MLA_ROPE)
    ang = positions.astype(F32).reshape(-1, 1) * inv_freq
    cos, sin = jnp.cos(ang), jnp.sin(ang)
    zeros = jnp.zeros((ang.shape[0], LANES - MLA_ROPE), F32)
    return jnp.concatenate([cos, cos, zeros], axis=1), jnp.concatenate([-sin, sin, zeros], axis=1)


def _even_mixer(x, b, s, positions, norm, w_in, g_cq, w_uq, g_ckv, w_ukv, b_gates, g_mlstm, w_o):
    n = b * s
    col_scale = jnp.ones((EV_COLS,), F32).at[EV_MK:EV_MK + ML_HEADS * ML_QK].set(ML_QK ** -0.5)
    z = _normproj(x, norm, _pack_even_w_in(w_in.astype(BF16)), col_scale, F32, Tiles.proj_col_steps_even)

    cos_t, sin_t = _rope_tables(positions)
    wk, wv_t = _split_w_ukv(w_ukv)
    q, k, vt = _mla_prep(z, b, s, g_cq, g_ckv, _pack_w_uq(w_uq), wk, wv_t, cos_t, sin_t)
    a_out = _mla_attention(q.reshape(b, s, -1), k.reshape(b, s, -1), vt)

    z3 = z.reshape(b, s, EV_COLS)
    gates_t = z3[:, :, EV_MG:EV_MG + 4 * ML_HEADS].reshape(b, s, 4, ML_HEADS).transpose(0, 3, 2, 1)
    gate_bias = b_gates.astype(F32).reshape(4, ML_HEADS).T.reshape(ML_HEADS, 4, 1)
    m_out = _mlstm(z3, gates_t, gate_bias, g_mlstm)

    split = MLA_HEADS * MLA_V
    return _outproj(x, [a_out.reshape(n, -1), m_out.reshape(n, -1)], [w_o[:split], w_o[split:]])


def _odd_mixer(x, b, s, positions, norm, w_in, lam_q1, lam_k1, lam_q2, lam_k2, g_sub, w_o, lam_init):
    n = b * s
    width = DA_HEADS * 2 * DA_HEAD
    zq, vt = _qkv_proj(x, b, s, norm, w_in, LOG2_E * DA_HEAD ** -0.5, DA_HEADS, 2 * DA_HEAD)
    zq = zq.reshape(b, s, 2 * width)
    slopes = jnp.asarray([2.0 ** (-8.0 * (h + 1) / DA_HEADS) for h in range(DA_HEADS)], dtype=F32)
    lam_vec = jnp.stack([lam_q1, lam_k1, lam_q2, lam_k2]).astype(F32)
    pos = positions - positions[:, :1]
    o = _diff_attention(zq, vt, pos.reshape(b, s, 1), pos.reshape(b, 1, s), slopes, lam_vec, g_sub, lam_init)
    return _outproj(x, [o.reshape(n, width)], [w_o])


def kernel(x, positions, l0_ffn1_norm, l0_ffn1_w_gu, l0_ffn1_w_down, l0_mix_norm, l0_w_in, l0_g_cq, l0_w_uq, l0_g_ckv, l0_w_ukv, l0_b_gates, l0_g_mlstm, l0_w_o, l0_ffn2_norm, l0_ffn2_w_gu, l0_ffn2_w_down, l1_ffn1_norm, l1_ffn1_w_gu, l1_ffn1_w_down, l1_mix_norm, l1_w_in, l1_lam_q1, l1_lam_k1, l1_lam_q2, l1_lam_k2, l1_g_sub, l1_w_o, l1_ffn2_norm, l1_ffn2_w_gu, l1_ffn2_w_down, final_norm):
    b, s, d = x.shape
    h = x.reshape(b * s, d)
    h = _ffn(h, l0_ffn1_norm, l0_ffn1_w_gu, l0_ffn1_w_down)
    h = _even_mixer(h, b, s, positions, l0_mix_norm, l0_w_in, l0_g_cq, l0_w_uq, l0_g_ckv, l0_w_ukv, l0_b_gates,
                    l0_g_mlstm, l0_w_o)
    h = _ffn(h, l0_ffn2_norm, l0_ffn2_w_gu, l0_ffn2_w_down)
    h = _ffn(h, l1_ffn1_norm, l1_ffn1_w_gu, l1_ffn1_w_down)
    lam_init = 0.8 - 0.6 * math.exp(-0.3 * 1)
    h = _odd_mixer(h, b, s, positions, l1_mix_norm, l1_w_in, l1_lam_q1, l1_lam_k1, l1_lam_q2, l1_lam_k2, l1_g_sub,
                   l1_w_o, lam_init)
    h = _ffn(h, l1_ffn2_norm, l1_ffn2_w_gu, l1_ffn2_w_down, final_gain=final_norm)
    return h.reshape(b, s, d)
```

```python
import functools
import math

import jax
import jax.numpy as jnp
import numpy as np
from jax import lax
from jax.experimental import pallas as pl
from jax.experimental.pallas import tpu as pltpu

F32 = jnp.float32
BF16 = jnp.bfloat16

D_MODEL = 2048
MLA_HEADS = 8
MLA_Q_RANK = 512
MLA_KV_RANK = 512
MLA_NOPE = 128
MLA_ROPE = 64
MLA_V = 128
ROPE_THETA = 10000.0
ML_HEADS = 4
ML_QK = 128
ML_V = 256
DA_HEADS = 8
DA_HEAD = 128
D_FF = 5632
EPS = 1e-6
NEG_INIT = -1e30
LOG2_E = math.log2(math.e)

LANES = 128
BF16_SUBLANES = 16
MLA_QK_PAD = 256
V7X_VMEM_BYTES = 64 * 1024 * 1024
VMEM_REQUEST_CAP_BYTES = V7X_VMEM_BYTES - 6 * 1024 * 1024

EV_CQ = 0
EV_CKV = EV_CQ + MLA_Q_RANK
EV_MV = EV_CKV + MLA_KV_RANK
EV_MO = EV_MV + ML_HEADS * ML_V
EV_MQ = EV_MO + ML_HEADS * ML_V
EV_MK = EV_MQ + ML_HEADS * ML_QK
EV_KR = EV_MK + ML_HEADS * ML_QK
EV_MG = EV_KR + LANES
EV_COLS = EV_MG + LANES
assert EV_MV % ML_V == 0 and EV_MO % ML_V == 0 and EV_MQ % ML_QK == 0 and EV_MK % ML_QK == 0


class Tiles:
    ffn_rows = 1024
    ffn_cols = 512
    proj_rows = 512
    proj_col_steps_even = 2
    prep_rows = 512
    out_rows = 512
    mla_q = 2048
    mla_k = 512
    attn_q = 1024
    attn_k = 512
    mlstm_chunk = 256
    mlstm_post_rows = 512


def _compiler_params(semantics, vmem_bytes):
    return pltpu.CompilerParams(dimension_semantics=semantics,
                                vmem_limit_bytes=int(min(vmem_bytes, VMEM_REQUEST_CAP_BYTES)))


def _rms(x, gain):
    y = x * lax.rsqrt(jnp.mean(x * x, axis=-1, keepdims=True) + EPS)
    return y * gain


def _ffn_kernel(*refs, n_ff_steps, final):
    if final:
        x_ref, g_ref, wg_ref, wu_ref, wd_ref, gf_ref, o_ref, xn_ref = refs
    else:
        x_ref, g_ref, wg_ref, wu_ref, wd_ref, o_ref, xn_ref = refs
    j = pl.program_id(1)

    def branch(xn):
        gate = jnp.dot(xn, wg_ref[...], preferred_element_type=F32)
        up = jnp.dot(xn, wu_ref[...], preferred_element_type=F32)
        act = (0.5 * (gate * jax.nn.sigmoid(gate)) * up).astype(BF16)
        return jnp.dot(act, wd_ref[...], preferred_element_type=F32)

    @pl.when(j == 0)
    def _():
        x = x_ref[...]
        xn = _rms(x, g_ref[...]).astype(BF16)
        xn_ref[...] = xn
        o_ref[...] = x + branch(xn)

    last = n_ff_steps - 1

    @pl.when((j > 0) & (j < last) if final else j > 0)
    def _():
        o_ref[...] += branch(xn_ref[...])

    if final:
        @pl.when(j == last)
        def _():
            o_ref[...] = _rms(o_ref[...] + branch(xn_ref[...]), gf_ref[...])


def _ffn(x, gain, w_gu, w_down, final_gain=None):
    n, d = x.shape
    tm, tf = Tiles.ffn_rows, Tiles.ffn_cols
    assert n % tm == 0 and D_FF % tf == 0 and D_FF // tf >= 2 and w_gu.shape == (d, 2 * D_FF)
    nf = D_FF // tf
    wgu = w_gu.astype(BF16)
    wd = w_down.astype(BF16)
    final = final_gain is not None
    in_specs = [
        pl.BlockSpec((tm, d), lambda i, j: (i, 0)),
        pl.BlockSpec((1, d), lambda i, j: (0, 0)),
        pl.BlockSpec((d, tf), lambda i, j: (0, j)),
        pl.BlockSpec((d, tf), lambda i, j: (0, j + nf)),
        pl.BlockSpec((tf, d), lambda i, j: (j, 0)),
    ]
    args = [x, gain.reshape(1, d), wgu, wgu, wd]
    if final:
        in_specs.append(pl.BlockSpec((1, d), lambda i, j: (0, 0)))
        args.append(final_gain.reshape(1, d))
    vmem = (2 * 2 * tm * d * 4 + tm * d * 2 + 2 * 3 * d * tf * 2 + 4 * tm * tf * 4 + tm * d * 4)
    return pl.pallas_call(
        functools.partial(_ffn_kernel, n_ff_steps=nf, final=final),
        grid=(n // tm, nf),
        in_specs=in_specs,
        out_specs=pl.BlockSpec((tm, d), lambda i, j: (i, 0)),
        out_shape=jax.ShapeDtypeStruct((n, d), F32),
        scratch_shapes=[pltpu.VMEM((tm, d), BF16)],
        compiler_params=_compiler_params(("parallel", "arbitrary"), vmem),
        name="ffn",
    )(*args)


def _normproj_kernel(x_ref, g_ref, w_ref, cs_ref, o_ref, xn_ref):
    def project(xn):
        acc = jnp.dot(xn, w_ref[...], preferred_element_type=F32)
        o_ref[...] = (acc * cs_ref[...]).astype(o_ref.dtype)

    @pl.when(pl.program_id(1) == 0)
    def _():
        xn = _rms(x_ref[...], g_ref[...]).astype(BF16)
        xn_ref[...] = xn
        project(xn)

    @pl.when(pl.program_id(1) > 0)
    def _():
        project(xn_ref[...])


def _normproj(x, gain, w, col_scale):
    n, d = x.shape
    cols = w.shape[1]
    col_steps = Tiles.proj_col_steps_even
    tm, tn = Tiles.proj_rows, cols // col_steps
    assert n % tm == 0 and tn * col_steps == cols and tn % LANES == 0
    out_dtype = F32
    vmem = 2 * tm * d * 4 + tm * d * 2 + 2 * d * tn * 2 + 2 * tm * tn * 4 + 2 * tm * tn * 4
    return pl.pallas_call(
        _normproj_kernel,
        grid=(n // tm, col_steps),
        in_specs=[
            pl.BlockSpec((tm, d), lambda i, j: (i, 0)),
            pl.BlockSpec((1, d), lambda i, j: (0, 0)),
            pl.BlockSpec((d, tn), lambda i, j: (0, j)),
            pl.BlockSpec((1, tn), lambda i, j: (0, j)),
        ],
        out_specs=pl.BlockSpec((tm, tn), lambda i, j: (i, j)),
        out_shape=jax.ShapeDtypeStruct((n, cols), out_dtype),
        scratch_shapes=[pltpu.VMEM((tm, d), BF16)],
        compiler_params=_compiler_params(("parallel", "arbitrary"), vmem),
        name="normproj",
    )(x, gain.reshape(1, d), w.astype(BF16), col_scale.reshape(1, cols))


def _qkv_proj_kernel(x_ref, g_ref, w_ref, o_ref, vt_ref, xn_ref, *, q_scale, heads, head_width):
    j = pl.program_id(1)

    @pl.when(j == 0)
    def _():
        xn_ref[...] = _rms(x_ref[...], g_ref[...]).astype(BF16)

    def project():
        return jnp.dot(xn_ref[...], w_ref[...], preferred_element_type=F32)

    @pl.when(j == 0)
    def _():
        o_ref[...] = (project() * q_scale).astype(o_ref.dtype)

    @pl.when(j == 1)
    def _():
        o_ref[...] = project().astype(o_ref.dtype)

    @pl.when(j == 2)
    def _():
        acc = project()
        for h in range(heads):
            vt_ref[0, h] = acc[:, h * head_width:(h + 1) * head_width].T.astype(vt_ref.dtype)


def _qkv_proj(x, b, s, gain, w, q_scale, heads, head_width):
    n, d = x.shape
    width = heads * head_width
    tm = Tiles.proj_rows
    assert w.shape == (d, 3 * width) and s % tm == 0 and n == b * s
    tiles_per_seq = s // tm
    vmem = 2 * tm * d * 4 + tm * d * 2 + 2 * d * width * 2 + 4 * tm * width * 2 + 3 * tm * width * 4
    return pl.pallas_call(
        functools.partial(_qkv_proj_kernel, q_scale=q_scale, heads=heads, head_width=head_width),
        grid=(n // tm, 3),
        in_specs=[
            pl.BlockSpec((tm, d), lambda i, j: (i, 0)),
            pl.BlockSpec((1, d), lambda i, j: (0, 0)),
            pl.BlockSpec((d, width), lambda i, j: (0, j)),
        ],
        out_specs=[
            pl.BlockSpec((tm, width), lambda i, j: (i, jnp.minimum(j, 1))),
            pl.BlockSpec((1, heads, head_width, tm), lambda i, j: (i // tiles_per_seq, 0, 0, i % tiles_per_seq)),
        ],
        out_shape=[
            jax.ShapeDtypeStruct((n, 2 * width), BF16),
            jax.ShapeDtypeStruct((b, heads, head_width, s), BF16),
        ],
        scratch_shapes=[pltpu.VMEM((tm, d), BF16)],
        compiler_params=_compiler_params(("parallel", "arbitrary"), vmem),
        name="qkv_proj",
    )(x, gain.reshape(1, d), w.astype(BF16))


def _rope_tile(x, cos_t, sin_t):
    lane = lax.broadcasted_iota(jnp.int32, x.shape, 1)
    half = MLA_ROPE // 2
    partner = jnp.where(lane % MLA_ROPE < half, pltpu.roll(x, LANES - half, 1), pltpu.roll(x, half, 1))
    return x * cos_t + partner * sin_t


def _mla_prep_kernel(cq_ref, ckv_ref, kr_ref, gq_ref, gkv_ref, wq_ref, wk_ref, wvt_ref, cos_ref, sin_ref,
                     q_ref, k_ref, vt_ref):
    cos_t = cos_ref[...]
    sin_t = sin_ref[...]
    cqn = _rms(cq_ref[...], gq_ref[...]).astype(BF16)
    ckvn_f32 = _rms(ckv_ref[...], gkv_ref[...])
    ckvn = ckvn_f32.astype(BF16)
    scale = LOG2_E * (MLA_NOPE + MLA_ROPE) ** -0.5
    q = jnp.dot(cqn, wq_ref[...], preferred_element_type=F32)
    kn = jnp.dot(ckvn, wk_ref[...], preferred_element_type=F32)
    v_t = jnp.dot(wvt_ref[...], ckvn_f32.T.astype(BF16), preferred_element_type=F32)
    ones = jnp.ones((BF16_SUBLANES, v_t.shape[1]), BF16)
    lane = lax.broadcasted_iota(jnp.int32, cos_t.shape, 1)
    k_rope = jnp.where(lane < MLA_ROPE, _rope_tile(kr_ref[...], cos_t, sin_t), 0.0).astype(BF16)
    for h in range(MLA_HEADS):
        base = h * MLA_QK_PAD
        q_ref[:, base:base + MLA_NOPE] = (q[:, base:base + MLA_NOPE] * scale).astype(BF16)
        q_rope = _rope_tile(q[:, base + MLA_NOPE:base + MLA_QK_PAD], cos_t, sin_t)
        q_ref[:, base + MLA_NOPE:base + MLA_QK_PAD] = (q_rope * scale).astype(BF16)
        k_ref[:, base:base + MLA_NOPE] = kn[:, h * MLA_NOPE:(h + 1) * MLA_NOPE].astype(BF16)
        k_ref[:, base + MLA_NOPE:base + MLA_QK_PAD] = k_rope
        vt_ref[0, h, :MLA_V, :] = v_t[h * MLA_V:(h + 1) * MLA_V].astype(BF16)
        vt_ref[0, h, MLA_V:, :] = ones


def _mla_prep(z, b, s, g_cq, g_ckv, wq, wk, wv_t, cos_t, sin_t):
    n = z.shape[0]
    tm = Tiles.prep_rows
    assert s % tm == 0 and n == b * s
    tiles_per_seq = s // tm
    qk_cols = MLA_HEADS * MLA_QK_PAD
    v_cols = MLA_HEADS * MLA_V
    vt_rows = MLA_V + BF16_SUBLANES
    row = lambda i: (i, 0)
    fixed = lambda i: (0, 0)
    vmem = (2 * tm * (MLA_Q_RANK + MLA_KV_RANK + 3 * LANES) * 4 + 2 * MLA_Q_RANK * (qk_cols + 2 * v_cols) * 2
            + 2 * tm * (2 * qk_cols + v_cols) * 2 + 3 * tm * qk_cols * 4)
    return pl.pallas_call(
        _mla_prep_kernel,
        grid=(n // tm,),
        in_specs=[
            pl.BlockSpec((tm, MLA_Q_RANK), lambda i: (i, EV_CQ // MLA_Q_RANK)),
            pl.BlockSpec((tm, MLA_KV_RANK), lambda i: (i, EV_CKV // MLA_KV_RANK)),
            pl.BlockSpec((tm, LANES), lambda i: (i, EV_KR // LANES)),
            pl.BlockSpec((1, MLA_Q_RANK), fixed),
            pl.BlockSpec((1, MLA_KV_RANK), fixed),
            pl.BlockSpec((MLA_Q_RANK, qk_cols), fixed),
            pl.BlockSpec((MLA_KV_RANK, v_cols), fixed),
            pl.BlockSpec((v_cols, MLA_KV_RANK), fixed),
            pl.BlockSpec((tm, LANES), row),
            pl.BlockSpec((tm, LANES), row),
        ],
        out_specs=[
            pl.BlockSpec((tm, qk_cols), row),
            pl.BlockSpec((tm, qk_cols), row),
            pl.BlockSpec((1, MLA_HEADS, vt_rows, tm), lambda i: (i // tiles_per_seq, 0, 0, i % tiles_per_seq)),
        ],
        out_shape=[
            jax.ShapeDtypeStruct((n, qk_cols), BF16),
            jax.ShapeDtypeStruct((n, qk_cols), BF16),
            jax.ShapeDtypeStruct((b, MLA_HEADS, vt_rows, s), BF16),
        ],
        compiler_params=_compiler_params(("parallel",), vmem),
        name="mla_prep",
    )(z, z, z, g_cq.reshape(1, -1), g_ckv.reshape(1, -1), wq, wk, wv_t, cos_t, sin_t)


def _mla_attn_kernel(q_ref, k_ref, vt_ref, o_ref, *, n_chunks, tk):
    q = q_ref[0]
    tq = q.shape[0]
    rows = vt_ref.shape[2]

    def scores(c):
        k = k_ref[0, c * tk:(c + 1) * tk, :]
        return lax.dot_general(k, q, (((1,), (1,)), ((), ())), preferred_element_type=F32)

    m = jnp.full((1, tq), NEG_INIT, F32)
    acc = jnp.zeros((rows, tq), F32)
    s = scores(0)
    for c in range(n_chunks):
        s_next = scores(c + 1) if c + 1 < n_chunks else None
        m_new = jnp.maximum(m, jnp.max(s, axis=0, keepdims=True))
        alpha = jnp.exp2(m - m_new)
        p = jnp.exp2(s - m_new).astype(BF16)
        acc = alpha * acc + jnp.dot(vt_ref[0, 0, :, c * tk:(c + 1) * tk], p, preferred_element_type=F32)
        m, s = m_new, s_next
    o_t = acc[:MLA_V] * (1.0 / acc[MLA_V:MLA_V + 1])
    o_ref[0] = o_t.T.astype(o_ref.dtype)


def _mla_attention(q, k, vt):
    b, s, _ = q.shape
    tq, tk = Tiles.mla_q, Tiles.mla_k
    rows = vt.shape[2]
    assert s % tq == 0 and s % tk == 0
    vmem = (2 * tq * MLA_QK_PAD * 2 + 2 * s * (MLA_QK_PAD + rows) * 2 + 2 * tq * MLA_V * 2
            + 6 * tq * tk * 4)
    return pl.pallas_call(
        functools.partial(_mla_attn_kernel, n_chunks=s // tk, tk=tk),
        grid=(b, MLA_HEADS, s // tq),
        in_specs=[
            pl.BlockSpec((1, tq, MLA_QK_PAD), lambda bi, h, i: (bi, i, h)),
            pl.BlockSpec((1, s, MLA_QK_PAD), lambda bi, h, i: (bi, 0, h)),
            pl.BlockSpec((1, 1, rows, s), lambda bi, h, i: (bi, h, 0, 0)),
        ],
        out_specs=pl.BlockSpec((1, tq, MLA_V), lambda bi, h, i: (bi, i, h)),
        out_shape=jax.ShapeDtypeStruct((b, s, MLA_HEADS * MLA_V), BF16),
        compiler_params=_compiler_params(("parallel", "parallel", "arbitrary"), vmem),
        name="mla_attn",
    )(q, k, vt)


def _log_sigmoid(x):
    return jnp.minimum(x, 0.0) - jnp.log(1.0 + jnp.exp(-jnp.abs(x)))


def _mlstm_masks(chunk):
    t_idx = lax.broadcasted_iota(jnp.int32, (chunk, chunk), 0)
    s_idx = lax.broadcasted_iota(jnp.int32, (chunk, chunk), 1)
    lower, upper = s_idx <= t_idx, s_idx >= t_idx
    one_hot = lambda m: jnp.where(m, 1.0, 0.0).astype(F32)
    blocked = lambda m: jnp.where(m, 0.0, NEG_INIT).astype(F32)
    return {"eye": one_hot(s_idx == t_idx), "lower": one_hot(lower), "upper": one_hot(upper),
            "neg_lower": blocked(lower), "neg_upper": blocked(upper)}


def _mlstm_chunk(q_ref, k_ref, v_ref, gates, masks, c_ref, n_row, m_st, start, chunk, li_idx, lf_idx, reverse):
    L = chunk
    q = q_ref[0, pl.ds(start, L), :]
    k = k_ref[0, pl.ds(start, L), :]
    v = v_ref[0, pl.ds(start, L), :].astype(BF16)
    li_row = gates[li_idx:li_idx + 1, :]
    lf_row = _log_sigmoid(gates[lf_idx:lf_idx + 1, :])
    li_col = jnp.sum(masks["eye"] * li_row, axis=1, keepdims=True)
    lf_col = jnp.sum(masks["eye"] * lf_row, axis=1, keepdims=True)
    visible, visible_t, blocked = ((masks["upper"], masks["lower"], masks["neg_upper"]) if reverse
                                   else (masks["lower"], masks["upper"], masks["neg_lower"]))
    a_col = jnp.sum(visible * lf_row, axis=1, keepdims=True)
    a_row = jnp.sum(visible_t * lf_col, axis=0, keepdims=True)
    g = jnp.sum(lf_row, axis=1, keepdims=True)

    dmat = (a_col - a_row) + li_row + blocked
    inter = a_col + m_st
    m = jnp.maximum(inter, jnp.max(dmat, axis=1, keepdims=True))
    w_inter = jnp.exp(inter - m)
    q16 = q.astype(BF16)
    qk = (lax.dot_general(q16, k.astype(BF16), (((1,), (1,)), ((), ())), preferred_element_type=F32)
          * jnp.exp(dmat - m))
    num = (w_inter * jnp.dot(q16, c_ref[...].astype(BF16), preferred_element_type=F32)
           + jnp.dot(qk.astype(BF16), v, preferred_element_type=F32))
    den = w_inter * jnp.sum(q * n_row, axis=1, keepdims=True) + jnp.sum(qk, axis=1, keepdims=True)
    h_out = num / jnp.maximum(jnp.abs(den), jnp.exp(-m))

    r_col = g - a_col + li_col
    m_new = jnp.maximum(g + m_st, jnp.max(r_col, axis=0, keepdims=True))
    w_old = jnp.exp(g + m_st - m_new)
    kw = k * jnp.exp(r_col - m_new)
    c_ref[...] = w_old * c_ref[...] + jnp.dot(kw.T.astype(BF16), v, preferred_element_type=F32)
    n_new = w_old * n_row + jnp.sum(kw, axis=0, keepdims=True)
    return h_out, n_new, m_new


def _mlstm_kernel(q_ref, k_ref, v_ref, gate_ref, bias_ref, mo_ref, gn_ref, o_ref,
                  hf_ref, hb_ref, cf_ref, cb_ref, *, seq, chunk, post_rows):
    nc = seq // chunk
    cf_ref[...] = jnp.zeros_like(cf_ref)
    cb_ref[...] = jnp.zeros_like(cb_ref)
    bias = bias_ref[0]
    masks = _mlstm_masks(chunk)

    def body(i, carry):
        n_f, m_f, n_b, m_b = carry
        sf = pl.multiple_of(i * chunk, chunk)
        sb = pl.multiple_of((nc - 1 - i) * chunk, chunk)
        h_f, n_f, m_f = _mlstm_chunk(q_ref, k_ref, v_ref, gate_ref[0, 0, :, pl.ds(sf, chunk)] + bias, masks,
                                     cf_ref, n_f, m_f, sf, chunk, 0, 1, False)
        hf_ref[pl.ds(sf, chunk), :] = h_f
        h_b, n_b, m_b = _mlstm_chunk(q_ref, k_ref, v_ref, gate_ref[0, 0, :, pl.ds(sb, chunk)] + bias, masks,
                                     cb_ref, n_b, m_b, sb, chunk, 2, 3, True)
        hb_ref[pl.ds(sb, chunk), :] = h_b
        return n_f, m_f, n_b, m_b

    n0 = jnp.zeros((1, ML_QK), F32)
    m0 = jnp.full((1, 1), NEG_INIT, F32)
    lax.fori_loop(0, nc, body, (n0, m0, n0, m0), unroll=2)

    gn = gn_ref[...]

    def post(i, carry):
        r = pl.multiple_of(i * post_rows, post_rows)
        hm = hf_ref[pl.ds(r, post_rows), :] + hb_ref[pl.ds(r, post_rows), :]
        y = _rms(hm, gn)
        o_ref[0, pl.ds(r, post_rows), :] = (y * jax.nn.sigmoid(mo_ref[0, pl.ds(r, post_rows), :])).astype(o_ref.dtype)
        return carry

    lax.fori_loop(0, seq // post_rows, post, 0)


def _mlstm(z3, gates_t, gate_bias, g_mlstm):
    b, s, _ = z3.shape
    chunk = Tiles.mlstm_chunk
    assert s % (2 * chunk) == 0 and s % Tiles.mlstm_post_rows == 0
    vmem = (2 * s * (2 * ML_QK + 2 * ML_V + LANES) * 4 + 2 * s * ML_V * 2 + 2 * s * ML_V * 4
            + 2 * ML_QK * ML_V * 4 + 24 * chunk * chunk * 4 + (2 << 20))
    return pl.pallas_call(
        functools.partial(_mlstm_kernel, seq=s, chunk=chunk, post_rows=Tiles.mlstm_post_rows),
        grid=(b, ML_HEADS),
        in_specs=[
            pl.BlockSpec((1, s, ML_QK), lambda bi, h: (bi, 0, EV_MQ // ML_QK + h)),
            pl.BlockSpec((1, s, ML_QK), lambda bi, h: (bi, 0, EV_MK // ML_QK + h)),
            pl.BlockSpec((1, s, ML_V), lambda bi, h: (bi, 0, EV_MV // ML_V + h)),
            pl.BlockSpec((1, 1, 4, s), lambda bi, h: (bi, h, 0, 0)),
            pl.BlockSpec((1, 4, 1), lambda bi, h: (h, 0, 0)),
            pl.BlockSpec((1, s, ML_V), lambda bi, h: (bi, 0, EV_MO // ML_V + h)),
            pl.BlockSpec((1, ML_V), lambda bi, h: (0, h)),
        ],
        out_specs=pl.BlockSpec((1, s, ML_V), lambda bi, h: (bi, 0, h)),
        out_shape=jax.ShapeDtypeStruct((b, s, ML_HEADS * ML_V), BF16),
        scratch_shapes=[
            pltpu.VMEM((s, ML_V), F32),
            pltpu.VMEM((s, ML_V), F32),
            pltpu.VMEM((ML_QK, ML_V), F32),
            pltpu.VMEM((ML_QK, ML_V), F32),
        ],
        compiler_params=_compiler_params(("parallel", "parallel"), vmem),
        name="mlstm",
    )(z3, z3, z3, gates_t, gate_bias, z3, g_mlstm.reshape(1, -1))


def _outproj_kernel(*refs, n_lhs):
    x_ref = refs[0]
    a_refs = refs[1:1 + n_lhs]
    w_refs = refs[1 + n_lhs:1 + 2 * n_lhs]
    o_ref = refs[1 + 2 * n_lhs]
    acc = x_ref[...]
    for a_ref, w_ref in zip(a_refs, w_refs):
        acc = acc + jnp.dot(a_ref[...], w_ref[...], preferred_element_type=F32)
    o_ref[...] = acc


def _outproj(x, lhs, weights):
    n, d = x.shape
    tm = Tiles.out_rows
    assert n % tm == 0
    row = lambda i: (i, 0)
    fixed = lambda i: (0, 0)
    k_total = sum(a.shape[1] for a in lhs)
    vmem = 2 * 2 * tm * d * 4 + 2 * tm * k_total * 2 + 2 * k_total * d * 2 + tm * d * 4
    return pl.pallas_call(
        functools.partial(_outproj_kernel, n_lhs=len(lhs)),
        grid=(n // tm,),
        in_specs=([pl.BlockSpec((tm, d), row)]
                  + [pl.BlockSpec((tm, a.shape[1]), row) for a in lhs]
                  + [pl.BlockSpec(w.shape, fixed) for w in weights]),
        out_specs=pl.BlockSpec((tm, d), row),
        out_shape=jax.ShapeDtypeStruct((n, d), F32),
        compiler_params=_compiler_params(("parallel",), vmem),
        name="outproj",
    )(x, *lhs, *[w.astype(BF16) for w in weights])


def _diff_attn_kernel(slope_ref, q1_ref, q2_ref, k1_ref, k2_ref, vt_ref, pq_ref, pk_ref, lam_ref, gs_ref, o_ref,
                      vta_ref, *, n_chunks, tk, lam_init):
    h = pl.program_id(1)
    slope = slope_ref[h] * LOG2_E
    q1 = q1_ref[0]
    q2 = q2_ref[0]
    tq = q1.shape[0]
    dv = 2 * DA_HEAD
    rows = vta_ref.shape[0]

    @pl.when(pl.program_id(2) == 0)
    def _():
        vta_ref[:dv, :] = vt_ref[0, 0]
        vta_ref[dv:, :] = jnp.ones((rows - dv, vta_ref.shape[1]), BF16)
    pos_q = slope * pq_ref[0].astype(F32)
    nt = (((1,), (1,)), ((), ()))

    def scores(c):
        rows_c = slice(c * tk, (c + 1) * tk)
        dist = jnp.abs(slope * pk_ref[0, rows_c, :].astype(F32) - pos_q)
        s1 = lax.dot_general(k1_ref[0, rows_c, :], q1, nt, preferred_element_type=F32) - dist
        s2 = lax.dot_general(k2_ref[0, rows_c, :], q2, nt, preferred_element_type=F32) - dist
        return s1, s2

    def softmax_step(s, vt, m, acc):
        m_new = jnp.maximum(m, jnp.max(s, axis=0, keepdims=True))
        alpha = jnp.exp2(m - m_new)
        p = jnp.exp2(s - m_new).astype(BF16)
        return m_new, alpha * acc + jnp.dot(vt, p, preferred_element_type=F32)

    m1 = m2 = jnp.full((1, tq), NEG_INIT, F32)
    acc1 = acc2 = jnp.zeros((rows, tq), F32)
    s1, s2 = scores(0)
    for c in range(n_chunks):
        nxt = scores(c + 1) if c + 1 < n_chunks else (None, None)
        vt = vta_ref[:, c * tk:(c + 1) * tk]
        m1, acc1 = softmax_step(s1, vt, m1, acc1)
        m2, acc2 = softmax_step(s2, vt, m2, acc2)
        s1, s2 = nxt

    lam_vec = lam_ref[...]
    lam = (jnp.exp(jnp.sum(lam_vec[0:1] * lam_vec[1:2], axis=1, keepdims=True))
           - jnp.exp(jnp.sum(lam_vec[2:3] * lam_vec[3:4], axis=1, keepdims=True)) + lam_init)
    w1 = 1.0 / acc1[dv:dv + 1]
    w2 = lam / acc2[dv:dv + 1]
    o_t = acc1[:dv] * w1 - acc2[:dv] * w2
    o_ref[0] = (_rms(o_t.T, gs_ref[...]) * (1.0 - lam_init)).astype(o_ref.dtype)


def _diff_attention(zq, vt, pos_col, pos_row, slopes, lam_vec, g_sub, lam_init):
    b, s, _ = zq.shape
    tq, tk = Tiles.attn_q, Tiles.attn_k
    assert s % tq == 0 and s % tk == 0
    dv = 2 * DA_HEAD
    rows = dv + BF16_SUBLANES
    k_base = DA_HEADS * 2
    vmem = (2 * 2 * tq * DA_HEAD * 2 + 3 * s * (2 * DA_HEAD + rows) * 2 + 2 * s * LANES * 4 + 2 * tq * dv * 2
            + 14 * tq * tk * 4)
    return pl.pallas_call(
        functools.partial(_diff_attn_kernel, n_chunks=s // tk, tk=tk, lam_init=lam_init),
        grid=(b, DA_HEADS, s // tq),
        in_specs=[
            pl.BlockSpec(memory_space=pltpu.SMEM),
            pl.BlockSpec((1, tq, DA_HEAD), lambda bi, h, i: (bi, i, 2 * h)),
            pl.BlockSpec((1, tq, DA_HEAD), lambda bi, h, i: (bi, i, 2 * h + 1)),
            pl.BlockSpec((1, s, DA_HEAD), lambda bi, h, i: (bi, 0, k_base + 2 * h)),
            pl.BlockSpec((1, s, DA_HEAD), lambda bi, h, i: (bi, 0, k_base + 2 * h + 1)),
            pl.BlockSpec((1, 1, dv, s), lambda bi, h, i: (bi, h, 0, 0)),
            pl.BlockSpec((1, 1, tq), lambda bi, h, i: (bi, 0, i)),
            pl.BlockSpec((1, s, 1), lambda bi, h, i: (bi, 0, 0)),
            pl.BlockSpec((4, DA_HEAD), lambda bi, h, i: (0, 0)),
            pl.BlockSpec((1, dv), lambda bi, h, i: (0, 0)),
        ],
        out_specs=pl.BlockSpec((1, tq, dv), lambda bi, h, i: (bi, i, h)),
        out_shape=jax.ShapeDtypeStruct((b, s, DA_HEADS * dv), BF16),
        scratch_shapes=[pltpu.VMEM((rows, s), BF16)],
        compiler_params=_compiler_params(("parallel", "parallel", "arbitrary"), vmem),
        name="diff_attn",
    )(slopes, zq, zq, zq, zq, vt, pos_row, pos_col, lam_vec, g_sub.reshape(1, dv))


def _pack_even_w_in(w_in):
    d = w_in.shape[0]
    sizes = (MLA_Q_RANK, MLA_KV_RANK, MLA_ROPE, ML_HEADS * ML_QK, ML_HEADS * ML_QK, ML_HEADS * ML_V,
             ML_HEADS * ML_V, 4 * ML_HEADS)
    c_q, c_kv, k_r, m_q, m_k, m_v, m_o, m_g = jnp.split(w_in, np.cumsum(sizes)[:-1].tolist(), axis=1)
    pad = lambda w, width: jnp.concatenate([w, jnp.zeros((d, width - w.shape[1]), w.dtype)], axis=1)
    packed = jnp.concatenate([c_q, c_kv, m_v, m_o, m_q, m_k, pad(k_r, LANES), pad(m_g, LANES)], axis=1)
    assert packed.shape[1] == EV_COLS
    return packed


def _pack_w_uq(w_uq):
    r = w_uq.shape[0]
    w = w_uq.reshape(r, MLA_HEADS, MLA_NOPE + MLA_ROPE)
    w = jnp.concatenate([w, jnp.zeros((r, MLA_HEADS, MLA_QK_PAD - MLA_NOPE - MLA_ROPE), w.dtype)], axis=2)
    return w.reshape(r, MLA_HEADS * MLA_QK_PAD).astype(BF16)


def _split_w_ukv(w_ukv):
    r = w_ukv.shape[0]
    w = w_ukv.reshape(r, MLA_HEADS, MLA_NOPE + MLA_V)
    wk = w[:, :, :MLA_NOPE].reshape(r, MLA_HEADS * MLA_NOPE)
    wv_t = w[:, :, MLA_NOPE:].reshape(r, MLA_HEADS * MLA_V).T
    return wk.astype(BF16), wv_t.astype(BF16)


def _rope_tables(positions):
    inv_freq = ROPE_THETA ** (-jnp.arange(0, MLA_ROPE, 2, dtype=F32) / MLA_ROPE)
    ang = positions.astype(F32).reshape(-1, 1) * inv_freq
    cos, sin = jnp.cos(ang), jnp.sin(ang)
    zeros = jnp.zeros((ang.shape[0], LANES - MLA_ROPE), F32)
    return jnp.concatenate([cos, cos, zeros], axis=1), jnp.concatenate([-sin, sin, zeros], axis=1)


def _even_mixer(x, b, s, positions, norm, w_in, g_cq, w_uq, g_ckv, w_ukv, b_gates, g_mlstm, w_o):
    n = b * s
    col_scale = jnp.ones((EV_COLS,), F32).at[EV_MK:EV_MK + ML_HEADS * ML_QK].set(ML_QK ** -0.5)
    z = _normproj(x, norm, _pack_even_w_in(w_in.astype(BF16)), col_scale)

    cos_t, sin_t = _rope_tables(positions)
    wk, wv_t = _split_w_ukv(w_ukv)
    q, k, vt = _mla_prep(z, b, s, g_cq, g_ckv, _pack_w_uq(w_uq), wk, wv_t, cos_t, sin_t)
    a_out = _mla_attention(q.reshape(b, s, -1), k.reshape(b, s, -1), vt)

    z3 = z.reshape(b, s, EV_COLS)
    gates_t = z3[:, :, EV_MG:EV_MG + 4 * ML_HEADS].reshape(b, s, 4, ML_HEADS).transpose(0, 3, 2, 1)
    gate_bias = b_gates.astype(F32).reshape(4, ML_HEADS).T.reshape(ML_HEADS, 4, 1)
    m_out = _mlstm(z3, gates_t, gate_bias, g_mlstm)

    split = MLA_HEADS * MLA_V
    return _outproj(x, [a_out.reshape(n, -1), m_out.reshape(n, -1)], [w_o[:split], w_o[split:]])


def _odd_mixer(x, b, s, positions, norm, w_in, lam_q1, lam_k1, lam_q2, lam_k2, g_sub, w_o, lam_init):
    n = b * s
    width = DA_HEADS * 2 * DA_HEAD
    zq, vt = _qkv_proj(x, b, s, norm, w_in, LOG2_E * DA_HEAD ** -0.5, DA_HEADS, 2 * DA_HEAD)
    zq = zq.reshape(b, s, 2 * width)
    slopes = jnp.asarray([2.0 ** (-8.0 * (h + 1) / DA_HEADS) for h in range(DA_HEADS)], dtype=F32)
    lam_vec = jnp.stack([lam_q1, lam_k1, lam_q2, lam_k2]).astype(F32)
    pos = positions - positions[:, :1]
    o = _diff_attention(zq, vt, pos.reshape(b, s, 1), pos.reshape(b, 1, s), slopes, lam_vec, g_sub, lam_init)
    return _outproj(x, [o.reshape(n, width)], [w_o])


def kernel(x, positions, l0_ffn1_norm, l0_ffn1_w_gu, l0_ffn1_w_down, l0_mix_norm, l0_w_in, l0_g_cq, l0_w_uq, l0_g_ckv, l0_w_ukv, l0_b_gates, l0_g_mlstm, l0_w_o, l0_ffn2_norm, l0_ffn2_w_gu, l0_ffn2_w_down, l1_ffn1_norm, l1_ffn1_w_gu, l1_ffn1_w_down, l1_mix_norm, l1_w_in, l1_lam_q1, l1_lam_k1, l1_lam_q2, l1_lam_k2, l1_g_sub, l1_w_o, l1_ffn2_norm, l1_ffn2_w_gu, l1_ffn2_w_down, final_norm):
    b, s, d = x.shape
    h = x.reshape(b * s, d)
    h = _ffn(h, l0_ffn1_norm, l0_ffn1_w_gu, l0_ffn1_w_down)
    h = _even_mixer(h, b, s, positions, l0_mix_norm, l0_w_in, l0_g_cq, l0_w_uq, l0_g_ckv, l0_w_ukv, l0_b_gates,
                    l0_g_mlstm, l0_w_o)
    h = _ffn(h, l0_ffn2_norm, l0_ffn2_w_gu, l0_ffn2_w_down)
    h = _ffn(h, l1_ffn1_norm, l1_ffn1_w_gu, l1_ffn1_w_down)
    lam_init = 0.8 - 0.6 * math.exp(-0.3 * 1)
    h = _odd_mixer(h, b, s, positions, l1_mix_norm, l1_w_in, l1_lam_q1, l1_lam_k1, l1_lam_q2, l1_lam_k2, l1_g_sub,
                   l1_w_o, lam_init)
    h = _ffn(h, l1_ffn2_norm, l1_ffn2_w_gu, l1_ffn2_w_down, final_gain=final_norm)
    return h.reshape(b, s, d)
```

```python
import functools
import math

import jax
import jax.numpy as jnp
import numpy as np
from jax import lax
from jax.experimental import pallas as pl
from jax.experimental.pallas import tpu as pltpu

F32 = jnp.float32
BF16 = jnp.bfloat16

D_MODEL = 2048
MLA_HEADS = 8
MLA_Q_RANK = 512
MLA_KV_RANK = 512
MLA_NOPE = 128
MLA_ROPE = 64
MLA_V = 128
ROPE_THETA = 10000.0
ML_HEADS = 4
ML_QK = 128
ML_V = 256
DA_HEADS = 8
DA_HEAD = 128
D_FF = 5632
EPS = 1e-6
NEG_INIT = -1e30
LOG2_E = math.log2(math.e)

LANES = 128
BF16_SUBLANES = 16
MLA_QK_PAD = 256
V7X_VMEM_BYTES = 64 * 1024 * 1024
VMEM_REQUEST_CAP_BYTES = V7X_VMEM_BYTES - 6 * 1024 * 1024

EV_CQ = 0
EV_CKV = EV_CQ + MLA_Q_RANK
EV_MV = EV_CKV + MLA_KV_RANK
EV_MO = EV_MV + ML_HEADS * ML_V
EV_MQ = EV_MO + ML_HEADS * ML_V
EV_MK = EV_MQ + ML_HEADS * ML_QK
EV_KR = EV_MK + ML_HEADS * ML_QK
EV_MG = EV_KR + LANES
EV_COLS = EV_MG + LANES
assert EV_MV % ML_V == 0 and EV_MO % ML_V == 0 and EV_MQ % ML_QK == 0 and EV_MK % ML_QK == 0


class Tiles:
    ffn_rows = 1024
    ffn_cols = 512
    proj_rows = 512
    proj_col_steps_even = 2
    prep_rows = 512
    out_rows = 512
    mla_q = 2048
    mla_k = 512
    attn_q = 1024
    attn_k = 512
    mlstm_chunk = 256
    mlstm_post_rows = 512


def _compiler_params(semantics, vmem_bytes):
    return pltpu.CompilerParams(dimension_semantics=semantics,
                                vmem_limit_bytes=int(min(vmem_bytes, VMEM_REQUEST_CAP_BYTES)))


def _rms(x, gain):
    y = x * lax.rsqrt(jnp.mean(x * x, axis=-1, keepdims=True) + EPS)
    return y * gain


def _ffn_kernel(*refs, n_ff_steps, final):
    if final:
        x_ref, g_ref, wg_ref, wu_ref, wd_ref, gf_ref, o_ref, xn_ref = refs
    else:
        x_ref, g_ref, wg_ref, wu_ref, wd_ref, o_ref, xn_ref = refs
    j = pl.program_id(1)

    def branch(xn):
        gate = jnp.dot(xn, wg_ref[...], preferred_element_type=F32)
        up = jnp.dot(xn, wu_ref[...], preferred_element_type=F32)
        act = (0.5 * (gate * jax.nn.sigmoid(gate)) * up).astype(BF16)
        return jnp.dot(act, wd_ref[...], preferred_element_type=F32)

    @pl.when(j == 0)
    def _():
        x = x_ref[...]
        xn = _rms(x, g_ref[...]).astype(BF16)
        xn_ref[...] = xn
        o_ref[...] = x + branch(xn)

    last = n_ff_steps - 1

    @pl.when((j > 0) & (j < last) if final else j > 0)
    def _():
        o_ref[...] += branch(xn_ref[...])

    if final:
        @pl.when(j == last)
        def _():
            o_ref[...] = _rms(o_ref[...] + branch(xn_ref[...]), gf_ref[...])


def _ffn(x, gain, w_gu, w_down, final_gain=None):
    n, d = x.shape
    tm, tf = Tiles.ffn_rows, Tiles.ffn_cols
    assert n % tm == 0 and D_FF % tf == 0 and D_FF // tf >= 2 and w_gu.shape == (d, 2 * D_FF)
    nf = D_FF // tf
    wgu = w_gu.astype(BF16)
    wd = w_down.astype(BF16)
    final = final_gain is not None
    in_specs = [
        pl.BlockSpec((tm, d), lambda i, j: (i, 0)),
        pl.BlockSpec((1, d), lambda i, j: (0, 0)),
        pl.BlockSpec((d, tf), lambda i, j: (0, j)),
        pl.BlockSpec((d, tf), lambda i, j: (0, j + nf)),
        pl.BlockSpec((tf, d), lambda i, j: (j, 0)),
    ]
    args = [x, gain.reshape(1, d), wgu, wgu, wd]
    if final:
        in_specs.append(pl.BlockSpec((1, d), lambda i, j: (0, 0)))
        args.append(final_gain.reshape(1, d))
    vmem = (2 * 2 * tm * d * 4 + tm * d * 2 + 2 * 3 * d * tf * 2 + 4 * tm * tf * 4 + tm * d * 4)
    return pl.pallas_call(
        functools.partial(_ffn_kernel, n_ff_steps=nf, final=final),
        grid=(n // tm, nf),
        in_specs=in_specs,
        out_specs=pl.BlockSpec((tm, d), lambda i, j: (i, 0)),
        out_shape=jax.ShapeDtypeStruct((n, d), F32),
        scratch_shapes=[pltpu.VMEM((tm, d), BF16)],
        compiler_params=_compiler_params(("parallel", "arbitrary"), vmem),
        name="ffn",
    )(*args)


def _normproj_kernel(x_ref, g_ref, w_ref, cs_ref, o_ref, xn_ref):
    tn = o_ref.shape[1]

    def project(xn):
        col = pl.multiple_of(pl.program_id(1) * tn, LANES)
        acc = jnp.dot(xn, w_ref[:, pl.ds(col, tn)], preferred_element_type=F32)
        o_ref[...] = (acc * cs_ref[...]).astype(o_ref.dtype)

    @pl.when(pl.program_id(1) == 0)
    def _():
        xn = _rms(x_ref[...], g_ref[...]).astype(BF16)
        xn_ref[...] = xn
        project(xn)

    @pl.when(pl.program_id(1) > 0)
    def _():
        project(xn_ref[...])


def _normproj(x, gain, w, col_scale):
    n, d = x.shape
    cols = w.shape[1]
    col_steps = Tiles.proj_col_steps_even
    tm, tn = Tiles.proj_rows, cols // col_steps
    assert n % tm == 0 and tn * col_steps == cols and tn % LANES == 0
    out_dtype = F32
    vmem = 2 * tm * d * 4 + tm * d * 2 + d * cols * 2 + 2 * tm * tn * 4 + 2 * tm * tn * 4 + (4 << 20)
    return pl.pallas_call(
        _normproj_kernel,
        grid=(n // tm, col_steps),
        in_specs=[
            pl.BlockSpec((tm, d), lambda i, j: (i, 0)),
            pl.BlockSpec((1, d), lambda i, j: (0, 0)),
            pl.BlockSpec((d, cols), lambda i, j: (0, 0), pipeline_mode=pl.Buffered(1)),
            pl.BlockSpec((1, tn), lambda i, j: (0, j)),
        ],
        out_specs=pl.BlockSpec((tm, tn), lambda i, j: (i, j)),
        out_shape=jax.ShapeDtypeStruct((n, cols), out_dtype),
        scratch_shapes=[pltpu.VMEM((tm, d), BF16)],
        compiler_params=_compiler_params(("parallel", "arbitrary"), vmem),
        name="normproj",
    )(x, gain.reshape(1, d), w.astype(BF16), col_scale.reshape(1, cols))


def _qkv_proj_kernel(x_ref, g_ref, w_ref, o_ref, vt_ref, xn_ref, *, q_scale, heads, head_width):
    j = pl.program_id(1)

    @pl.when(j == 0)
    def _():
        xn_ref[...] = _rms(x_ref[...], g_ref[...]).astype(BF16)

    width = heads * head_width

    def project(part):
        return jnp.dot(xn_ref[...], w_ref[:, part * width:(part + 1) * width], preferred_element_type=F32)

    @pl.when(j == 0)
    def _():
        o_ref[...] = (project(0) * q_scale).astype(o_ref.dtype)

    @pl.when(j == 1)
    def _():
        o_ref[...] = project(1).astype(o_ref.dtype)

    @pl.when(j == 2)
    def _():
        acc = project(2)
        for h in range(heads):
            vt_ref[0, h] = acc[:, h * head_width:(h + 1) * head_width].T.astype(vt_ref.dtype)


def _qkv_proj(x, b, s, gain, w, q_scale, heads, head_width):
    n, d = x.shape
    width = heads * head_width
    tm = Tiles.proj_rows
    assert w.shape == (d, 3 * width) and s % tm == 0 and n == b * s
    tiles_per_seq = s // tm
    vmem = 2 * tm * d * 4 + tm * d * 2 + 3 * d * width * 2 + 4 * tm * width * 2 + 3 * tm * width * 4
    return pl.pallas_call(
        functools.partial(_qkv_proj_kernel, q_scale=q_scale, heads=heads, head_width=head_width),
        grid=(n // tm, 3),
        in_specs=[
            pl.BlockSpec((tm, d), lambda i, j: (i, 0)),
            pl.BlockSpec((1, d), lambda i, j: (0, 0)),
            pl.BlockSpec((d, 3 * width), lambda i, j: (0, 0), pipeline_mode=pl.Buffered(1)),
        ],
        out_specs=[
            pl.BlockSpec((tm, width), lambda i, j: (i, jnp.minimum(j, 1))),
            pl.BlockSpec((1, heads, head_width, tm), lambda i, j: (i // tiles_per_seq, 0, 0, i % tiles_per_seq)),
        ],
        out_shape=[
            jax.ShapeDtypeStruct((n, 2 * width), BF16),
            jax.ShapeDtypeStruct((b, heads, head_width, s), BF16),
        ],
        scratch_shapes=[pltpu.VMEM((tm, d), BF16)],
        compiler_params=_compiler_params(("parallel", "arbitrary"), vmem),
        name="qkv_proj",
    )(x, gain.reshape(1, d), w.astype(BF16))


def _rope_tile(x, cos_t, sin_t):
    lane = lax.broadcasted_iota(jnp.int32, x.shape, 1)
    half = MLA_ROPE // 2
    partner = jnp.where(lane % MLA_ROPE < half, pltpu.roll(x, LANES - half, 1), pltpu.roll(x, half, 1))
    return x * cos_t + partner * sin_t


def _mla_prep_kernel(cq_ref, ckv_ref, kr_ref, gq_ref, gkv_ref, wq_ref, wk_ref, wvt_ref, cos_ref, sin_ref,
                     q_ref, k_ref, vt_ref):
    cos_t = cos_ref[...]
    sin_t = sin_ref[...]
    cqn = _rms(cq_ref[...], gq_ref[...]).astype(BF16)
    ckvn_f32 = _rms(ckv_ref[...], gkv_ref[...])
    ckvn = ckvn_f32.astype(BF16)
    scale = LOG2_E * (MLA_NOPE + MLA_ROPE) ** -0.5
    q = jnp.dot(cqn, wq_ref[...], preferred_element_type=F32)
    kn = jnp.dot(ckvn, wk_ref[...], preferred_element_type=F32)
    v_t = jnp.dot(wvt_ref[...], ckvn_f32.T.astype(BF16), preferred_element_type=F32)
    ones = jnp.ones((BF16_SUBLANES, v_t.shape[1]), BF16)
    lane = lax.broadcasted_iota(jnp.int32, cos_t.shape, 1)
    k_rope = jnp.where(lane < MLA_ROPE, _rope_tile(kr_ref[...], cos_t, sin_t), 0.0).astype(BF16)
    for h in range(MLA_HEADS):
        base = h * MLA_QK_PAD
        q_ref[:, base:base + MLA_NOPE] = (q[:, base:base + MLA_NOPE] * scale).astype(BF16)
        q_rope = _rope_tile(q[:, base + MLA_NOPE:base + MLA_QK_PAD], cos_t, sin_t)
        q_ref[:, base + MLA_NOPE:base + MLA_QK_PAD] = (q_rope * scale).astype(BF16)
        k_ref[:, base:base + MLA_NOPE] = kn[:, h * MLA_NOPE:(h + 1) * MLA_NOPE].astype(BF16)
        k_ref[:, base + MLA_NOPE:base + MLA_QK_PAD] = k_rope
        vt_ref[0, h, :MLA_V, :] = v_t[h * MLA_V:(h + 1) * MLA_V].astype(BF16)
        vt_ref[0, h, MLA_V:, :] = ones


def _mla_prep(z, b, s, g_cq, g_ckv, wq, wk, wv_t, cos_t, sin_t):
    n = z.shape[0]
    tm = Tiles.prep_rows
    assert s % tm == 0 and n == b * s
    tiles_per_seq = s // tm
    qk_cols = MLA_HEADS * MLA_QK_PAD
    v_cols = MLA_HEADS * MLA_V
    vt_rows = MLA_V + BF16_SUBLANES
    row = lambda i: (i, 0)
    fixed = lambda i: (0, 0)
    vmem = (2 * tm * (MLA_Q_RANK + MLA_KV_RANK + 3 * LANES) * 4 + 2 * MLA_Q_RANK * (qk_cols + 2 * v_cols) * 2
            + 2 * tm * (2 * qk_cols + v_cols) * 2 + 3 * tm * qk_cols * 4)
    return pl.pallas_call(
        _mla_prep_kernel,
        grid=(n // tm,),
        in_specs=[
            pl.BlockSpec((tm, MLA_Q_RANK), lambda i: (i, EV_CQ // MLA_Q_RANK)),
            pl.BlockSpec((tm, MLA_KV_RANK), lambda i: (i, EV_CKV // MLA_KV_RANK)),
            pl.BlockSpec((tm, LANES), lambda i: (i, EV_KR // LANES)),
            pl.BlockSpec((1, MLA_Q_RANK), fixed),
            pl.BlockSpec((1, MLA_KV_RANK), fixed),
            pl.BlockSpec((MLA_Q_RANK, qk_cols), fixed),
            pl.BlockSpec((MLA_KV_RANK, v_cols), fixed),
            pl.BlockSpec((v_cols, MLA_KV_RANK), fixed),
            pl.BlockSpec((tm, LANES), row),
            pl.BlockSpec((tm, LANES), row),
        ],
        out_specs=[
            pl.BlockSpec((tm, qk_cols), row),
            pl.BlockSpec((tm, qk_cols), row),
            pl.BlockSpec((1, MLA_HEADS, vt_rows, tm), lambda i: (i // tiles_per_seq, 0, 0, i % tiles_per_seq)),
        ],
        out_shape=[
            jax.ShapeDtypeStruct((n, qk_cols), BF16),
            jax.ShapeDtypeStruct((n, qk_cols), BF16),
            jax.ShapeDtypeStruct((b, MLA_HEADS, vt_rows, s), BF16),
        ],
        compiler_params=_compiler_params(("parallel",), vmem),
        name="mla_prep",
    )(z, z, z, g_cq.reshape(1, -1), g_ckv.reshape(1, -1), wq, wk, wv_t, cos_t, sin_t)


def _mla_attn_kernel(q_ref, k_ref, vt_ref, o_ref, *, n_chunks, tk):
    q = q_ref[0]
    tq = q.shape[0]
    rows = vt_ref.shape[2]

    def scores(c):
        k = k_ref[0, c * tk:(c + 1) * tk, :]
        return lax.dot_general(k, q, (((1,), (1,)), ((), ())), preferred_element_type=F32)

    m = jnp.full((1, tq), NEG_INIT, F32)
    acc = jnp.zeros((rows, tq), F32)
    s = scores(0)
    for c in range(n_chunks):
        s_next = scores(c + 1) if c + 1 < n_chunks else None
        m_new = jnp.maximum(m, jnp.max(s, axis=0, keepdims=True))
        alpha = jnp.exp2(m - m_new)
        p = jnp.exp2(s - m_new).astype(BF16)
        acc = alpha * acc + jnp.dot(vt_ref[0, 0, :, c * tk:(c + 1) * tk], p, preferred_element_type=F32)
        m, s = m_new, s_next
    o_t = acc[:MLA_V] * (1.0 / acc[MLA_V:MLA_V + 1])
    o_ref[0] = o_t.T.astype(o_ref.dtype)


def _mla_attention(q, k, vt):
    b, s, _ = q.shape
    tq, tk = Tiles.mla_q, Tiles.mla_k
    rows = vt.shape[2]
    assert s % tq == 0 and s % tk == 0
    vmem = (2 * tq * MLA_QK_PAD * 2 + 2 * s * (MLA_QK_PAD + rows) * 2 + 2 * tq * MLA_V * 2
            + 6 * tq * tk * 4)
    return pl.pallas_call(
        functools.partial(_mla_attn_kernel, n_chunks=s // tk, tk=tk),
        grid=(b, MLA_HEADS, s // tq),
        in_specs=[
            pl.BlockSpec((1, tq, MLA_QK_PAD), lambda bi, h, i: (bi, i, h)),
            pl.BlockSpec((1, s, MLA_QK_PAD), lambda bi, h, i: (bi, 0, h)),
            pl.BlockSpec((1, 1, rows, s), lambda bi, h, i: (bi, h, 0, 0)),
        ],
        out_specs=pl.BlockSpec((1, tq, MLA_V), lambda bi, h, i: (bi, i, h)),
        out_shape=jax.ShapeDtypeStruct((b, s, MLA_HEADS * MLA_V), BF16),
        compiler_params=_compiler_params(("parallel", "parallel", "arbitrary"), vmem),
        name="mla_attn",
    )(q, k, vt)


def _log_sigmoid(x):
    return jnp.minimum(x, 0.0) - jnp.log(1.0 + jnp.exp(-jnp.abs(x)))


def _mlstm_masks(chunk):
    t_idx = lax.broadcasted_iota(jnp.int32, (chunk, chunk), 0)
    s_idx = lax.broadcasted_iota(jnp.int32, (chunk, chunk), 1)
    lower, upper = s_idx <= t_idx, s_idx >= t_idx
    one_hot = lambda m: jnp.where(m, 1.0, 0.0).astype(F32)
    blocked = lambda m: jnp.where(m, 0.0, NEG_INIT).astype(F32)
    return {"eye": one_hot(s_idx == t_idx), "lower": one_hot(lower), "upper": one_hot(upper),
            "neg_lower": blocked(lower), "neg_upper": blocked(upper)}


def _mlstm_chunk(q_ref, k_ref, v_ref, gates, masks, c_ref, n_row, m_st, start, chunk, li_idx, lf_idx, reverse):
    L = chunk
    q = q_ref[0, pl.ds(start, L), :]
    k = k_ref[0, pl.ds(start, L), :]
    v = v_ref[0, pl.ds(start, L), :].astype(BF16)
    li_row = gates[li_idx:li_idx + 1, :]
    lf_row = _log_sigmoid(gates[lf_idx:lf_idx + 1, :])
    li_col = jnp.sum(masks["eye"] * li_row, axis=1, keepdims=True)
    lf_col = jnp.sum(masks["eye"] * lf_row, axis=1, keepdims=True)
    visible, visible_t, blocked = ((masks["upper"], masks["lower"], masks["neg_upper"]) if reverse
                                   else (masks["lower"], masks["upper"], masks["neg_lower"]))
    a_col = jnp.sum(visible * lf_row, axis=1, keepdims=True)
    a_row = jnp.sum(visible_t * lf_col, axis=0, keepdims=True)
    g = jnp.sum(lf_row, axis=1, keepdims=True)

    dmat = (a_col - a_row) + li_row + blocked
    inter = a_col + m_st
    m = jnp.maximum(inter, jnp.max(dmat, axis=1, keepdims=True))
    w_inter = jnp.exp(inter - m)
    q16 = q.astype(BF16)
    qk = (lax.dot_general(q16, k.astype(BF16), (((1,), (1,)), ((), ())), preferred_element_type=F32)
          * jnp.exp(dmat - m))
    num = (w_inter * jnp.dot(q16, c_ref[...].astype(BF16), preferred_element_type=F32)
           + jnp.dot(qk.astype(BF16), v, preferred_element_type=F32))
    den = w_inter * jnp.sum(q * n_row, axis=1, keepdims=True) + jnp.sum(qk, axis=1, keepdims=True)
    h_out = num / jnp.maximum(jnp.abs(den), jnp.exp(-m))

    r_col = g - a_col + li_col
    m_new = jnp.maximum(g + m_st, jnp.max(r_col, axis=0, keepdims=True))
    w_old = jnp.exp(g + m_st - m_new)
    kw = k * jnp.exp(r_col - m_new)
    c_ref[...] = w_old * c_ref[...] + jnp.dot(kw.T.astype(BF16), v, preferred_element_type=F32)
    n_new = w_old * n_row + jnp.sum(kw, axis=0, keepdims=True)
    return h_out, n_new, m_new


def _mlstm_kernel(q_ref, k_ref, v_ref, gate_ref, bias_ref, mo_ref, gn_ref, o_ref,
                  hf_ref, hb_ref, cf_ref, cb_ref, *, seq, chunk, post_rows):
    nc = seq // chunk
    cf_ref[...] = jnp.zeros_like(cf_ref)
    cb_ref[...] = jnp.zeros_like(cb_ref)
    bias = bias_ref[0]
    masks = _mlstm_masks(chunk)

    def body(i, carry):
        n_f, m_f, n_b, m_b = carry
        sf = pl.multiple_of(i * chunk, chunk)
        sb = pl.multiple_of((nc - 1 - i) * chunk, chunk)
        h_f, n_f, m_f = _mlstm_chunk(q_ref, k_ref, v_ref, gate_ref[0, 0, :, pl.ds(sf, chunk)] + bias, masks,
                                     cf_ref, n_f, m_f, sf, chunk, 0, 1, False)
        hf_ref[pl.ds(sf, chunk), :] = h_f
        h_b, n_b, m_b = _mlstm_chunk(q_ref, k_ref, v_ref, gate_ref[0, 0, :, pl.ds(sb, chunk)] + bias, masks,
                                     cb_ref, n_b, m_b, sb, chunk, 2, 3, True)
        hb_ref[pl.ds(sb, chunk), :] = h_b
        return n_f, m_f, n_b, m_b

    n0 = jnp.zeros((1, ML_QK), F32)
    m0 = jnp.full((1, 1), NEG_INIT, F32)
    lax.fori_loop(0, nc, body, (n0, m0, n0, m0), unroll=2)

    gn = gn_ref[...]

    def post(i, carry):
        r = pl.multiple_of(i * post_rows, post_rows)
        hm = hf_ref[pl.ds(r, post_rows), :] + hb_ref[pl.ds(r, post_rows), :]
        y = _rms(hm, gn)
        o_ref[0, pl.ds(r, post_rows), :] = (y * jax.nn.sigmoid(mo_ref[0, pl.ds(r, post_rows), :])).astype(o_ref.dtype)
        return carry

    lax.fori_loop(0, seq // post_rows, post, 0)


def _mlstm(z3, gates_t, gate_bias, g_mlstm):
    b, s, _ = z3.shape
    chunk = Tiles.mlstm_chunk
    assert s % (2 * chunk) == 0 and s % Tiles.mlstm_post_rows == 0
    vmem = (2 * s * (2 * ML_QK + 2 * ML_V + LANES) * 4 + 2 * s * ML_V * 2 + 2 * s * ML_V * 4
            + 2 * ML_QK * ML_V * 4 + 24 * chunk * chunk * 4 + (2 << 20))
    return pl.pallas_call(
        functools.partial(_mlstm_kernel, seq=s, chunk=chunk, post_rows=Tiles.mlstm_post_rows),
        grid=(b, ML_HEADS),
        in_specs=[
            pl.BlockSpec((1, s, ML_QK), lambda bi, h: (bi, 0, EV_MQ // ML_QK + h)),
            pl.BlockSpec((1, s, ML_QK), lambda bi, h: (bi, 0, EV_MK // ML_QK + h)),
            pl.BlockSpec((1, s, ML_V), lambda bi, h: (bi, 0, EV_MV // ML_V + h)),
            pl.BlockSpec((1, 1, 4, s), lambda bi, h: (bi, h, 0, 0)),
            pl.BlockSpec((1, 4, 1), lambda bi, h: (h, 0, 0)),
            pl.BlockSpec((1, s, ML_V), lambda bi, h: (bi, 0, EV_MO // ML_V + h)),
            pl.BlockSpec((1, ML_V), lambda bi, h: (0, h)),
        ],
        out_specs=pl.BlockSpec((1, s, ML_V), lambda bi, h: (bi, 0, h)),
        out_shape=jax.ShapeDtypeStruct((b, s, ML_HEADS * ML_V), BF16),
        scratch_shapes=[
            pltpu.VMEM((s, ML_V), F32),
            pltpu.VMEM((s, ML_V), F32),
            pltpu.VMEM((ML_QK, ML_V), F32),
            pltpu.VMEM((ML_QK, ML_V), F32),
        ],
        compiler_params=_compiler_params(("parallel", "parallel"), vmem),
        name="mlstm",
    )(z3, z3, z3, gates_t, gate_bias, z3, g_mlstm.reshape(1, -1))


def _outproj_kernel(*refs, n_lhs):
    x_ref = refs[0]
    a_refs = refs[1:1 + n_lhs]
    w_refs = refs[1 + n_lhs:1 + 2 * n_lhs]
    o_ref = refs[1 + 2 * n_lhs]
    acc = x_ref[...]
    for a_ref, w_ref in zip(a_refs, w_refs):
        acc = acc + jnp.dot(a_ref[...], w_ref[...], preferred_element_type=F32)
    o_ref[...] = acc


def _outproj(x, lhs, weights):
    n, d = x.shape
    tm = Tiles.out_rows
    assert n % tm == 0
    row = lambda i: (i, 0)
    fixed = lambda i: (0, 0)
    k_total = sum(a.shape[1] for a in lhs)
    vmem = 2 * 2 * tm * d * 4 + 2 * tm * k_total * 2 + 2 * k_total * d * 2 + tm * d * 4
    return pl.pallas_call(
        functools.partial(_outproj_kernel, n_lhs=len(lhs)),
        grid=(n // tm,),
        in_specs=([pl.BlockSpec((tm, d), row)]
                  + [pl.BlockSpec((tm, a.shape[1]), row) for a in lhs]
                  + [pl.BlockSpec(w.shape, fixed) for w in weights]),
        out_specs=pl.BlockSpec((tm, d), row),
        out_shape=jax.ShapeDtypeStruct((n, d), F32),
        compiler_params=_compiler_params(("parallel",), vmem),
        name="outproj",
    )(x, *lhs, *[w.astype(BF16) for w in weights])


def _diff_attn_kernel(slope_ref, q1_ref, q2_ref, k1_ref, k2_ref, vt_ref, pq_ref, pk_ref, lam_ref, gs_ref, o_ref,
                      vta_ref, *, n_chunks, tk, lam_init):
    h = pl.program_id(1)
    slope = slope_ref[h] * LOG2_E
    q1 = q1_ref[0]
    q2 = q2_ref[0]
    tq = q1.shape[0]
    dv = 2 * DA_HEAD
    rows = vta_ref.shape[0]

    @pl.when(pl.program_id(2) == 0)
    def _():
        vta_ref[:dv, :] = vt_ref[0, 0]
        vta_ref[dv:, :] = jnp.ones((rows - dv, vta_ref.shape[1]), BF16)
    pos_q = slope * pq_ref[0].astype(F32)
    nt = (((1,), (1,)), ((), ()))

    def scores(c):
        rows_c = slice(c * tk, (c + 1) * tk)
        dist = jnp.abs(slope * pk_ref[0, rows_c, :].astype(F32) - pos_q)
        s1 = lax.dot_general(k1_ref[0, rows_c, :], q1, nt, preferred_element_type=F32) - dist
        s2 = lax.dot_general(k2_ref[0, rows_c, :], q2, nt, preferred_element_type=F32) - dist
        return s1, s2

    def softmax_step(s, vt, m, acc):
        m_new = jnp.maximum(m, jnp.max(s, axis=0, keepdims=True))
        alpha = jnp.exp2(m - m_new)
        p = jnp.exp2(s - m_new).astype(BF16)
        return m_new, alpha * acc + jnp.dot(vt, p, preferred_element_type=F32)

    m1 = m2 = jnp.full((1, tq), NEG_INIT, F32)
    acc1 = acc2 = jnp.zeros((rows, tq), F32)
    s1, s2 = scores(0)
    for c in range(n_chunks):
        nxt = scores(c + 1) if c + 1 < n_chunks else (None, None)
        vt = vta_ref[:, c * tk:(c + 1) * tk]
        m1, acc1 = softmax_step(s1, vt, m1, acc1)
        m2, acc2 = softmax_step(s2, vt, m2, acc2)
        s1, s2 = nxt

    lam_vec = lam_ref[...]
    lam = (jnp.exp(jnp.sum(lam_vec[0:1] * lam_vec[1:2], axis=1, keepdims=True))
           - jnp.exp(jnp.sum(lam_vec[2:3] * lam_vec[3:4], axis=1, keepdims=True)) + lam_init)
    w1 = 1.0 / acc1[dv:dv + 1]
    w2 = lam / acc2[dv:dv + 1]
    o_t = acc1[:dv] * w1 - acc2[:dv] * w2
    o_ref[0] = (_rms(o_t.T, gs_ref[...]) * (1.0 - lam_init)).astype(o_ref.dtype)


def _diff_attention(zq, vt, pos_col, pos_row, slopes, lam_vec, g_sub, lam_init):
    b, s, _ = zq.shape
    tq, tk = Tiles.attn_q, Tiles.attn_k
    assert s % tq == 0 and s % tk == 0
    dv = 2 * DA_HEAD
    rows = dv + BF16_SUBLANES
    k_base = DA_HEADS * 2
    vmem = (2 * 2 * tq * DA_HEAD * 2 + 3 * s * (2 * DA_HEAD + rows) * 2 + 2 * s * LANES * 4 + 2 * tq * dv * 2
            + 14 * tq * tk * 4)
    return pl.pallas_call(
        functools.partial(_diff_attn_kernel, n_chunks=s // tk, tk=tk, lam_init=lam_init),
        grid=(b, DA_HEADS, s // tq),
        in_specs=[
            pl.BlockSpec(memory_space=pltpu.SMEM),
            pl.BlockSpec((1, tq, DA_HEAD), lambda bi, h, i: (bi, i, 2 * h)),
            pl.BlockSpec((1, tq, DA_HEAD), lambda bi, h, i: (bi, i, 2 * h + 1)),
            pl.BlockSpec((1, s, DA_HEAD), lambda bi, h, i: (bi, 0, k_base + 2 * h)),
            pl.BlockSpec((1, s, DA_HEAD), lambda bi, h, i: (bi, 0, k_base + 2 * h + 1)),
            pl.BlockSpec((1, 1, dv, s), lambda bi, h, i: (bi, h, 0, 0)),
            pl.BlockSpec((1, 1, tq), lambda bi, h, i: (bi, 0, i)),
            pl.BlockSpec((1, s, 1), lambda bi, h, i: (bi, 0, 0)),
            pl.BlockSpec((4, DA_HEAD), lambda bi, h, i: (0, 0)),
            pl.BlockSpec((1, dv), lambda bi, h, i: (0, 0)),
        ],
        out_specs=pl.BlockSpec((1, tq, dv), lambda bi, h, i: (bi, i, h)),
        out_shape=jax.ShapeDtypeStruct((b, s, DA_HEADS * dv), BF16),
        scratch_shapes=[pltpu.VMEM((rows, s), BF16)],
        compiler_params=_compiler_params(("parallel", "parallel", "arbitrary"), vmem),
        name="diff_attn",
    )(slopes, zq, zq, zq, zq, vt, pos_row, pos_col, lam_vec, g_sub.reshape(1, dv))


def _pack_even_w_in(w_in):
    d = w_in.shape[0]
    sizes = (MLA_Q_RANK, MLA_KV_RANK, MLA_ROPE, ML_HEADS * ML_QK, ML_HEADS * ML_QK, ML_HEADS * ML_V,
             ML_HEADS * ML_V, 4 * ML_HEADS)
    c_q, c_kv, k_r, m_q, m_k, m_v, m_o, m_g = jnp.split(w_in, np.cumsum(sizes)[:-1].tolist(), axis=1)
    pad = lambda w, width: jnp.concatenate([w, jnp.zeros((d, width - w.shape[1]), w.dtype)], axis=1)
    packed = jnp.concatenate([c_q, c_kv, m_v, m_o, m_q, m_k, pad(k_r, LANES), pad(m_g, LANES)], axis=1)
    assert packed.shape[1] == EV_COLS
    return packed


def _pack_w_uq(w_uq):
    r = w_uq.shape[0]
    w = w_uq.reshape(r, MLA_HEADS, MLA_NOPE + MLA_ROPE)
    w = jnp.concatenate([w, jnp.zeros((r, MLA_HEADS, MLA_QK_PAD - MLA_NOPE - MLA_ROPE), w.dtype)], axis=2)
    return w.reshape(r, MLA_HEADS * MLA_QK_PAD).astype(BF16)


def _split_w_ukv(w_ukv):
    r = w_ukv.shape[0]
    w = w_ukv.reshape(r, MLA_HEADS, MLA_NOPE + MLA_V)
    wk = w[:, :, :MLA_NOPE].reshape(r, MLA_HEADS * MLA_NOPE)
    wv_t = w[:, :, MLA_NOPE:].reshape(r, MLA_HEADS * MLA_V).T
    return wk.astype(BF16), wv_t.astype(BF16)


def _rope_tables(positions):
    inv_freq = ROPE_THETA ** (-jnp.arange(0, MLA_ROPE, 2, dtype=F32) / MLA_ROPE)
    ang = positions.astype(F32).reshape(-1, 1) * inv_freq
    cos, sin = jnp.cos(ang), jnp.sin(ang)
    zeros = jnp.zeros((ang.shape[0], LANES - MLA_ROPE), F32)
    return jnp.concatenate([cos, cos, zeros], axis=1), jnp.concatenate([-sin, sin, zeros], axis=1)


def _even_mixer(x, b, s, positions, norm, w_in, g_cq, w_uq, g_ckv, w_ukv, b_gates, g_mlstm, w_o):
    n = b * s
    col_scale = jnp.ones((EV_COLS,), F32).at[EV_MK:EV_MK + ML_HEADS * ML_QK].set(ML_QK ** -0.5)
    z = _normproj(x, norm, _pack_even_w_in(w_in.astype(BF16)), col_scale)

    cos_t, sin_t = _rope_tables(positions)
    wk, wv_t = _split_w_ukv(w_ukv)
    q, k, vt = _mla_prep(z, b, s, g_cq, g_ckv, _pack_w_uq(w_uq), wk, wv_t, cos_t, sin_t)
    a_out = _mla_attention(q.reshape(b, s, -1), k.reshape(b, s, -1), vt)

    z3 = z.reshape(b, s, EV_COLS)
    gates_t = z3[:, :, EV_MG:EV_MG + 4 * ML_HEADS].reshape(b, s, 4, ML_HEADS).transpose(0, 3, 2, 1)
    gate_bias = b_gates.astype(F32).reshape(4, ML_HEADS).T.reshape(ML_HEADS, 4, 1)
    m_out = _mlstm(z3, gates_t, gate_bias, g_mlstm)

    split = MLA_HEADS * MLA_V
    return _outproj(x, [a_out.reshape(n, -1), m_out.reshape(n, -1)], [w_o[:split], w_o[split:]])


def _odd_mixer(x, b, s, positions, norm, w_in, lam_q1, lam_k1, lam_q2, lam_k2, g_sub, w_o, lam_init):
    n = b * s
    width = DA_HEADS * 2 * DA_HEAD
    zq, vt = _qkv_proj(x, b, s, norm, w_in, LOG2_E * DA_HEAD ** -0.5, DA_HEADS, 2 * DA_HEAD)
    zq = zq.reshape(b, s, 2 * width)
    slopes = jnp.asarray([2.0 ** (-8.0 * (h + 1) / DA_HEADS) for h in range(DA_HEADS)], dtype=F32)
    lam_vec = jnp.stack([lam_q1, lam_k1, lam_q2, lam_k2]).astype(F32)
    pos = positions - positions[:, :1]
    o = _diff_attention(zq, vt, pos.reshape(b, s, 1), pos.reshape(b, 1, s), slopes, lam_vec, g_sub, lam_init)
    return _outproj(x, [o.reshape(n, width)], [w_o])


def kernel(x, positions, l0_ffn1_norm, l0_ffn1_w_gu, l0_ffn1_w_down, l0_mix_norm, l0_w_in, l0_g_cq, l0_w_uq, l0_g_ckv, l0_w_ukv, l0_b_gates, l0_g_mlstm, l0_w_o, l0_ffn2_norm, l0_ffn2_w_gu, l0_ffn2_w_down, l1_ffn1_norm, l1_ffn1_w_gu, l1_ffn1_w_down, l1_mix_norm, l1_w_in, l1_lam_q1, l1_lam_k1, l1_lam_q2, l1_lam_k2, l1_g_sub, l1_w_o, l1_ffn2_norm, l1_ffn2_w_gu, l1_ffn2_w_down, final_norm):
    b, s, d = x.shape
    h = x.reshape(b * s, d)
    h = _ffn(h, l0_ffn1_norm, l0_ffn1_w_gu, l0_ffn1_w_down)
    h = _even_mixer(h, b, s, positions, l0_mix_norm, l0_w_in, l0_g_cq, l0_w_uq, l0_g_ckv, l0_w_ukv, l0_b_gates,
                    l0_g_mlstm, l0_w_o)
    h = _ffn(h, l0_ffn2_norm, l0_ffn2_w_gu, l0_ffn2_w_down)
    h = _ffn(h, l1_ffn1_norm, l1_ffn1_w_gu, l1_ffn1_w_down)
    lam_init = 0.8 - 0.6 * math.exp(-0.3 * 1)
    h = _odd_mixer(h, b, s, positions, l1_mix_norm, l1_w_in, l1_lam_q1, l1_lam_k1, l1_lam_q2, l1_lam_k2, l1_g_sub,
                   l1_w_o, lam_init)
    h = _ffn(h, l1_ffn2_norm, l1_ffn2_w_gu, l1_ffn2_w_down, final_gain=final_norm)
    return h.reshape(b, s, d)
```

```python
import functools
import math

import jax
import jax.numpy as jnp
import numpy as np
from jax import lax
from jax.experimental import pallas as pl
from jax.experimental.pallas import tpu as pltpu

F32 = jnp.float32
BF16 = jnp.bfloat16

D_MODEL = 2048
MLA_HEADS = 8
MLA_Q_RANK = 512
MLA_KV_RANK = 512
MLA_NOPE = 128
MLA_ROPE = 64
MLA_V = 128
ROPE_THETA = 10000.0
ML_HEADS = 4
ML_QK = 128
ML_V = 256
DA_HEADS = 8
DA_HEAD = 128
D_FF = 5632
EPS = 1e-6
NEG_INIT = -1e30
LOG2_E = math.log2(math.e)

LANES = 128
BF16_SUBLANES = 16
MLA_QK_PAD = 256
V7X_VMEM_BYTES = 64 * 1024 * 1024
VMEM_REQUEST_CAP_BYTES = V7X_VMEM_BYTES - 6 * 1024 * 1024

EV_CQ = 0
EV_CKV = EV_CQ + MLA_Q_RANK
EV_MV = EV_CKV + MLA_KV_RANK
EV_MO = EV_MV + ML_HEADS * ML_V
EV_MQ = EV_MO + ML_HEADS * ML_V
EV_MK = EV_MQ + ML_HEADS * ML_QK
EV_KR = EV_MK + ML_HEADS * ML_QK
EV_MG = EV_KR + LANES
EV_COLS = EV_MG + LANES
assert EV_MV % ML_V == 0 and EV_MO % ML_V == 0 and EV_MQ % ML_QK == 0 and EV_MK % ML_QK == 0


class Tiles:
    ffn_rows = 1024
    ffn_cols = 512
    proj_rows = 512
    proj_col_steps_even = 2
    prep_rows = 512
    out_rows = 512
    mla_q = 2048
    mla_k = 512
    attn_q = 1024
    attn_k = 512
    mlstm_chunk = 256
    mlstm_post_rows = 512


def _compiler_params(semantics, vmem_bytes):
    return pltpu.CompilerParams(dimension_semantics=semantics,
                                vmem_limit_bytes=int(min(vmem_bytes, VMEM_REQUEST_CAP_BYTES)))


def _cast_view(w, steps):
    for lanes in (8 * LANES, 4 * LANES, 2 * LANES, LANES):
        rows = w.size // lanes
        if w.size % lanes == 0 and rows % (steps * BF16_SUBLANES) == 0:
            return w.reshape(rows, lanes)
    raise ValueError(f"no lane-aligned view of {w.shape} splits over {steps} steps")


def _side_cast_specs(side_casts, steps, step_index):
    views = [_cast_view(w, steps) for w in side_casts]
    specs = [pl.BlockSpec((v.shape[0] // steps, v.shape[1]), lambda *idx: (step_index(*idx), 0)) for v in views]
    shapes = [jax.ShapeDtypeStruct(v.shape, BF16) for v in views]
    return views, specs, list(specs), shapes


def _rms(x, gain):
    y = x * lax.rsqrt(jnp.mean(x * x, axis=-1, keepdims=True) + EPS)
    return y * gain


def _ffn_kernel(*refs, n_ff_steps, final):
    if final:
        x_ref, g_ref, wg_ref, wu_ref, wd_ref, gf_ref, o_ref, xn_ref = refs
    else:
        x_ref, g_ref, wg_ref, wu_ref, wd_ref, o_ref, xn_ref = refs
    j = pl.program_id(1)

    def branch(xn):
        gate = jnp.dot(xn, wg_ref[...], preferred_element_type=F32)
        up = jnp.dot(xn, wu_ref[...], preferred_element_type=F32)
        act = (0.5 * (gate * jax.nn.sigmoid(gate)) * up).astype(BF16)
        return jnp.dot(act, wd_ref[...], preferred_element_type=F32)

    @pl.when(j == 0)
    def _():
        x = x_ref[...]
        xn = _rms(x, g_ref[...]).astype(BF16)
        xn_ref[...] = xn
        o_ref[...] = x + branch(xn)

    last = n_ff_steps - 1

    @pl.when((j > 0) & (j < last) if final else j > 0)
    def _():
        o_ref[...] += branch(xn_ref[...])

    if final:
        @pl.when(j == last)
        def _():
            o_ref[...] = _rms(o_ref[...] + branch(xn_ref[...]), gf_ref[...])


def _ffn(x, gain, w_gu, w_down, final_gain=None):
    n, d = x.shape
    tm, tf = Tiles.ffn_rows, Tiles.ffn_cols
    assert n % tm == 0 and D_FF % tf == 0 and D_FF // tf >= 2 and w_gu.shape == (d, 2 * D_FF)
    nf = D_FF // tf
    wgu = w_gu.astype(BF16)
    wd = w_down.astype(BF16)
    final = final_gain is not None
    in_specs = [
        pl.BlockSpec((tm, d), lambda i, j: (i, 0)),
        pl.BlockSpec((1, d), lambda i, j: (0, 0)),
        pl.BlockSpec((d, tf), lambda i, j: (0, j)),
        pl.BlockSpec((d, tf), lambda i, j: (0, j + nf)),
        pl.BlockSpec((tf, d), lambda i, j: (j, 0)),
    ]
    args = [x, gain.reshape(1, d), wgu, wgu, wd]
    if final:
        in_specs.append(pl.BlockSpec((1, d), lambda i, j: (0, 0)))
        args.append(final_gain.reshape(1, d))
    vmem = (2 * 2 * tm * d * 4 + tm * d * 2 + 2 * 3 * d * tf * 2 + 4 * tm * tf * 4 + tm * d * 4)
    return pl.pallas_call(
        functools.partial(_ffn_kernel, n_ff_steps=nf, final=final),
        grid=(n // tm, nf),
        in_specs=in_specs,
        out_specs=pl.BlockSpec((tm, d), lambda i, j: (i, 0)),
        out_shape=jax.ShapeDtypeStruct((n, d), F32),
        scratch_shapes=[pltpu.VMEM((tm, d), BF16)],
        compiler_params=_compiler_params(("parallel", "arbitrary"), vmem),
        name="ffn",
    )(*args)


def _normproj_kernel(x_ref, g_ref, w_ref, cs_ref, o_ref, xn_ref):
    tn = o_ref.shape[1]

    def project(xn):
        col = pl.multiple_of(pl.program_id(1) * tn, LANES)
        acc = jnp.dot(xn, w_ref[:, pl.ds(col, tn)], preferred_element_type=F32)
        o_ref[...] = (acc * cs_ref[...]).astype(o_ref.dtype)

    @pl.when(pl.program_id(1) == 0)
    def _():
        xn = _rms(x_ref[...], g_ref[...]).astype(BF16)
        xn_ref[...] = xn
        project(xn)

    @pl.when(pl.program_id(1) > 0)
    def _():
        project(xn_ref[...])


def _normproj(x, gain, w, col_scale):
    n, d = x.shape
    cols = w.shape[1]
    col_steps = Tiles.proj_col_steps_even
    tm, tn = Tiles.proj_rows, cols // col_steps
    assert n % tm == 0 and tn * col_steps == cols and tn % LANES == 0
    out_dtype = F32
    vmem = 2 * tm * d * 4 + tm * d * 2 + d * cols * 2 + 2 * tm * tn * 4 + 2 * tm * tn * 4 + (4 << 20)
    return pl.pallas_call(
        _normproj_kernel,
        grid=(n // tm, col_steps),
        in_specs=[
            pl.BlockSpec((tm, d), lambda i, j: (i, 0)),
            pl.BlockSpec((1, d), lambda i, j: (0, 0)),
            pl.BlockSpec((d, cols), lambda i, j: (0, 0), pipeline_mode=pl.Buffered(1)),
            pl.BlockSpec((1, tn), lambda i, j: (0, j)),
        ],
        out_specs=pl.BlockSpec((tm, tn), lambda i, j: (i, j)),
        out_shape=jax.ShapeDtypeStruct((n, cols), out_dtype),
        scratch_shapes=[pltpu.VMEM((tm, d), BF16)],
        compiler_params=_compiler_params(("parallel", "arbitrary"), vmem),
        name="normproj",
    )(x, gain.reshape(1, d), w.astype(BF16), col_scale.reshape(1, cols))


def _qkv_proj_kernel(x_ref, g_ref, w_ref, o_ref, vt_ref, xn_ref, *, q_scale, heads, head_width):
    j = pl.program_id(1)

    @pl.when(j == 0)
    def _():
        xn_ref[...] = _rms(x_ref[...], g_ref[...]).astype(BF16)

    width = heads * head_width

    def project(part):
        return jnp.dot(xn_ref[...], w_ref[:, part * width:(part + 1) * width], preferred_element_type=F32)

    @pl.when(j == 0)
    def _():
        o_ref[...] = (project(0) * q_scale).astype(o_ref.dtype)

    @pl.when(j == 1)
    def _():
        o_ref[...] = project(1).astype(o_ref.dtype)

    @pl.when(j == 2)
    def _():
        acc = project(2)
        for h in range(heads):
            vt_ref[0, h] = acc[:, h * head_width:(h + 1) * head_width].T.astype(vt_ref.dtype)


def _qkv_proj(x, b, s, gain, w, q_scale, heads, head_width):
    n, d = x.shape
    width = heads * head_width
    tm = Tiles.proj_rows
    assert w.shape == (d, 3 * width) and s % tm == 0 and n == b * s
    tiles_per_seq = s // tm
    vmem = 2 * tm * d * 4 + tm * d * 2 + 3 * d * width * 2 + 4 * tm * width * 2 + 3 * tm * width * 4
    return pl.pallas_call(
        functools.partial(_qkv_proj_kernel, q_scale=q_scale, heads=heads, head_width=head_width),
        grid=(n // tm, 3),
        in_specs=[
            pl.BlockSpec((tm, d), lambda i, j: (i, 0)),
            pl.BlockSpec((1, d), lambda i, j: (0, 0)),
            pl.BlockSpec((d, 3 * width), lambda i, j: (0, 0), pipeline_mode=pl.Buffered(1)),
        ],
        out_specs=[
            pl.BlockSpec((tm, width), lambda i, j: (i, jnp.minimum(j, 1))),
            pl.BlockSpec((1, heads, head_width, tm), lambda i, j: (i // tiles_per_seq, 0, 0, i % tiles_per_seq)),
        ],
        out_shape=[
            jax.ShapeDtypeStruct((n, 2 * width), BF16),
            jax.ShapeDtypeStruct((b, heads, head_width, s), BF16),
        ],
        scratch_shapes=[pltpu.VMEM((tm, d), BF16)],
        compiler_params=_compiler_params(("parallel", "arbitrary"), vmem),
        name="qkv_proj",
    )(x, gain.reshape(1, d), w.astype(BF16))


def _rope_tile(x, cos_t, sin_t):
    lane = lax.broadcasted_iota(jnp.int32, x.shape, 1)
    half = MLA_ROPE // 2
    partner = jnp.where(lane % MLA_ROPE < half, pltpu.roll(x, LANES - half, 1), pltpu.roll(x, half, 1))
    return x * cos_t + partner * sin_t


def _mla_prep_kernel(cq_ref, ckv_ref, kr_ref, gq_ref, gkv_ref, wq_ref, wk_ref, wvt_ref, cos_ref, sin_ref,
                     q_ref, k_ref, vt_ref):
    cos_t = cos_ref[...]
    sin_t = sin_ref[...]
    cqn = _rms(cq_ref[...], gq_ref[...]).astype(BF16)
    ckvn_f32 = _rms(ckv_ref[...], gkv_ref[...])
    ckvn = ckvn_f32.astype(BF16)
    scale = LOG2_E * (MLA_NOPE + MLA_ROPE) ** -0.5
    q = jnp.dot(cqn, wq_ref[...], preferred_element_type=F32)
    kn = jnp.dot(ckvn, wk_ref[...], preferred_element_type=F32)
    v_t = jnp.dot(wvt_ref[...], ckvn_f32.T.astype(BF16), preferred_element_type=F32)
    ones = jnp.ones((BF16_SUBLANES, v_t.shape[1]), BF16)
    lane = lax.broadcasted_iota(jnp.int32, cos_t.shape, 1)
    k_rope = jnp.where(lane < MLA_ROPE, _rope_tile(kr_ref[...], cos_t, sin_t), 0.0).astype(BF16)
    for h in range(MLA_HEADS):
        base = h * MLA_QK_PAD
        q_ref[:, base:base + MLA_NOPE] = (q[:, base:base + MLA_NOPE] * scale).astype(BF16)
        q_rope = _rope_tile(q[:, base + MLA_NOPE:base + MLA_QK_PAD], cos_t, sin_t)
        q_ref[:, base + MLA_NOPE:base + MLA_QK_PAD] = (q_rope * scale).astype(BF16)
        k_ref[:, base:base + MLA_NOPE] = kn[:, h * MLA_NOPE:(h + 1) * MLA_NOPE].astype(BF16)
        k_ref[:, base + MLA_NOPE:base + MLA_QK_PAD] = k_rope
        vt_ref[0, h, :MLA_V, :] = v_t[h * MLA_V:(h + 1) * MLA_V].astype(BF16)
        vt_ref[0, h, MLA_V:, :] = ones


def _mla_prep(z, b, s, g_cq, g_ckv, wq, wk, wv_t, cos_t, sin_t):
    n = z.shape[0]
    tm = Tiles.prep_rows
    assert s % tm == 0 and n == b * s
    tiles_per_seq = s // tm
    qk_cols = MLA_HEADS * MLA_QK_PAD
    v_cols = MLA_HEADS * MLA_V
    vt_rows = MLA_V + BF16_SUBLANES
    row = lambda i: (i, 0)
    fixed = lambda i: (0, 0)
    vmem = (2 * tm * (MLA_Q_RANK + MLA_KV_RANK + 3 * LANES) * 4 + 2 * MLA_Q_RANK * (qk_cols + 2 * v_cols) * 2
            + 2 * tm * (2 * qk_cols + v_cols) * 2 + 3 * tm * qk_cols * 4)
    return pl.pallas_call(
        _mla_prep_kernel,
        grid=(n // tm,),
        in_specs=[
            pl.BlockSpec((tm, MLA_Q_RANK), lambda i: (i, EV_CQ // MLA_Q_RANK)),
            pl.BlockSpec((tm, MLA_KV_RANK), lambda i: (i, EV_CKV // MLA_KV_RANK)),
            pl.BlockSpec((tm, LANES), lambda i: (i, EV_KR // LANES)),
            pl.BlockSpec((1, MLA_Q_RANK), fixed),
            pl.BlockSpec((1, MLA_KV_RANK), fixed),
            pl.BlockSpec((MLA_Q_RANK, qk_cols), fixed),
            pl.BlockSpec((MLA_KV_RANK, v_cols), fixed),
            pl.BlockSpec((v_cols, MLA_KV_RANK), fixed),
            pl.BlockSpec((tm, LANES), row),
            pl.BlockSpec((tm, LANES), row),
        ],
        out_specs=[
            pl.BlockSpec((tm, qk_cols), row),
            pl.BlockSpec((tm, qk_cols), row),
            pl.BlockSpec((1, MLA_HEADS, vt_rows, tm), lambda i: (i // tiles_per_seq, 0, 0, i % tiles_per_seq)),
        ],
        out_shape=[
            jax.ShapeDtypeStruct((n, qk_cols), BF16),
            jax.ShapeDtypeStruct((n, qk_cols), BF16),
            jax.ShapeDtypeStruct((b, MLA_HEADS, vt_rows, s), BF16),
        ],
        compiler_params=_compiler_params(("parallel",), vmem),
        name="mla_prep",
    )(z, z, z, g_cq.reshape(1, -1), g_ckv.reshape(1, -1), wq, wk, wv_t, cos_t, sin_t)


def _mla_attn_kernel(q_ref, k_ref, vt_ref, *rest, n_chunks, tk, n_casts):
    cast_in, o_ref, cast_out = rest[:n_casts], rest[n_casts], rest[n_casts + 1:]
    for src, dst in zip(cast_in, cast_out):
        dst[...] = src[...].astype(BF16)
    q = q_ref[0]
    tq = q.shape[0]
    rows = vt_ref.shape[2]

    def scores(c):
        k = k_ref[0, c * tk:(c + 1) * tk, :]
        return lax.dot_general(k, q, (((1,), (1,)), ((), ())), preferred_element_type=F32)

    m = jnp.full((1, tq), NEG_INIT, F32)
    acc = jnp.zeros((rows, tq), F32)
    s = scores(0)
    for c in range(n_chunks):
        s_next = scores(c + 1) if c + 1 < n_chunks else None
        m_new = jnp.maximum(m, jnp.max(s, axis=0, keepdims=True))
        alpha = jnp.exp2(m - m_new)
        p = jnp.exp2(s - m_new).astype(BF16)
        acc = alpha * acc + jnp.dot(vt_ref[0, 0, :, c * tk:(c + 1) * tk], p, preferred_element_type=F32)
        m, s = m_new, s_next
    o_t = acc[:MLA_V] * (1.0 / acc[MLA_V:MLA_V + 1])
    o_ref[0] = o_t.T.astype(o_ref.dtype)


def _mla_attention(q, k, vt, side_casts=()):
    b, s, _ = q.shape
    tq, tk = Tiles.mla_q, Tiles.mla_k
    rows = vt.shape[2]
    assert s % tq == 0 and s % tk == 0
    nq = s // tq
    views, cast_in_specs, cast_out_specs, cast_shapes = _side_cast_specs(
        side_casts, b * MLA_HEADS * nq, lambda bi, h, i: (bi * MLA_HEADS + h) * nq + i)
    cast_bytes = sum(3 * 2 * v.size * 2 // (b * MLA_HEADS * nq) for v in views)
    vmem = (2 * tq * MLA_QK_PAD * 2 + 2 * s * (MLA_QK_PAD + rows) * 2 + 2 * tq * MLA_V * 2
            + 6 * tq * tk * 4 + cast_bytes)
    out = pl.pallas_call(
        functools.partial(_mla_attn_kernel, n_chunks=s // tk, tk=tk, n_casts=len(views)),
        grid=(b, MLA_HEADS, nq),
        in_specs=[
            pl.BlockSpec((1, tq, MLA_QK_PAD), lambda bi, h, i: (bi, i, h)),
            pl.BlockSpec((1, s, MLA_QK_PAD), lambda bi, h, i: (bi, 0, h)),
            pl.BlockSpec((1, 1, rows, s), lambda bi, h, i: (bi, h, 0, 0)),
        ] + cast_in_specs,
        out_specs=[pl.BlockSpec((1, tq, MLA_V), lambda bi, h, i: (bi, i, h))] + cast_out_specs,
        out_shape=[jax.ShapeDtypeStruct((b, s, MLA_HEADS * MLA_V), BF16)] + cast_shapes,
        compiler_params=_compiler_params(("parallel", "parallel", "arbitrary"), vmem),
        name="mla_attn",
    )(q, k, vt, *views)
    return out[0], [c.reshape(w.shape) for c, w in zip(out[1:], side_casts)]


def _log_sigmoid(x):
    return jnp.minimum(x, 0.0) - jnp.log(1.0 + jnp.exp(-jnp.abs(x)))


def _mlstm_masks(chunk):
    t_idx = lax.broadcasted_iota(jnp.int32, (chunk, chunk), 0)
    s_idx = lax.broadcasted_iota(jnp.int32, (chunk, chunk), 1)
    lower, upper = s_idx <= t_idx, s_idx >= t_idx
    one_hot = lambda m: jnp.where(m, 1.0, 0.0).astype(F32)
    blocked = lambda m: jnp.where(m, 0.0, NEG_INIT).astype(F32)
    return {"eye": one_hot(s_idx == t_idx), "lower": one_hot(lower), "upper": one_hot(upper),
            "neg_lower": blocked(lower), "neg_upper": blocked(upper)}


def _mlstm_chunk(q_ref, k_ref, v_ref, gates, masks, c_ref, n_row, m_st, start, chunk, li_idx, lf_idx, reverse):
    L = chunk
    q = q_ref[0, pl.ds(start, L), :]
    k = k_ref[0, pl.ds(start, L), :]
    v = v_ref[0, pl.ds(start, L), :].astype(BF16)
    li_row = gates[li_idx:li_idx + 1, :]
    lf_row = _log_sigmoid(gates[lf_idx:lf_idx + 1, :])
    li_col = jnp.sum(masks["eye"] * li_row, axis=1, keepdims=True)
    lf_col = jnp.sum(masks["eye"] * lf_row, axis=1, keepdims=True)
    visible, visible_t, blocked = ((masks["upper"], masks["lower"], masks["neg_upper"]) if reverse
                                   else (masks["lower"], masks["upper"], masks["neg_lower"]))
    a_col = jnp.sum(visible * lf_row, axis=1, keepdims=True)
    a_row = jnp.sum(visible_t * lf_col, axis=0, keepdims=True)
    g = jnp.sum(lf_row, axis=1, keepdims=True)

    dmat = (a_col - a_row) + li_row + blocked
    inter = a_col + m_st
    m = jnp.maximum(inter, jnp.max(dmat, axis=1, keepdims=True))
    w_inter = jnp.exp(inter - m)
    q16 = q.astype(BF16)
    qk = (lax.dot_general(q16, k.astype(BF16), (((1,), (1,)), ((), ())), preferred_element_type=F32)
          * jnp.exp(dmat - m))
    num = (w_inter * jnp.dot(q16, c_ref[...].astype(BF16), preferred_element_type=F32)
           + jnp.dot(qk.astype(BF16), v, preferred_element_type=F32))
    den = w_inter * jnp.sum(q * n_row, axis=1, keepdims=True) + jnp.sum(qk, axis=1, keepdims=True)
    h_out = num / jnp.maximum(jnp.abs(den), jnp.exp(-m))

    r_col = g - a_col + li_col
    m_new = jnp.maximum(g + m_st, jnp.max(r_col, axis=0, keepdims=True))
    w_old = jnp.exp(g + m_st - m_new)
    kw = k * jnp.exp(r_col - m_new)
    c_ref[...] = w_old * c_ref[...] + jnp.dot(kw.T.astype(BF16), v, preferred_element_type=F32)
    n_new = w_old * n_row + jnp.sum(kw, axis=0, keepdims=True)
    return h_out, n_new, m_new


def _mlstm_kernel(q_ref, k_ref, v_ref, gate_ref, bias_ref, mo_ref, gn_ref, o_ref,
                  hf_ref, hb_ref, cf_ref, cb_ref, *, seq, chunk, post_rows):
    nc = seq // chunk
    cf_ref[...] = jnp.zeros_like(cf_ref)
    cb_ref[...] = jnp.zeros_like(cb_ref)
    bias = bias_ref[0]
    masks = _mlstm_masks(chunk)

    def body(i, carry):
        n_f, m_f, n_b, m_b = carry
        sf = pl.multiple_of(i * chunk, chunk)
        sb = pl.multiple_of((nc - 1 - i) * chunk, chunk)
        h_f, n_f, m_f = _mlstm_chunk(q_ref, k_ref, v_ref, gate_ref[0, 0, :, pl.ds(sf, chunk)] + bias, masks,
                                     cf_ref, n_f, m_f, sf, chunk, 0, 1, False)
        hf_ref[pl.ds(sf, chunk), :] = h_f
        h_b, n_b, m_b = _mlstm_chunk(q_ref, k_ref, v_ref, gate_ref[0, 0, :, pl.ds(sb, chunk)] + bias, masks,
                                     cb_ref, n_b, m_b, sb, chunk, 2, 3, True)
        hb_ref[pl.ds(sb, chunk), :] = h_b
        return n_f, m_f, n_b, m_b

    n0 = jnp.zeros((1, ML_QK), F32)
    m0 = jnp.full((1, 1), NEG_INIT, F32)
    lax.fori_loop(0, nc, body, (n0, m0, n0, m0), unroll=2)

    gn = gn_ref[...]

    def post(i, carry):
        r = pl.multiple_of(i * post_rows, post_rows)
        hm = hf_ref[pl.ds(r, post_rows), :] + hb_ref[pl.ds(r, post_rows), :]
        y = _rms(hm, gn)
        o_ref[0, pl.ds(r, post_rows), :] = (y * jax.nn.sigmoid(mo_ref[0, pl.ds(r, post_rows), :])).astype(o_ref.dtype)
        return carry

    lax.fori_loop(0, seq // post_rows, post, 0)


def _mlstm(z3, gates_t, gate_bias, g_mlstm):
    b, s, _ = z3.shape
    chunk = Tiles.mlstm_chunk
    assert s % (2 * chunk) == 0 and s % Tiles.mlstm_post_rows == 0
    vmem = (2 * s * (2 * ML_QK + 2 * ML_V + LANES) * 4 + 2 * s * ML_V * 2 + 2 * s * ML_V * 4
            + 2 * ML_QK * ML_V * 4 + 24 * chunk * chunk * 4 + (2 << 20))
    return pl.pallas_call(
        functools.partial(_mlstm_kernel, seq=s, chunk=chunk, post_rows=Tiles.mlstm_post_rows),
        grid=(b, ML_HEADS),
        in_specs=[
            pl.BlockSpec((1, s, ML_QK), lambda bi, h: (bi, 0, EV_MQ // ML_QK + h)),
            pl.BlockSpec((1, s, ML_QK), lambda bi, h: (bi, 0, EV_MK // ML_QK + h)),
            pl.BlockSpec((1, s, ML_V), lambda bi, h: (bi, 0, EV_MV // ML_V + h)),
            pl.BlockSpec((1, 1, 4, s), lambda bi, h: (bi, h, 0, 0)),
            pl.BlockSpec((1, 4, 1), lambda bi, h: (h, 0, 0)),
            pl.BlockSpec((1, s, ML_V), lambda bi, h: (bi, 0, EV_MO // ML_V + h)),
            pl.BlockSpec((1, ML_V), lambda bi, h: (0, h)),
        ],
        out_specs=pl.BlockSpec((1, s, ML_V), lambda bi, h: (bi, 0, h)),
        out_shape=jax.ShapeDtypeStruct((b, s, ML_HEADS * ML_V), BF16),
        scratch_shapes=[
            pltpu.VMEM((s, ML_V), F32),
            pltpu.VMEM((s, ML_V), F32),
            pltpu.VMEM((ML_QK, ML_V), F32),
            pltpu.VMEM((ML_QK, ML_V), F32),
        ],
        compiler_params=_compiler_params(("parallel", "parallel"), vmem),
        name="mlstm",
    )(z3, z3, z3, gates_t, gate_bias, z3, g_mlstm.reshape(1, -1))


def _outproj_kernel(*refs, n_lhs):
    x_ref = refs[0]
    a_refs = refs[1:1 + n_lhs]
    w_refs = refs[1 + n_lhs:1 + 2 * n_lhs]
    o_ref = refs[1 + 2 * n_lhs]
    acc = x_ref[...]
    for a_ref, w_ref in zip(a_refs, w_refs):
        acc = acc + jnp.dot(a_ref[...], w_ref[...], preferred_element_type=F32)
    o_ref[...] = acc


def _outproj(x, lhs, weights):
    n, d = x.shape
    tm = Tiles.out_rows
    assert n % tm == 0
    row = lambda i: (i, 0)
    fixed = lambda i: (0, 0)
    k_total = sum(a.shape[1] for a in lhs)
    vmem = 2 * 2 * tm * d * 4 + 2 * tm * k_total * 2 + 2 * k_total * d * 2 + tm * d * 4
    return pl.pallas_call(
        functools.partial(_outproj_kernel, n_lhs=len(lhs)),
        grid=(n // tm,),
        in_specs=([pl.BlockSpec((tm, d), row)]
                  + [pl.BlockSpec((tm, a.shape[1]), row) for a in lhs]
                  + [pl.BlockSpec(w.shape, fixed) for w in weights]),
        out_specs=pl.BlockSpec((tm, d), row),
        out_shape=jax.ShapeDtypeStruct((n, d), F32),
        compiler_params=_compiler_params(("parallel",), vmem),
        name="outproj",
    )(x, *lhs, *[w.astype(BF16) for w in weights])


def _diff_attn_kernel(slope_ref, q1_ref, q2_ref, k1_ref, k2_ref, vt_ref, pq_ref, pk_ref, lam_ref, gs_ref, *rest,
                      n_chunks, tk, lam_init, n_casts):
    cast_in, o_ref, cast_out, vta_ref = rest[:n_casts], rest[n_casts], rest[n_casts + 1:-1], rest[-1]
    for src, dst in zip(cast_in, cast_out):
        dst[...] = src[...].astype(BF16)
    h = pl.program_id(1)
    slope = slope_ref[h] * LOG2_E
    q1 = q1_ref[0]
    q2 = q2_ref[0]
    tq = q1.shape[0]
    dv = 2 * DA_HEAD
    rows = vta_ref.shape[0]

    @pl.when(pl.program_id(2) == 0)
    def _():
        vta_ref[:dv, :] = vt_ref[0, 0]
        vta_ref[dv:, :] = jnp.ones((rows - dv, vta_ref.shape[1]), BF16)
    pos_q = slope * pq_ref[0].astype(F32)
    nt = (((1,), (1,)), ((), ()))

    def scores(c):
        rows_c = slice(c * tk, (c + 1) * tk)
        dist = jnp.abs(slope * pk_ref[0, rows_c, :].astype(F32) - pos_q)
        s1 = lax.dot_general(k1_ref[0, rows_c, :], q1, nt, preferred_element_type=F32) - dist
        s2 = lax.dot_general(k2_ref[0, rows_c, :], q2, nt, preferred_element_type=F32) - dist
        return s1, s2

    def softmax_step(s, vt, m, acc):
        m_new = jnp.maximum(m, jnp.max(s, axis=0, keepdims=True))
        alpha = jnp.exp2(m - m_new)
        p = jnp.exp2(s - m_new).astype(BF16)
        return m_new, alpha * acc + jnp.dot(vt, p, preferred_element_type=F32)

    m1 = m2 = jnp.full((1, tq), NEG_INIT, F32)
    acc1 = acc2 = jnp.zeros((rows, tq), F32)
    s1, s2 = scores(0)
    for c in range(n_chunks):
        nxt = scores(c + 1) if c + 1 < n_chunks else (None, None)
        vt = vta_ref[:, c * tk:(c + 1) * tk]
        m1, acc1 = softmax_step(s1, vt, m1, acc1)
        m2, acc2 = softmax_step(s2, vt, m2, acc2)
        s1, s2 = nxt

    lam_vec = lam_ref[...]
    lam = (jnp.exp(jnp.sum(lam_vec[0:1] * lam_vec[1:2], axis=1, keepdims=True))
           - jnp.exp(jnp.sum(lam_vec[2:3] * lam_vec[3:4], axis=1, keepdims=True)) + lam_init)
    w1 = 1.0 / acc1[dv:dv + 1]
    w2 = lam / acc2[dv:dv + 1]
    o_t = acc1[:dv] * w1 - acc2[:dv] * w2
    o_ref[0] = (_rms(o_t.T, gs_ref[...]) * (1.0 - lam_init)).astype(o_ref.dtype)


def _diff_attention(zq, vt, pos_col, pos_row, slopes, lam_vec, g_sub, lam_init, side_casts=()):
    b, s, _ = zq.shape
    tq, tk = Tiles.attn_q, Tiles.attn_k
    assert s % tq == 0 and s % tk == 0
    nq = s // tq
    dv = 2 * DA_HEAD
    rows = dv + BF16_SUBLANES
    k_base = DA_HEADS * 2
    views, cast_in_specs, cast_out_specs, cast_shapes = _side_cast_specs(
        side_casts, b * DA_HEADS * nq, lambda bi, h, i: (bi * DA_HEADS + h) * nq + i)
    cast_bytes = sum(3 * 2 * v.size * 2 // (b * DA_HEADS * nq) for v in views)
    vmem = (2 * 2 * tq * DA_HEAD * 2 + 3 * s * (2 * DA_HEAD + rows) * 2 + 2 * s * LANES * 4 + 2 * tq * dv * 2
            + 14 * tq * tk * 4 + cast_bytes)
    out = pl.pallas_call(
        functools.partial(_diff_attn_kernel, n_chunks=s // tk, tk=tk, lam_init=lam_init, n_casts=len(views)),
        grid=(b, DA_HEADS, nq),
        in_specs=[
            pl.BlockSpec(memory_space=pltpu.SMEM),
            pl.BlockSpec((1, tq, DA_HEAD), lambda bi, h, i: (bi, i, 2 * h)),
            pl.BlockSpec((1, tq, DA_HEAD), lambda bi, h, i: (bi, i, 2 * h + 1)),
            pl.BlockSpec((1, s, DA_HEAD), lambda bi, h, i: (bi, 0, k_base + 2 * h)),
            pl.BlockSpec((1, s, DA_HEAD), lambda bi, h, i: (bi, 0, k_base + 2 * h + 1)),
            pl.BlockSpec((1, 1, dv, s), lambda bi, h, i: (bi, h, 0, 0)),
            pl.BlockSpec((1, 1, tq), lambda bi, h, i: (bi, 0, i)),
            pl.BlockSpec((1, s, 1), lambda bi, h, i: (bi, 0, 0)),
            pl.BlockSpec((4, DA_HEAD), lambda bi, h, i: (0, 0)),
            pl.BlockSpec((1, dv), lambda bi, h, i: (0, 0)),
        ] + cast_in_specs,
        out_specs=[pl.BlockSpec((1, tq, dv), lambda bi, h, i: (bi, i, h))] + cast_out_specs,
        out_shape=[jax.ShapeDtypeStruct((b, s, DA_HEADS * dv), BF16)] + cast_shapes,
        scratch_shapes=[pltpu.VMEM((rows, s), BF16)],
        compiler_params=_compiler_params(("parallel", "parallel", "arbitrary"), vmem),
        name="diff_attn",
    )(slopes, zq, zq, zq, zq, vt, pos_row, pos_col, lam_vec, g_sub.reshape(1, dv), *views)
    return out[0], [c.reshape(w.shape) for c, w in zip(out[1:], side_casts)]


def _pack_even_w_in(w_in):
    d = w_in.shape[0]
    sizes = (MLA_Q_RANK, MLA_KV_RANK, MLA_ROPE, ML_HEADS * ML_QK, ML_HEADS * ML_QK, ML_HEADS * ML_V,
             ML_HEADS * ML_V, 4 * ML_HEADS)
    c_q, c_kv, k_r, m_q, m_k, m_v, m_o, m_g = jnp.split(w_in, np.cumsum(sizes)[:-1].tolist(), axis=1)
    pad = lambda w, width: jnp.concatenate([w, jnp.zeros((d, width - w.shape[1]), w.dtype)], axis=1)
    packed = jnp.concatenate([c_q, c_kv, m_v, m_o, m_q, m_k, pad(k_r, LANES), pad(m_g, LANES)], axis=1)
    assert packed.shape[1] == EV_COLS
    return packed


def _pack_w_uq(w_uq):
    r = w_uq.shape[0]
    w = w_uq.reshape(r, MLA_HEADS, MLA_NOPE + MLA_ROPE)
    w = jnp.concatenate([w, jnp.zeros((r, MLA_HEADS, MLA_QK_PAD - MLA_NOPE - MLA_ROPE), w.dtype)], axis=2)
    return w.reshape(r, MLA_HEADS * MLA_QK_PAD).astype(BF16)


def _split_w_ukv(w_ukv):
    r = w_ukv.shape[0]
    w = w_ukv.reshape(r, MLA_HEADS, MLA_NOPE + MLA_V)
    wk = w[:, :, :MLA_NOPE].reshape(r, MLA_HEADS * MLA_NOPE)
    wv_t = w[:, :, MLA_NOPE:].reshape(r, MLA_HEADS * MLA_V).T
    return wk.astype(BF16), wv_t.astype(BF16)


def _rope_tables(positions):
    inv_freq = ROPE_THETA ** (-jnp.arange(0, MLA_ROPE, 2, dtype=F32) / MLA_ROPE)
    ang = positions.astype(F32).reshape(-1, 1) * inv_freq
    cos, sin = jnp.cos(ang), jnp.sin(ang)
    zeros = jnp.zeros((ang.shape[0], LANES - MLA_ROPE), F32)
    return jnp.concatenate([cos, cos, zeros], axis=1), jnp.concatenate([-sin, sin, zeros], axis=1)


def _even_mixer(x, b, s, positions, norm, w_in, g_cq, w_uq, g_ckv, w_ukv, b_gates, g_mlstm, w_o, side_casts):
    n = b * s
    col_scale = jnp.ones((EV_COLS,), F32).at[EV_MK:EV_MK + ML_HEADS * ML_QK].set(ML_QK ** -0.5)
    z = _normproj(x, norm, _pack_even_w_in(w_in.astype(BF16)), col_scale)

    cos_t, sin_t = _rope_tables(positions)
    wk, wv_t = _split_w_ukv(w_ukv)
    q, k, vt = _mla_prep(z, b, s, g_cq, g_ckv, _pack_w_uq(w_uq), wk, wv_t, cos_t, sin_t)
    a_out, casted = _mla_attention(q.reshape(b, s, -1), k.reshape(b, s, -1), vt, side_casts)

    z3 = z.reshape(b, s, EV_COLS)
    gates_t = z3[:, :, EV_MG:EV_MG + 4 * ML_HEADS].reshape(b, s, 4, ML_HEADS).transpose(0, 3, 2, 1)
    gate_bias = b_gates.astype(F32).reshape(4, ML_HEADS).T.reshape(ML_HEADS, 4, 1)
    m_out = _mlstm(z3, gates_t, gate_bias, g_mlstm)

    split = MLA_HEADS * MLA_V
    return _outproj(x, [a_out.reshape(n, -1), m_out.reshape(n, -1)], [w_o[:split], w_o[split:]]), casted


def _odd_mixer(x, b, s, positions, norm, w_in, lam_q1, lam_k1, lam_q2, lam_k2, g_sub, w_o, lam_init, side_casts):
    n = b * s
    width = DA_HEADS * 2 * DA_HEAD
    zq, vt = _qkv_proj(x, b, s, norm, w_in, LOG2_E * DA_HEAD ** -0.5, DA_HEADS, 2 * DA_HEAD)
    zq = zq.reshape(b, s, 2 * width)
    slopes = jnp.asarray([2.0 ** (-8.0 * (h + 1) / DA_HEADS) for h in range(DA_HEADS)], dtype=F32)
    lam_vec = jnp.stack([lam_q1, lam_k1, lam_q2, lam_k2]).astype(F32)
    pos = positions - positions[:, :1]
    o, casted = _diff_attention(zq, vt, pos.reshape(b, s, 1), pos.reshape(b, 1, s), slopes, lam_vec, g_sub, lam_init,
                                side_casts)
    return _outproj(x, [o.reshape(n, width)], [w_o]), casted


def kernel(x, positions, l0_ffn1_norm, l0_ffn1_w_gu, l0_ffn1_w_down, l0_mix_norm, l0_w_in, l0_g_cq, l0_w_uq, l0_g_ckv, l0_w_ukv, l0_b_gates, l0_g_mlstm, l0_w_o, l0_ffn2_norm, l0_ffn2_w_gu, l0_ffn2_w_down, l1_ffn1_norm, l1_ffn1_w_gu, l1_ffn1_w_down, l1_mix_norm, l1_w_in, l1_lam_q1, l1_lam_k1, l1_lam_q2, l1_lam_k2, l1_g_sub, l1_w_o, l1_ffn2_norm, l1_ffn2_w_gu, l1_ffn2_w_down, final_norm):
    b, s, d = x.shape
    h = x.reshape(b * s, d)
    h = _ffn(h, l0_ffn1_norm, l0_ffn1_w_gu, l0_ffn1_w_down)
    h, (w2_gu, w2_down, w3_gu, w3_down) = _even_mixer(
        h, b, s, positions, l0_mix_norm, l0_w_in, l0_g_cq, l0_w_uq, l0_g_ckv, l0_w_ukv, l0_b_gates, l0_g_mlstm, l0_w_o,
        side_casts=(l0_ffn2_w_gu, l0_ffn2_w_down, l1_ffn1_w_gu, l1_ffn1_w_down))
    h = _ffn(h, l0_ffn2_norm, w2_gu, w2_down)
    h = _ffn(h, l1_ffn1_norm, w3_gu, w3_down)
    lam_init = 0.8 - 0.6 * math.exp(-0.3 * 1)
    h, (w4_gu, w4_down) = _odd_mixer(
        h, b, s, positions, l1_mix_norm, l1_w_in, l1_lam_q1, l1_lam_k1, l1_lam_q2, l1_lam_k2, l1_g_sub, l1_w_o, lam_init,
        side_casts=(l1_ffn2_w_gu, l1_ffn2_w_down))
    h = _ffn(h, l1_ffn2_norm, w4_gu, w4_down, final_gain=final_norm)
    return h.reshape(b, s, d)
```

```python
import functools
import math

import jax
import jax.numpy as jnp
import numpy as np
from jax import lax
from jax.experimental import pallas as pl
from jax.experimental.pallas import tpu as pltpu

F32 = jnp.float32
BF16 = jnp.bfloat16

D_MODEL = 2048
MLA_HEADS = 8
MLA_Q_RANK = 512
MLA_KV_RANK = 512
MLA_NOPE = 128
MLA_ROPE = 64
MLA_V = 128
ROPE_THETA = 10000.0
ML_HEADS = 4
ML_QK = 128
ML_V = 256
DA_HEADS = 8
DA_HEAD = 128
D_FF = 5632
EPS = 1e-6
NEG_INIT = -1e30
LOG2_E = math.log2(math.e)

LANES = 128
BF16_SUBLANES = 16
MLA_QK_PAD = 256
V7X_VMEM_BYTES = 64 * 1024 * 1024
VMEM_REQUEST_CAP_BYTES = V7X_VMEM_BYTES - 6 * 1024 * 1024

EV_CQ = 0
EV_CKV = EV_CQ + MLA_Q_RANK
EV_MV = EV_CKV + MLA_KV_RANK
EV_MO = EV_MV + ML_HEADS * ML_V
EV_MQ = EV_MO + ML_HEADS * ML_V
EV_MK = EV_MQ + ML_HEADS * ML_QK
EV_KR = EV_MK + ML_HEADS * ML_QK
EV_MG = EV_KR + LANES
EV_COLS = EV_MG + LANES
assert EV_MV % ML_V == 0 and EV_MO % ML_V == 0 and EV_MQ % ML_QK == 0 and EV_MK % ML_QK == 0


class Tiles:
    ffn_rows = 1024
    ffn_cols = 512
    proj_rows = 512
    proj_col_steps_even = 2
    prep_rows = 512
    out_rows = 512
    mla_q = 2048
    mla_k = 512
    attn_q = 1024
    attn_k = 512
    mlstm_chunk = 256
    mlstm_post_rows = 512


def _compiler_params(semantics, vmem_bytes):
    return pltpu.CompilerParams(dimension_semantics=semantics,
                                vmem_limit_bytes=int(min(vmem_bytes, VMEM_REQUEST_CAP_BYTES)))


def _cast_blocking(shape, steps):
    rows, cols = shape
    for col_blocks in range(1, cols // LANES + 1):
        if cols % col_blocks or (cols // col_blocks) % LANES or steps % col_blocks:
            continue
        row_blocks = steps // col_blocks
        if rows % row_blocks == 0 and (rows // row_blocks) % BF16_SUBLANES == 0:
            return rows // row_blocks, cols // col_blocks
    raise ValueError(f"{shape} does not tile into {steps} aligned blocks")


def _side_cast_specs(side_casts, steps, step_index):
    specs = []
    for w in side_casts:
        rb, cb = _cast_blocking(w.shape, steps)
        col_blocks = w.shape[1] // cb
        specs.append(pl.BlockSpec(
            (rb, cb), lambda *idx, n=col_blocks: (step_index(*idx) // n, step_index(*idx) % n)))
    return specs, list(specs), [jax.ShapeDtypeStruct(w.shape, BF16) for w in side_casts]


def _rms(x, gain):
    y = x * lax.rsqrt(jnp.mean(x * x, axis=-1, keepdims=True) + EPS)
    return y * gain


def _ffn_kernel(*refs, n_ff_steps, final):
    if final:
        x_ref, g_ref, wg_ref, wu_ref, wd_ref, gf_ref, o_ref, xn_ref = refs
    else:
        x_ref, g_ref, wg_ref, wu_ref, wd_ref, o_ref, xn_ref = refs
    j = pl.program_id(1)

    def branch(xn):
        gate = jnp.dot(xn, wg_ref[...], preferred_element_type=F32)
        up = jnp.dot(xn, wu_ref[...], preferred_element_type=F32)
        act = (0.5 * (gate * jax.nn.sigmoid(gate)) * up).astype(BF16)
        return jnp.dot(act, wd_ref[...], preferred_element_type=F32)

    @pl.when(j == 0)
    def _():
        x = x_ref[...]
        xn = _rms(x, g_ref[...]).astype(BF16)
        xn_ref[...] = xn
        o_ref[...] = x + branch(xn)

    last = n_ff_steps - 1

    @pl.when((j > 0) & (j < last) if final else j > 0)
    def _():
        o_ref[...] += branch(xn_ref[...])

    if final:
        @pl.when(j == last)
        def _():
            o_ref[...] = _rms(o_ref[...] + branch(xn_ref[...]), gf_ref[...])


def _ffn(x, gain, w_gu, w_down, final_gain=None):
    n, d = x.shape
    tm, tf = Tiles.ffn_rows, Tiles.ffn_cols
    assert n % tm == 0 and D_FF % tf == 0 and D_FF // tf >= 2 and w_gu.shape == (d, 2 * D_FF)
    nf = D_FF // tf
    wgu = w_gu.astype(BF16)
    wd = w_down.astype(BF16)
    final = final_gain is not None
    in_specs = [
        pl.BlockSpec((tm, d), lambda i, j: (i, 0)),
        pl.BlockSpec((1, d), lambda i, j: (0, 0)),
        pl.BlockSpec((d, tf), lambda i, j: (0, j)),
        pl.BlockSpec((d, tf), lambda i, j: (0, j + nf)),
        pl.BlockSpec((tf, d), lambda i, j: (j, 0)),
    ]
    args = [x, gain.reshape(1, d), wgu, wgu, wd]
    if final:
        in_specs.append(pl.BlockSpec((1, d), lambda i, j: (0, 0)))
        args.append(final_gain.reshape(1, d))
    vmem = (2 * 2 * tm * d * 4 + tm * d * 2 + 2 * 3 * d * tf * 2 + 4 * tm * tf * 4 + tm * d * 4)
    return pl.pallas_call(
        functools.partial(_ffn_kernel, n_ff_steps=nf, final=final),
        grid=(n // tm, nf),
        in_specs=in_specs,
        out_specs=pl.BlockSpec((tm, d), lambda i, j: (i, 0)),
        out_shape=jax.ShapeDtypeStruct((n, d), F32),
        scratch_shapes=[pltpu.VMEM((tm, d), BF16)],
        compiler_params=_compiler_params(("parallel", "arbitrary"), vmem),
        name="ffn",
    )(*args)


def _normproj_kernel(x_ref, g_ref, w_ref, cs_ref, o_ref, xn_ref):
    tn = o_ref.shape[1]

    def project(xn):
        col = pl.multiple_of(pl.program_id(1) * tn, LANES)
        acc = jnp.dot(xn, w_ref[:, pl.ds(col, tn)], preferred_element_type=F32)
        o_ref[...] = (acc * cs_ref[...]).astype(o_ref.dtype)

    @pl.when(pl.program_id(1) == 0)
    def _():
        xn = _rms(x_ref[...], g_ref[...]).astype(BF16)
        xn_ref[...] = xn
        project(xn)

    @pl.when(pl.program_id(1) > 0)
    def _():
        project(xn_ref[...])


def _normproj(x, gain, w, col_scale):
    n, d = x.shape
    cols = w.shape[1]
    col_steps = Tiles.proj_col_steps_even
    tm, tn = Tiles.proj_rows, cols // col_steps
    assert n % tm == 0 and tn * col_steps == cols and tn % LANES == 0
    out_dtype = F32
    vmem = 2 * tm * d * 4 + tm * d * 2 + d * cols * 2 + 2 * tm * tn * 4 + 2 * tm * tn * 4 + (4 << 20)
    return pl.pallas_call(
        _normproj_kernel,
        grid=(n // tm, col_steps),
        in_specs=[
            pl.BlockSpec((tm, d), lambda i, j: (i, 0)),
            pl.BlockSpec((1, d), lambda i, j: (0, 0)),
            pl.BlockSpec((d, cols), lambda i, j: (0, 0), pipeline_mode=pl.Buffered(1)),
            pl.BlockSpec((1, tn), lambda i, j: (0, j)),
        ],
        out_specs=pl.BlockSpec((tm, tn), lambda i, j: (i, j)),
        out_shape=jax.ShapeDtypeStruct((n, cols), out_dtype),
        scratch_shapes=[pltpu.VMEM((tm, d), BF16)],
        compiler_params=_compiler_params(("parallel", "arbitrary"), vmem),
        name="normproj",
    )(x, gain.reshape(1, d), w.astype(BF16), col_scale.reshape(1, cols))


def _qkv_proj_kernel(x_ref, g_ref, w_ref, o_ref, vt_ref, xn_ref, *, q_scale, heads, head_width):
    j = pl.program_id(1)

    @pl.when(j == 0)
    def _():
        xn_ref[...] = _rms(x_ref[...], g_ref[...]).astype(BF16)

    width = heads * head_width

    def project(part):
        return jnp.dot(xn_ref[...], w_ref[:, part * width:(part + 1) * width], preferred_element_type=F32)

    @pl.when(j == 0)
    def _():
        o_ref[...] = (project(0) * q_scale).astype(o_ref.dtype)

    @pl.when(j == 1)
    def _():
        o_ref[...] = project(1).astype(o_ref.dtype)

    @pl.when(j == 2)
    def _():
        acc = project(2)
        for h in range(heads):
            vt_ref[0, h] = acc[:, h * head_width:(h + 1) * head_width].T.astype(vt_ref.dtype)


def _qkv_proj(x, b, s, gain, w, q_scale, heads, head_width):
    n, d = x.shape
    width = heads * head_width
    tm = Tiles.proj_rows
    assert w.shape == (d, 3 * width) and s % tm == 0 and n == b * s
    tiles_per_seq = s // tm
    vmem = 2 * tm * d * 4 + tm * d * 2 + 3 * d * width * 2 + 4 * tm * width * 2 + 3 * tm * width * 4
    return pl.pallas_call(
        functools.partial(_qkv_proj_kernel, q_scale=q_scale, heads=heads, head_width=head_width),
        grid=(n // tm, 3),
        in_specs=[
            pl.BlockSpec((tm, d), lambda i, j: (i, 0)),
            pl.BlockSpec((1, d), lambda i, j: (0, 0)),
            pl.BlockSpec((d, 3 * width), lambda i, j: (0, 0), pipeline_mode=pl.Buffered(1)),
        ],
        out_specs=[
            pl.BlockSpec((tm, width), lambda i, j: (i, jnp.minimum(j, 1))),
            pl.BlockSpec((1, heads, head_width, tm), lambda i, j: (i // tiles_per_seq, 0, 0, i % tiles_per_seq)),
        ],
        out_shape=[
            jax.ShapeDtypeStruct((n, 2 * width), BF16),
            jax.ShapeDtypeStruct((b, heads, head_width, s), BF16),
        ],
        scratch_shapes=[pltpu.VMEM((tm, d), BF16)],
        compiler_params=_compiler_params(("parallel", "arbitrary"), vmem),
        name="qkv_proj",
    )(x, gain.reshape(1, d), w.astype(BF16))


def _rope_tile(x, cos_t, sin_t):
    lane = lax.broadcasted_iota(jnp.int32, x.shape, 1)
    half = MLA_ROPE // 2
    partner = jnp.where(lane % MLA_ROPE < half, pltpu.roll(x, LANES - half, 1), pltpu.roll(x, half, 1))
    return x * cos_t + partner * sin_t


def _mla_prep_kernel(cq_ref, ckv_ref, kr_ref, gq_ref, gkv_ref, wq_ref, wk_ref, wvt_ref, cos_ref, sin_ref,
                     q_ref, k_ref, vt_ref):
    cos_t = cos_ref[...]
    sin_t = sin_ref[...]
    cqn = _rms(cq_ref[...], gq_ref[...]).astype(BF16)
    ckvn_f32 = _rms(ckv_ref[...], gkv_ref[...])
    ckvn = ckvn_f32.astype(BF16)
    scale = LOG2_E * (MLA_NOPE + MLA_ROPE) ** -0.5
    q = jnp.dot(cqn, wq_ref[...], preferred_element_type=F32)
    kn = jnp.dot(ckvn, wk_ref[...], preferred_element_type=F32)
    v_t = jnp.dot(wvt_ref[...], ckvn_f32.T.astype(BF16), preferred_element_type=F32)
    ones = jnp.ones((BF16_SUBLANES, v_t.shape[1]), BF16)
    lane = lax.broadcasted_iota(jnp.int32, cos_t.shape, 1)
    k_rope = jnp.where(lane < MLA_ROPE, _rope_tile(kr_ref[...], cos_t, sin_t), 0.0).astype(BF16)
    for h in range(MLA_HEADS):
        base = h * MLA_QK_PAD
        q_ref[:, base:base + MLA_NOPE] = (q[:, base:base + MLA_NOPE] * scale).astype(BF16)
        q_rope = _rope_tile(q[:, base + MLA_NOPE:base + MLA_QK_PAD], cos_t, sin_t)
        q_ref[:, base + MLA_NOPE:base + MLA_QK_PAD] = (q_rope * scale).astype(BF16)
        k_ref[:, base:base + MLA_NOPE] = kn[:, h * MLA_NOPE:(h + 1) * MLA_NOPE].astype(BF16)
        k_ref[:, base + MLA_NOPE:base + MLA_QK_PAD] = k_rope
        vt_ref[0, h, :MLA_V, :] = v_t[h * MLA_V:(h + 1) * MLA_V].astype(BF16)
        vt_ref[0, h, MLA_V:, :] = ones


def _mla_prep(z, b, s, g_cq, g_ckv, wq, wk, wv_t, cos_t, sin_t):
    n = z.shape[0]
    tm = Tiles.prep_rows
    assert s % tm == 0 and n == b * s
    tiles_per_seq = s // tm
    qk_cols = MLA_HEADS * MLA_QK_PAD
    v_cols = MLA_HEADS * MLA_V
    vt_rows = MLA_V + BF16_SUBLANES
    row = lambda i: (i, 0)
    fixed = lambda i: (0, 0)
    vmem = (2 * tm * (MLA_Q_RANK + MLA_KV_RANK + 3 * LANES) * 4 + 2 * MLA_Q_RANK * (qk_cols + 2 * v_cols) * 2
            + 2 * tm * (2 * qk_cols + v_cols) * 2 + 3 * tm * qk_cols * 4)
    return pl.pallas_call(
        _mla_prep_kernel,
        grid=(n // tm,),
        in_specs=[
            pl.BlockSpec((tm, MLA_Q_RANK), lambda i: (i, EV_CQ // MLA_Q_RANK)),
            pl.BlockSpec((tm, MLA_KV_RANK), lambda i: (i, EV_CKV // MLA_KV_RANK)),
            pl.BlockSpec((tm, LANES), lambda i: (i, EV_KR // LANES)),
            pl.BlockSpec((1, MLA_Q_RANK), fixed),
            pl.BlockSpec((1, MLA_KV_RANK), fixed),
            pl.BlockSpec((MLA_Q_RANK, qk_cols), fixed),
            pl.BlockSpec((MLA_KV_RANK, v_cols), fixed),
            pl.BlockSpec((v_cols, MLA_KV_RANK), fixed),
            pl.BlockSpec((tm, LANES), row),
            pl.BlockSpec((tm, LANES), row),
        ],
        out_specs=[
            pl.BlockSpec((tm, qk_cols), row),
            pl.BlockSpec((tm, qk_cols), row),
            pl.BlockSpec((1, MLA_HEADS, vt_rows, tm), lambda i: (i // tiles_per_seq, 0, 0, i % tiles_per_seq)),
        ],
        out_shape=[
            jax.ShapeDtypeStruct((n, qk_cols), BF16),
            jax.ShapeDtypeStruct((n, qk_cols), BF16),
            jax.ShapeDtypeStruct((b, MLA_HEADS, vt_rows, s), BF16),
        ],
        compiler_params=_compiler_params(("parallel",), vmem),
        name="mla_prep",
    )(z, z, z, g_cq.reshape(1, -1), g_ckv.reshape(1, -1), wq, wk, wv_t, cos_t, sin_t)


def _mla_attn_kernel(q_ref, k_ref, vt_ref, *rest, n_chunks, tk, n_casts):
    cast_in, o_ref, cast_out = rest[:n_casts], rest[n_casts], rest[n_casts + 1:]
    for src, dst in zip(cast_in, cast_out):
        dst[...] = src[...].astype(BF16)
    q = q_ref[0]
    tq = q.shape[0]
    rows = vt_ref.shape[2]

    def scores(c):
        k = k_ref[0, c * tk:(c + 1) * tk, :]
        return lax.dot_general(k, q, (((1,), (1,)), ((), ())), preferred_element_type=F32)

    m = jnp.full((1, tq), NEG_INIT, F32)
    acc = jnp.zeros((rows, tq), F32)
    s = scores(0)
    for c in range(n_chunks):
        s_next = scores(c + 1) if c + 1 < n_chunks else None
        m_new = jnp.maximum(m, jnp.max(s, axis=0, keepdims=True))
        alpha = jnp.exp2(m - m_new)
        p = jnp.exp2(s - m_new).astype(BF16)
        acc = alpha * acc + jnp.dot(vt_ref[0, 0, :, c * tk:(c + 1) * tk], p, preferred_element_type=F32)
        m, s = m_new, s_next
    o_t = acc[:MLA_V] * (1.0 / acc[MLA_V:MLA_V + 1])
    o_ref[0] = o_t.T.astype(o_ref.dtype)


def _mla_attention(q, k, vt, side_casts=()):
    b, s, _ = q.shape
    tq, tk = Tiles.mla_q, Tiles.mla_k
    rows = vt.shape[2]
    assert s % tq == 0 and s % tk == 0
    nq = s // tq
    cast_in_specs, cast_out_specs, cast_shapes = _side_cast_specs(
        side_casts, b * MLA_HEADS * nq, lambda bi, h, i: (bi * MLA_HEADS + h) * nq + i)
    cast_bytes = sum(3 * 2 * w.size * 2 // (b * MLA_HEADS * nq) for w in side_casts)
    vmem = (2 * tq * MLA_QK_PAD * 2 + 2 * s * (MLA_QK_PAD + rows) * 2 + 2 * tq * MLA_V * 2
            + 6 * tq * tk * 4 + cast_bytes)
    out = pl.pallas_call(
        functools.partial(_mla_attn_kernel, n_chunks=s // tk, tk=tk, n_casts=len(side_casts)),
        grid=(b, MLA_HEADS, nq),
        in_specs=[
            pl.BlockSpec((1, tq, MLA_QK_PAD), lambda bi, h, i: (bi, i, h)),
            pl.BlockSpec((1, s, MLA_QK_PAD), lambda bi, h, i: (bi, 0, h)),
            pl.BlockSpec((1, 1, rows, s), lambda bi, h, i: (bi, h, 0, 0)),
        ] + cast_in_specs,
        out_specs=[pl.BlockSpec((1, tq, MLA_V), lambda bi, h, i: (bi, i, h))] + cast_out_specs,
        out_shape=[jax.ShapeDtypeStruct((b, s, MLA_HEADS * MLA_V), BF16)] + cast_shapes,
        compiler_params=_compiler_params(("parallel", "parallel", "arbitrary"), vmem),
        name="mla_attn",
    )(q, k, vt, *side_casts)
    return out[0], out[1:]


def _log_sigmoid(x):
    return jnp.minimum(x, 0.0) - jnp.log(1.0 + jnp.exp(-jnp.abs(x)))


def _mlstm_masks(chunk):
    t_idx = lax.broadcasted_iota(jnp.int32, (chunk, chunk), 0)
    s_idx = lax.broadcasted_iota(jnp.int32, (chunk, chunk), 1)
    lower, upper = s_idx <= t_idx, s_idx >= t_idx
    one_hot = lambda m: jnp.where(m, 1.0, 0.0).astype(F32)
    blocked = lambda m: jnp.where(m, 0.0, NEG_INIT).astype(F32)
    return {"eye": one_hot(s_idx == t_idx), "lower": one_hot(lower), "upper": one_hot(upper),
            "neg_lower": blocked(lower), "neg_upper": blocked(upper)}


def _mlstm_chunk(q_ref, k_ref, v_ref, gates, masks, c_ref, n_row, m_st, start, chunk, li_idx, lf_idx, reverse):
    L = chunk
    q = q_ref[0, pl.ds(start, L), :]
    k = k_ref[0, pl.ds(start, L), :]
    v = v_ref[0, pl.ds(start, L), :].astype(BF16)
    li_row = gates[li_idx:li_idx + 1, :]
    lf_row = _log_sigmoid(gates[lf_idx:lf_idx + 1, :])
    li_col = jnp.sum(masks["eye"] * li_row, axis=1, keepdims=True)
    lf_col = jnp.sum(masks["eye"] * lf_row, axis=1, keepdims=True)
    visible, visible_t, blocked = ((masks["upper"], masks["lower"], masks["neg_upper"]) if reverse
                                   else (masks["lower"], masks["upper"], masks["neg_lower"]))
    a_col = jnp.sum(visible * lf_row, axis=1, keepdims=True)
    a_row = jnp.sum(visible_t * lf_col, axis=0, keepdims=True)
    g = jnp.sum(lf_row, axis=1, keepdims=True)

    dmat = (a_col - a_row) + li_row + blocked
    inter = a_col + m_st
    m = jnp.maximum(inter, jnp.max(dmat, axis=1, keepdims=True))
    w_inter = jnp.exp(inter - m)
    q16 = q.astype(BF16)
    qk = (lax.dot_general(q16, k.astype(BF16), (((1,), (1,)), ((), ())), preferred_element_type=F32)
          * jnp.exp(dmat - m))
    num = (w_inter * jnp.dot(q16, c_ref[...].astype(BF16), preferred_element_type=F32)
           + jnp.dot(qk.astype(BF16), v, preferred_element_type=F32))
    den = w_inter * jnp.sum(q * n_row, axis=1, keepdims=True) + jnp.sum(qk, axis=1, keepdims=True)
    h_out = num / jnp.maximum(jnp.abs(den), jnp.exp(-m))

    r_col = g - a_col + li_col
    m_new = jnp.maximum(g + m_st, jnp.max(r_col, axis=0, keepdims=True))
    w_old = jnp.exp(g + m_st - m_new)
    kw = k * jnp.exp(r_col - m_new)
    c_ref[...] = w_old * c_ref[...] + jnp.dot(kw.T.astype(BF16), v, preferred_element_type=F32)
    n_new = w_old * n_row + jnp.sum(kw, axis=0, keepdims=True)
    return h_out, n_new, m_new


def _mlstm_kernel(q_ref, k_ref, v_ref, gate_ref, bias_ref, mo_ref, gn_ref, o_ref,
                  hf_ref, hb_ref, cf_ref, cb_ref, *, seq, chunk, post_rows):
    nc = seq // chunk
    cf_ref[...] = jnp.zeros_like(cf_ref)
    cb_ref[...] = jnp.zeros_like(cb_ref)
    bias = bias_ref[0]
    masks = _mlstm_masks(chunk)

    def body(i, carry):
        n_f, m_f, n_b, m_b = carry
        sf = pl.multiple_of(i * chunk, chunk)
        sb = pl.multiple_of((nc - 1 - i) * chunk, chunk)
        h_f, n_f, m_f = _mlstm_chunk(q_ref, k_ref, v_ref, gate_ref[0, 0, :, pl.ds(sf, chunk)] + bias, masks,
                                     cf_ref, n_f, m_f, sf, chunk, 0, 1, False)
        hf_ref[pl.ds(sf, chunk), :] = h_f
        h_b, n_b, m_b = _mlstm_chunk(q_ref, k_ref, v_ref, gate_ref[0, 0, :, pl.ds(sb, chunk)] + bias, masks,
                                     cb_ref, n_b, m_b, sb, chunk, 2, 3, True)
        hb_ref[pl.ds(sb, chunk), :] = h_b
        return n_f, m_f, n_b, m_b

    n0 = jnp.zeros((1, ML_QK), F32)
    m0 = jnp.full((1, 1), NEG_INIT, F32)
    lax.fori_loop(0, nc, body, (n0, m0, n0, m0), unroll=2)

    gn = gn_ref[...]

    def post(i, carry):
        r = pl.multiple_of(i * post_rows, post_rows)
        hm = hf_ref[pl.ds(r, post_rows), :] + hb_ref[pl.ds(r, post_rows), :]
        y = _rms(hm, gn)
        o_ref[0, pl.ds(r, post_rows), :] = (y * jax.nn.sigmoid(mo_ref[0, pl.ds(r, post_rows), :])).astype(o_ref.dtype)
        return carry

    lax.fori_loop(0, seq // post_rows, post, 0)


def _mlstm(z3, gates_t, gate_bias, g_mlstm):
    b, s, _ = z3.shape
    chunk = Tiles.mlstm_chunk
    assert s % (2 * chunk) == 0 and s % Tiles.mlstm_post_rows == 0
    vmem = (2 * s * (2 * ML_QK + 2 * ML_V + LANES) * 4 + 2 * s * ML_V * 2 + 2 * s * ML_V * 4
            + 2 * ML_QK * ML_V * 4 + 24 * chunk * chunk * 4 + (2 << 20))
    return pl.pallas_call(
        functools.partial(_mlstm_kernel, seq=s, chunk=chunk, post_rows=Tiles.mlstm_post_rows),
        grid=(b, ML_HEADS),
        in_specs=[
            pl.BlockSpec((1, s, ML_QK), lambda bi, h: (bi, 0, EV_MQ // ML_QK + h)),
            pl.BlockSpec((1, s, ML_QK), lambda bi, h: (bi, 0, EV_MK // ML_QK + h)),
            pl.BlockSpec((1, s, ML_V), lambda bi, h: (bi, 0, EV_MV // ML_V + h)),
            pl.BlockSpec((1, 1, 4, s), lambda bi, h: (bi, h, 0, 0)),
            pl.BlockSpec((1, 4, 1), lambda bi, h: (h, 0, 0)),
            pl.BlockSpec((1, s, ML_V), lambda bi, h: (bi, 0, EV_MO // ML_V + h)),
            pl.BlockSpec((1, ML_V), lambda bi, h: (0, h)),
        ],
        out_specs=pl.BlockSpec((1, s, ML_V), lambda bi, h: (bi, 0, h)),
        out_shape=jax.ShapeDtypeStruct((b, s, ML_HEADS * ML_V), BF16),
        scratch_shapes=[
            pltpu.VMEM((s, ML_V), F32),
            pltpu.VMEM((s, ML_V), F32),
            pltpu.VMEM((ML_QK, ML_V), F32),
            pltpu.VMEM((ML_QK, ML_V), F32),
        ],
        compiler_params=_compiler_params(("parallel", "parallel"), vmem),
        name="mlstm",
    )(z3, z3, z3, gates_t, gate_bias, z3, g_mlstm.reshape(1, -1))


def _outproj_kernel(*refs, n_lhs):
    x_ref = refs[0]
    a_refs = refs[1:1 + n_lhs]
    w_refs = refs[1 + n_lhs:1 + 2 * n_lhs]
    o_ref = refs[1 + 2 * n_lhs]
    acc = x_ref[...]
    for a_ref, w_ref in zip(a_refs, w_refs):
        acc = acc + jnp.dot(a_ref[...], w_ref[...], preferred_element_type=F32)
    o_ref[...] = acc


def _outproj(x, lhs, weights):
    n, d = x.shape
    tm = Tiles.out_rows
    assert n % tm == 0
    row = lambda i: (i, 0)
    fixed = lambda i: (0, 0)
    k_total = sum(a.shape[1] for a in lhs)
    vmem = 2 * 2 * tm * d * 4 + 2 * tm * k_total * 2 + 2 * k_total * d * 2 + tm * d * 4
    return pl.pallas_call(
        functools.partial(_outproj_kernel, n_lhs=len(lhs)),
        grid=(n // tm,),
        in_specs=([pl.BlockSpec((tm, d), row)]
                  + [pl.BlockSpec((tm, a.shape[1]), row) for a in lhs]
                  + [pl.BlockSpec(w.shape, fixed) for w in weights]),
        out_specs=pl.BlockSpec((tm, d), row),
        out_shape=jax.ShapeDtypeStruct((n, d), F32),
        compiler_params=_compiler_params(("parallel",), vmem),
        name="outproj",
    )(x, *lhs, *[w.astype(BF16) for w in weights])


def _diff_attn_kernel(slope_ref, q1_ref, q2_ref, k1_ref, k2_ref, vt_ref, pq_ref, pk_ref, lam_ref, gs_ref, *rest,
                      n_chunks, tk, lam_init, n_casts):
    cast_in, o_ref, cast_out, vta_ref = rest[:n_casts], rest[n_casts], rest[n_casts + 1:-1], rest[-1]
    for src, dst in zip(cast_in, cast_out):
        dst[...] = src[...].astype(BF16)
    h = pl.program_id(1)
    slope = slope_ref[h] * LOG2_E
    q1 = q1_ref[0]
    q2 = q2_ref[0]
    tq = q1.shape[0]
    dv = 2 * DA_HEAD
    rows = vta_ref.shape[0]

    @pl.when(pl.program_id(2) == 0)
    def _():
        vta_ref[:dv, :] = vt_ref[0, 0]
        vta_ref[dv:, :] = jnp.ones((rows - dv, vta_ref.shape[1]), BF16)
    pos_q = slope * pq_ref[0].astype(F32)
    nt = (((1,), (1,)), ((), ()))

    def scores(c):
        rows_c = slice(c * tk, (c + 1) * tk)
        dist = jnp.abs(slope * pk_ref[0, rows_c, :].astype(F32) - pos_q)
        s1 = lax.dot_general(k1_ref[0, rows_c, :], q1, nt, preferred_element_type=F32) - dist
        s2 = lax.dot_general(k2_ref[0, rows_c, :], q2, nt, preferred_element_type=F32) - dist
        return s1, s2

    def softmax_step(s, vt, m, acc):
        m_new = jnp.maximum(m, jnp.max(s, axis=0, keepdims=True))
        alpha = jnp.exp2(m - m_new)
        p = jnp.exp2(s - m_new).astype(BF16)
        return m_new, alpha * acc + jnp.dot(vt, p, preferred_element_type=F32)

    m1 = m2 = jnp.full((1, tq), NEG_INIT, F32)
    acc1 = acc2 = jnp.zeros((rows, tq), F32)
    s1, s2 = scores(0)
    for c in range(n_chunks):
        nxt = scores(c + 1) if c + 1 < n_chunks else (None, None)
        vt = vta_ref[:, c * tk:(c + 1) * tk]
        m1, acc1 = softmax_step(s1, vt, m1, acc1)
        m2, acc2 = softmax_step(s2, vt, m2, acc2)
        s1, s2 = nxt

    lam_vec = lam_ref[...]
    lam = (jnp.exp(jnp.sum(lam_vec[0:1] * lam_vec[1:2], axis=1, keepdims=True))
           - jnp.exp(jnp.sum(lam_vec[2:3] * lam_vec[3:4], axis=1, keepdims=True)) + lam_init)
    w1 = 1.0 / acc1[dv:dv + 1]
    w2 = lam / acc2[dv:dv + 1]
    o_t = acc1[:dv] * w1 - acc2[:dv] * w2
    o_ref[0] = (_rms(o_t.T, gs_ref[...]) * (1.0 - lam_init)).astype(o_ref.dtype)


def _diff_attention(zq, vt, pos_col, pos_row, slopes, lam_vec, g_sub, lam_init, side_casts=()):
    b, s, _ = zq.shape
    tq, tk = Tiles.attn_q, Tiles.attn_k
    assert s % tq == 0 and s % tk == 0
    nq = s // tq
    dv = 2 * DA_HEAD
    rows = dv + BF16_SUBLANES
    k_base = DA_HEADS * 2
    cast_in_specs, cast_out_specs, cast_shapes = _side_cast_specs(
        side_casts, b * DA_HEADS * nq, lambda bi, h, i: (bi * DA_HEADS + h) * nq + i)
    cast_bytes = sum(3 * 2 * w.size * 2 // (b * DA_HEADS * nq) for w in side_casts)
    vmem = (2 * 2 * tq * DA_HEAD * 2 + 3 * s * (2 * DA_HEAD + rows) * 2 + 2 * s * LANES * 4 + 2 * tq * dv * 2
            + 14 * tq * tk * 4 + cast_bytes)
    out = pl.pallas_call(
        functools.partial(_diff_attn_kernel, n_chunks=s // tk, tk=tk, lam_init=lam_init, n_casts=len(side_casts)),
        grid=(b, DA_HEADS, nq),
        in_specs=[
            pl.BlockSpec(memory_space=pltpu.SMEM),
            pl.BlockSpec((1, tq, DA_HEAD), lambda bi, h, i: (bi, i, 2 * h)),
            pl.BlockSpec((1, tq, DA_HEAD), lambda bi, h, i: (bi, i, 2 * h + 1)),
            pl.BlockSpec((1, s, DA_HEAD), lambda bi, h, i: (bi, 0, k_base + 2 * h)),
            pl.BlockSpec((1, s, DA_HEAD), lambda bi, h, i: (bi, 0, k_base + 2 * h + 1)),
            pl.BlockSpec((1, 1, dv, s), lambda bi, h, i: (bi, h, 0, 0)),
            pl.BlockSpec((1, 1, tq), lambda bi, h, i: (bi, 0, i)),
            pl.BlockSpec((1, s, 1), lambda bi, h, i: (bi, 0, 0)),
            pl.BlockSpec((4, DA_HEAD), lambda bi, h, i: (0, 0)),
            pl.BlockSpec((1, dv), lambda bi, h, i: (0, 0)),
        ] + cast_in_specs,
        out_specs=[pl.BlockSpec((1, tq, dv), lambda bi, h, i: (bi, i, h))] + cast_out_specs,
        out_shape=[jax.ShapeDtypeStruct((b, s, DA_HEADS * dv), BF16)] + cast_shapes,
        scratch_shapes=[pltpu.VMEM((rows, s), BF16)],
        compiler_params=_compiler_params(("parallel", "parallel", "arbitrary"), vmem),
        name="diff_attn",
    )(slopes, zq, zq, zq, zq, vt, pos_row, pos_col, lam_vec, g_sub.reshape(1, dv), *side_casts)
    return out[0], out[1:]


def _pack_even_w_in(w_in):
    d = w_in.shape[0]
    sizes = (MLA_Q_RANK, MLA_KV_RANK, MLA_ROPE, ML_HEADS * ML_QK, ML_HEADS * ML_QK, ML_HEADS * ML_V,
             ML_HEADS * ML_V, 4 * ML_HEADS)
    c_q, c_kv, k_r, m_q, m_k, m_v, m_o, m_g = jnp.split(w_in, np.cumsum(sizes)[:-1].tolist(), axis=1)
    pad = lambda w, width: jnp.concatenate([w, jnp.zeros((d, width - w.shape[1]), w.dtype)], axis=1)
    packed = jnp.concatenate([c_q, c_kv, m_v, m_o, m_q, m_k, pad(k_r, LANES), pad(m_g, LANES)], axis=1)
    assert packed.shape[1] == EV_COLS
    return packed


def _pack_w_uq(w_uq):
    r = w_uq.shape[0]
    w = w_uq.reshape(r, MLA_HEADS, MLA_NOPE + MLA_ROPE)
    w = jnp.concatenate([w, jnp.zeros((r, MLA_HEADS, MLA_QK_PAD - MLA_NOPE - MLA_ROPE), w.dtype)], axis=2)
    return w.reshape(r, MLA_HEADS * MLA_QK_PAD).astype(BF16)


def _split_w_ukv(w_ukv):
    r = w_ukv.shape[0]
    w = w_ukv.reshape(r, MLA_HEADS, MLA_NOPE + MLA_V)
    wk = w[:, :, :MLA_NOPE].reshape(r, MLA_HEADS * MLA_NOPE)
    wv_t = w[:, :, MLA_NOPE:].reshape(r, MLA_HEADS * MLA_V).T
    return wk.astype(BF16), wv_t.astype(BF16)


def _rope_tables(positions):
    inv_freq = ROPE_THETA ** (-jnp.arange(0, MLA_ROPE, 2, dtype=F32) / MLA_ROPE)
    ang = positions.astype(F32).reshape(-1, 1) * inv_freq
    cos, sin = jnp.cos(ang), jnp.sin(ang)
    zeros = jnp.zeros((ang.shape[0], LANES - MLA_ROPE), F32)
    return jnp.concatenate([cos, cos, zeros], axis=1), jnp.concatenate([-sin, sin, zeros], axis=1)


def _even_mixer(x, b, s, positions, norm, w_in, g_cq, w_uq, g_ckv, w_ukv, b_gates, g_mlstm, w_o, side_casts):
    n = b * s
    col_scale = jnp.ones((EV_COLS,), F32).at[EV_MK:EV_MK + ML_HEADS * ML_QK].set(ML_QK ** -0.5)
    z = _normproj(x, norm, _pack_even_w_in(w_in.astype(BF16)), col_scale)

    cos_t, sin_t = _rope_tables(positions)
    wk, wv_t = _split_w_ukv(w_ukv)
    q, k, vt = _mla_prep(z, b, s, g_cq, g_ckv, _pack_w_uq(w_uq), wk, wv_t, cos_t, sin_t)
    a_out, casted = _mla_attention(q.reshape(b, s, -1), k.reshape(b, s, -1), vt, side_casts)

    z3 = z.reshape(b, s, EV_COLS)
    gates_t = z3[:, :, EV_MG:EV_MG + 4 * ML_HEADS].reshape(b, s, 4, ML_HEADS).transpose(0, 3, 2, 1)
    gate_bias = b_gates.astype(F32).reshape(4, ML_HEADS).T.reshape(ML_HEADS, 4, 1)
    m_out = _mlstm(z3, gates_t, gate_bias, g_mlstm)

    split = MLA_HEADS * MLA_V
    return _outproj(x, [a_out.reshape(n, -1), m_out.reshape(n, -1)], [w_o[:split], w_o[split:]]), casted


def _odd_mixer(x, b, s, positions, norm, w_in, lam_q1, lam_k1, lam_q2, lam_k2, g_sub, w_o, lam_init, side_casts):
    n = b * s
    width = DA_HEADS * 2 * DA_HEAD
    zq, vt = _qkv_proj(x, b, s, norm, w_in, LOG2_E * DA_HEAD ** -0.5, DA_HEADS, 2 * DA_HEAD)
    zq = zq.reshape(b, s, 2 * width)
    slopes = jnp.asarray([2.0 ** (-8.0 * (h + 1) / DA_HEADS) for h in range(DA_HEADS)], dtype=F32)
    lam_vec = jnp.stack([lam_q1, lam_k1, lam_q2, lam_k2]).astype(F32)
    pos = positions - positions[:, :1]
    o, casted = _diff_attention(zq, vt, pos.reshape(b, s, 1), pos.reshape(b, 1, s), slopes, lam_vec, g_sub, lam_init,
                                side_casts)
    return _outproj(x, [o.reshape(n, width)], [w_o]), casted


def kernel(x, positions, l0_ffn1_norm, l0_ffn1_w_gu, l0_ffn1_w_down, l0_mix_norm, l0_w_in, l0_g_cq, l0_w_uq, l0_g_ckv, l0_w_ukv, l0_b_gates, l0_g_mlstm, l0_w_o, l0_ffn2_norm, l0_ffn2_w_gu, l0_ffn2_w_down, l1_ffn1_norm, l1_ffn1_w_gu, l1_ffn1_w_down, l1_mix_norm, l1_w_in, l1_lam_q1, l1_lam_k1, l1_lam_q2, l1_lam_k2, l1_g_sub, l1_w_o, l1_ffn2_norm, l1_ffn2_w_gu, l1_ffn2_w_down, final_norm):
    b, s, d = x.shape
    h = x.reshape(b * s, d)
    h = _ffn(h, l0_ffn1_norm, l0_ffn1_w_gu, l0_ffn1_w_down)
    h, (w2_gu, w2_down, w3_gu, w3_down) = _even_mixer(
        h, b, s, positions, l0_mix_norm, l0_w_in, l0_g_cq, l0_w_uq, l0_g_ckv, l0_w_ukv, l0_b_gates, l0_g_mlstm, l0_w_o,
        side_casts=(l0_ffn2_w_gu, l0_ffn2_w_down, l1_ffn1_w_gu, l1_ffn1_w_down))
    h = _ffn(h, l0_ffn2_norm, w2_gu, w2_down)
    h = _ffn(h, l1_ffn1_norm, w3_gu, w3_down)
    lam_init = 0.8 - 0.6 * math.exp(-0.3 * 1)
    h, (w4_gu, w4_down) = _odd_mixer(
        h, b, s, positions, l1_mix_norm, l1_w_in, l1_lam_q1, l1_lam_k1, l1_lam_q2, l1_lam_k2, l1_g_sub, l1_w_o, lam_init,
        side_casts=(l1_ffn2_w_gu, l1_ffn2_w_down))
    h = _ffn(h, l1_ffn2_norm, w4_gu, w4_down, final_gain=final_norm)
    return h.reshape(b, s, d)
```

```python
import functools
import math

import jax
import jax.numpy as jnp
import numpy as np
from jax import lax
from jax.experimental import pallas as pl
from jax.experimental.pallas import tpu as pltpu

F32 = jnp.float32
BF16 = jnp.bfloat16

D_MODEL = 2048
MLA_HEADS = 8
MLA_Q_RANK = 512
MLA_KV_RANK = 512
MLA_NOPE = 128
MLA_ROPE = 64
MLA_V = 128
ROPE_THETA = 10000.0
ML_HEADS = 4
ML_QK = 128
ML_V = 256
DA_HEADS = 8
DA_HEAD = 128
D_FF = 5632
EPS = 1e-6
NEG_INIT = -1e30
LOG2_E = math.log2(math.e)

LANES = 128
BF16_SUBLANES = 16
MLA_QK_PAD = 256
V7X_VMEM_BYTES = 64 * 1024 * 1024
VMEM_REQUEST_CAP_BYTES = V7X_VMEM_BYTES - 6 * 1024 * 1024

EV_CQ = 0
EV_CKV = EV_CQ + MLA_Q_RANK
EV_MV = EV_CKV + MLA_KV_RANK
EV_MO = EV_MV + ML_HEADS * ML_V
EV_MQ = EV_MO + ML_HEADS * ML_V
EV_MK = EV_MQ + ML_HEADS * ML_QK
EV_KR = EV_MK + ML_HEADS * ML_QK
EV_MG = EV_KR + LANES
EV_COLS = EV_MG + LANES
assert EV_MV % ML_V == 0 and EV_MO % ML_V == 0 and EV_MQ % ML_QK == 0 and EV_MK % ML_QK == 0


class Tiles:
    ffn_rows = 1024
    ffn_cols = 512
    proj_rows = 512
    proj_col_steps_even = 2
    prep_rows = 512
    out_rows = 512
    mla_q = 2048
    mla_k = 512
    attn_q = 1024
    attn_k = 512
    mlstm_chunk = 256
    mlstm_post_rows = 512


def _compiler_params(semantics, vmem_bytes):
    return pltpu.CompilerParams(dimension_semantics=semantics,
                                vmem_limit_bytes=int(min(vmem_bytes, VMEM_REQUEST_CAP_BYTES)))


def _cast_blocking(shape, steps):
    rows, cols = shape
    for col_blocks in range(1, cols // LANES + 1):
        if cols % col_blocks or (cols // col_blocks) % LANES or steps % col_blocks:
            continue
        row_blocks = steps // col_blocks
        if rows % row_blocks == 0 and (rows // row_blocks) % BF16_SUBLANES == 0:
            return rows // row_blocks, cols // col_blocks
    raise ValueError(f"{shape} does not tile into {steps} aligned blocks")


def _side_cast_specs(side_casts, steps, step_index):
    specs = []
    for w in side_casts:
        rb, cb = _cast_blocking(w.shape, steps)
        col_blocks = w.shape[1] // cb
        specs.append(pl.BlockSpec(
            (rb, cb), lambda *idx, n=col_blocks: (step_index(*idx) // n, step_index(*idx) % n)))
    return specs, list(specs), [jax.ShapeDtypeStruct(w.shape, BF16) for w in side_casts]


def _rms(x, gain):
    y = x * lax.rsqrt(jnp.mean(x * x, axis=-1, keepdims=True) + EPS)
    return y * gain


def _ffn_kernel(*refs, n_ff_steps, final):
    if final:
        x_ref, g_ref, wg_ref, wu_ref, wd_ref, gf_ref, o_ref, xn_ref = refs
    else:
        x_ref, g_ref, wg_ref, wu_ref, wd_ref, o_ref, xn_ref = refs
    j = pl.program_id(1)

    def branch(xn):
        gate = jnp.dot(xn, wg_ref[...], preferred_element_type=F32)
        up = jnp.dot(xn, wu_ref[...], preferred_element_type=F32)
        act = (0.5 * (gate * jax.nn.sigmoid(gate)) * up).astype(BF16)
        return jnp.dot(act, wd_ref[...], preferred_element_type=F32)

    @pl.when(j == 0)
    def _():
        x = x_ref[...]
        xn = _rms(x, g_ref[...]).astype(BF16)
        xn_ref[...] = xn
        o_ref[...] = x + branch(xn)

    last = n_ff_steps - 1

    @pl.when((j > 0) & (j < last) if final else j > 0)
    def _():
        o_ref[...] += branch(xn_ref[...])

    if final:
        @pl.when(j == last)
        def _():
            o_ref[...] = _rms(o_ref[...] + branch(xn_ref[...]), gf_ref[...])


def _ffn(x, gain, w_gu, w_down, final_gain=None):
    n, d = x.shape
    tm, tf = Tiles.ffn_rows, Tiles.ffn_cols
    assert n % tm == 0 and D_FF % tf == 0 and D_FF // tf >= 2 and w_gu.shape == (d, 2 * D_FF)
    nf = D_FF // tf
    wgu = w_gu.astype(BF16)
    wd = w_down.astype(BF16)
    final = final_gain is not None
    in_specs = [
        pl.BlockSpec((tm, d), lambda i, j: (i, 0)),
        pl.BlockSpec((1, d), lambda i, j: (0, 0)),
        pl.BlockSpec((d, tf), lambda i, j: (0, j)),
        pl.BlockSpec((d, tf), lambda i, j: (0, j + nf)),
        pl.BlockSpec((tf, d), lambda i, j: (j, 0)),
    ]
    args = [x, gain.reshape(1, d), wgu, wgu, wd]
    if final:
        in_specs.append(pl.BlockSpec((1, d), lambda i, j: (0, 0)))
        args.append(final_gain.reshape(1, d))
    vmem = (2 * 2 * tm * d * 4 + tm * d * 2 + 2 * 3 * d * tf * 2 + 4 * tm * tf * 4 + tm * d * 4)
    return pl.pallas_call(
        functools.partial(_ffn_kernel, n_ff_steps=nf, final=final),
        grid=(n // tm, nf),
        in_specs=in_specs,
        out_specs=pl.BlockSpec((tm, d), lambda i, j: (i, 0)),
        out_shape=jax.ShapeDtypeStruct((n, d), F32),
        scratch_shapes=[pltpu.VMEM((tm, d), BF16)],
        compiler_params=_compiler_params(("parallel", "arbitrary"), vmem),
        name="ffn",
    )(*args)


def _normproj_kernel(x_ref, g_ref, w_ref, cs_ref, o_ref, xn_ref):
    tn = o_ref.shape[1]

    def project(xn):
        col = pl.multiple_of(pl.program_id(1) * tn, LANES)
        acc = jnp.dot(xn, w_ref[:, pl.ds(col, tn)], preferred_element_type=F32)
        o_ref[...] = (acc * cs_ref[...]).astype(o_ref.dtype)

    @pl.when(pl.program_id(1) == 0)
    def _():
        xn = _rms(x_ref[...], g_ref[...]).astype(BF16)
        xn_ref[...] = xn
        project(xn)

    @pl.when(pl.program_id(1) > 0)
    def _():
        project(xn_ref[...])


def _normproj(x, gain, w, col_scale):
    n, d = x.shape
    cols = w.shape[1]
    col_steps = Tiles.proj_col_steps_even
    tm, tn = Tiles.proj_rows, cols // col_steps
    assert n % tm == 0 and tn * col_steps == cols and tn % LANES == 0
    out_dtype = F32
    vmem = 2 * tm * d * 4 + tm * d * 2 + d * cols * 2 + 2 * tm * tn * 4 + 2 * tm * tn * 4 + (4 << 20)
    return pl.pallas_call(
        _normproj_kernel,
        grid=(n // tm, col_steps),
        in_specs=[
            pl.BlockSpec((tm, d), lambda i, j: (i, 0)),
            pl.BlockSpec((1, d), lambda i, j: (0, 0)),
            pl.BlockSpec((d, cols), lambda i, j: (0, 0), pipeline_mode=pl.Buffered(1)),
            pl.BlockSpec((1, tn), lambda i, j: (0, j)),
        ],
        out_specs=pl.BlockSpec((tm, tn), lambda i, j: (i, j)),
        out_shape=jax.ShapeDtypeStruct((n, cols), out_dtype),
        scratch_shapes=[pltpu.VMEM((tm, d), BF16)],
        compiler_params=_compiler_params(("parallel", "arbitrary"), vmem),
        name="normproj",
    )(x, gain.reshape(1, d), w.astype(BF16), col_scale.reshape(1, cols))


def _qkv_proj_kernel(x_ref, g_ref, w_ref, o_ref, vt_ref, xn_ref, *, q_scale, heads, head_width):
    j = pl.program_id(1)

    @pl.when(j == 0)
    def _():
        xn_ref[...] = _rms(x_ref[...], g_ref[...]).astype(BF16)

    width = heads * head_width

    def project(part):
        return jnp.dot(xn_ref[...], w_ref[:, part * width:(part + 1) * width], preferred_element_type=F32)

    @pl.when(j == 0)
    def _():
        o_ref[...] = (project(0) * q_scale).astype(o_ref.dtype)

    @pl.when(j == 1)
    def _():
        o_ref[...] = project(1).astype(o_ref.dtype)

    @pl.when(j == 2)
    def _():
        acc = project(2)
        for h in range(heads):
            vt_ref[0, h] = acc[:, h * head_width:(h + 1) * head_width].T.astype(vt_ref.dtype)


def _qkv_proj(x, b, s, gain, w, q_scale, heads, head_width):
    n, d = x.shape
    width = heads * head_width
    tm = Tiles.proj_rows
    assert w.shape == (d, 3 * width) and s % tm == 0 and n == b * s
    tiles_per_seq = s // tm
    vmem = 2 * tm * d * 4 + tm * d * 2 + 3 * d * width * 2 + 4 * tm * width * 2 + 3 * tm * width * 4
    return pl.pallas_call(
        functools.partial(_qkv_proj_kernel, q_scale=q_scale, heads=heads, head_width=head_width),
        grid=(n // tm, 3),
        in_specs=[
            pl.BlockSpec((tm, d), lambda i, j: (i, 0)),
            pl.BlockSpec((1, d), lambda i, j: (0, 0)),
            pl.BlockSpec((d, 3 * width), lambda i, j: (0, 0), pipeline_mode=pl.Buffered(1)),
        ],
        out_specs=[
            pl.BlockSpec((tm, width), lambda i, j: (i, jnp.minimum(j, 1))),
            pl.BlockSpec((1, heads, head_width, tm), lambda i, j: (i // tiles_per_seq, 0, 0, i % tiles_per_seq)),
        ],
        out_shape=[
            jax.ShapeDtypeStruct((n, 2 * width), BF16),
            jax.ShapeDtypeStruct((b, heads, head_width, s), BF16),
        ],
        scratch_shapes=[pltpu.VMEM((tm, d), BF16)],
        compiler_params=_compiler_params(("parallel", "arbitrary"), vmem),
        name="qkv_proj",
    )(x, gain.reshape(1, d), w.astype(BF16))


def _rope_tile(x, cos_t, sin_t):
    lane = lax.broadcasted_iota(jnp.int32, x.shape, 1)
    half = MLA_ROPE // 2
    partner = jnp.where(lane % MLA_ROPE < half, pltpu.roll(x, LANES - half, 1), pltpu.roll(x, half, 1))
    return x * cos_t + partner * sin_t


def _mla_prep_kernel(cq_ref, ckv_ref, kr_ref, gq_ref, gkv_ref, wq_ref, wk_ref, wvt_ref, cos_ref, sin_ref,
                     q_ref, k_ref, vt_ref):
    cos_t = cos_ref[...]
    sin_t = sin_ref[...]
    cqn = _rms(cq_ref[...], gq_ref[...]).astype(BF16)
    ckvn_f32 = _rms(ckv_ref[...], gkv_ref[...])
    ckvn = ckvn_f32.astype(BF16)
    scale = LOG2_E * (MLA_NOPE + MLA_ROPE) ** -0.5
    q = jnp.dot(cqn, wq_ref[...], preferred_element_type=F32)
    kn = jnp.dot(ckvn, wk_ref[...], preferred_element_type=F32)
    v_t = jnp.dot(wvt_ref[...], ckvn_f32.T.astype(BF16), preferred_element_type=F32)
    ones = jnp.ones((BF16_SUBLANES, v_t.shape[1]), BF16)
    lane = lax.broadcasted_iota(jnp.int32, cos_t.shape, 1)
    k_rope = jnp.where(lane < MLA_ROPE, _rope_tile(kr_ref[...], cos_t, sin_t), 0.0).astype(BF16)
    for h in range(MLA_HEADS):
        base = h * MLA_QK_PAD
        q_ref[:, base:base + MLA_NOPE] = (q[:, base:base + MLA_NOPE] * scale).astype(BF16)
        q_rope = _rope_tile(q[:, base + MLA_NOPE:base + MLA_QK_PAD], cos_t, sin_t)
        q_ref[:, base + MLA_NOPE:base + MLA_QK_PAD] = (q_rope * scale).astype(BF16)
        k_ref[:, base:base + MLA_NOPE] = kn[:, h * MLA_NOPE:(h + 1) * MLA_NOPE].astype(BF16)
        k_ref[:, base + MLA_NOPE:base + MLA_QK_PAD] = k_rope
        vt_ref[0, h, :MLA_V, :] = v_t[h * MLA_V:(h + 1) * MLA_V].astype(BF16)
        vt_ref[0, h, MLA_V:, :] = ones


def _mla_prep(z, b, s, g_cq, g_ckv, wq, wk, wv_t, cos_t, sin_t):
    n = z.shape[0]
    tm = Tiles.prep_rows
    assert s % tm == 0 and n == b * s
    tiles_per_seq = s // tm
    qk_cols = MLA_HEADS * MLA_QK_PAD
    v_cols = MLA_HEADS * MLA_V
    vt_rows = MLA_V + BF16_SUBLANES
    row = lambda i: (i, 0)
    fixed = lambda i: (0, 0)
    vmem = (2 * tm * (MLA_Q_RANK + MLA_KV_RANK + 3 * LANES) * 4 + 2 * MLA_Q_RANK * (qk_cols + 2 * v_cols) * 2
            + 2 * tm * (2 * qk_cols + v_cols) * 2 + 3 * tm * qk_cols * 4)
    return pl.pallas_call(
        _mla_prep_kernel,
        grid=(n // tm,),
        in_specs=[
            pl.BlockSpec((tm, MLA_Q_RANK), lambda i: (i, EV_CQ // MLA_Q_RANK)),
            pl.BlockSpec((tm, MLA_KV_RANK), lambda i: (i, EV_CKV // MLA_KV_RANK)),
            pl.BlockSpec((tm, LANES), lambda i: (i, EV_KR // LANES)),
            pl.BlockSpec((1, MLA_Q_RANK), fixed),
            pl.BlockSpec((1, MLA_KV_RANK), fixed),
            pl.BlockSpec((MLA_Q_RANK, qk_cols), fixed),
            pl.BlockSpec((MLA_KV_RANK, v_cols), fixed),
            pl.BlockSpec((v_cols, MLA_KV_RANK), fixed),
            pl.BlockSpec((tm, LANES), row),
            pl.BlockSpec((tm, LANES), row),
        ],
        out_specs=[
            pl.BlockSpec((tm, qk_cols), row),
            pl.BlockSpec((tm, qk_cols), row),
            pl.BlockSpec((1, MLA_HEADS, vt_rows, tm), lambda i: (i // tiles_per_seq, 0, 0, i % tiles_per_seq)),
        ],
        out_shape=[
            jax.ShapeDtypeStruct((n, qk_cols), BF16),
            jax.ShapeDtypeStruct((n, qk_cols), BF16),
            jax.ShapeDtypeStruct((b, MLA_HEADS, vt_rows, s), BF16),
        ],
        compiler_params=_compiler_params(("parallel",), vmem),
        name="mla_prep",
    )(z, z, z, g_cq.reshape(1, -1), g_ckv.reshape(1, -1), wq, wk, wv_t, cos_t, sin_t)


def _mla_attn_kernel(q_ref, k_ref, vt_ref, *rest, n_chunks, tk, n_casts):
    cast_in, o_ref, cast_out = rest[:n_casts], rest[n_casts], rest[n_casts + 1:]
    for src, dst in zip(cast_in, cast_out):
        dst[...] = src[...].astype(BF16)
    q = q_ref[0]
    tq = q.shape[0]
    rows = vt_ref.shape[2]

    def scores(c):
        k = k_ref[0, c * tk:(c + 1) * tk, :]
        return lax.dot_general(k, q, (((1,), (1,)), ((), ())), preferred_element_type=F32)

    m = jnp.full((1, tq), NEG_INIT, F32)
    acc = jnp.zeros((rows, tq), F32)
    s = scores(0)
    for c in range(n_chunks):
        s_next = scores(c + 1) if c + 1 < n_chunks else None
        m_new = jnp.maximum(m, jnp.max(s, axis=0, keepdims=True))
        alpha = jnp.exp2(m - m_new)
        p = jnp.exp2(s - m_new).astype(BF16)
        acc = alpha * acc + jnp.dot(vt_ref[0, 0, :, c * tk:(c + 1) * tk], p, preferred_element_type=F32)
        m, s = m_new, s_next
    o_t = acc[:MLA_V] * (1.0 / acc[MLA_V:MLA_V + 1])
    o_ref[0] = o_t.T.astype(o_ref.dtype)


def _mla_attention(q, k, vt, side_casts=()):
    b, s, _ = q.shape
    tq, tk = Tiles.mla_q, Tiles.mla_k
    rows = vt.shape[2]
    assert s % tq == 0 and s % tk == 0
    nq = s // tq
    cast_in_specs, cast_out_specs, cast_shapes = _side_cast_specs(
        side_casts, b * MLA_HEADS * nq, lambda bi, h, i: (bi * MLA_HEADS + h) * nq + i)
    cast_bytes = sum(3 * 2 * w.size * 2 // (b * MLA_HEADS * nq) for w in side_casts)
    vmem = (2 * tq * MLA_QK_PAD * 2 + 2 * s * (MLA_QK_PAD + rows) * 2 + 2 * tq * MLA_V * 2
            + 6 * tq * tk * 4 + cast_bytes)
    out = pl.pallas_call(
        functools.partial(_mla_attn_kernel, n_chunks=s // tk, tk=tk, n_casts=len(side_casts)),
        grid=(b, MLA_HEADS, nq),
        in_specs=[
            pl.BlockSpec((1, tq, MLA_QK_PAD), lambda bi, h, i: (bi, i, h)),
            pl.BlockSpec((1, s, MLA_QK_PAD), lambda bi, h, i: (bi, 0, h)),
            pl.BlockSpec((1, 1, rows, s), lambda bi, h, i: (bi, h, 0, 0)),
        ] + cast_in_specs,
        out_specs=[pl.BlockSpec((1, tq, MLA_V), lambda bi, h, i: (bi, i, h))] + cast_out_specs,
        out_shape=[jax.ShapeDtypeStruct((b, s, MLA_HEADS * MLA_V), BF16)] + cast_shapes,
        compiler_params=_compiler_params(("parallel", "parallel", "arbitrary"), vmem),
        name="mla_attn",
    )(q, k, vt, *side_casts)
    return out[0], out[1:]


def _log_sigmoid(x):
    return jnp.minimum(x, 0.0) - jnp.log(1.0 + jnp.exp(-jnp.abs(x)))


def _mlstm_masks(chunk):
    t_idx = lax.broadcasted_iota(jnp.int32, (chunk, chunk), 0)
    s_idx = lax.broadcasted_iota(jnp.int32, (chunk, chunk), 1)
    lower, upper = s_idx <= t_idx, s_idx >= t_idx
    one_hot = lambda m: jnp.where(m, 1.0, 0.0).astype(F32)
    blocked = lambda m: jnp.where(m, 0.0, NEG_INIT).astype(F32)
    return {"eye": one_hot(s_idx == t_idx), "lower": one_hot(lower), "upper": one_hot(upper),
            "neg_lower": blocked(lower), "neg_upper": blocked(upper)}


def _mlstm_chunk(q_ref, k_ref, v_ref, gates, masks, c_ref, n_row, m_st, start, chunk, li_idx, lf_idx, reverse):
    L = chunk
    q = q_ref[0, pl.ds(start, L), :]
    k = k_ref[0, pl.ds(start, L), :]
    v = v_ref[0, pl.ds(start, L), :].astype(BF16)
    li_row = gates[li_idx:li_idx + 1, :]
    lf_row = _log_sigmoid(gates[lf_idx:lf_idx + 1, :])
    li_col = jnp.sum(masks["eye"] * li_row, axis=1, keepdims=True)
    lf_col = jnp.sum(masks["eye"] * lf_row, axis=1, keepdims=True)
    visible, visible_t, blocked = ((masks["upper"], masks["lower"], masks["neg_upper"]) if reverse
                                   else (masks["lower"], masks["upper"], masks["neg_lower"]))
    a_col = jnp.sum(visible * lf_row, axis=1, keepdims=True)
    a_row = jnp.sum(visible_t * lf_col, axis=0, keepdims=True)
    g = jnp.sum(lf_row, axis=1, keepdims=True)

    dmat = (a_col - a_row) + li_row + blocked
    inter = a_col + m_st
    m = jnp.maximum(inter, jnp.max(dmat, axis=1, keepdims=True))
    w_inter = jnp.exp(inter - m)
    q16 = q.astype(BF16)
    qk = (lax.dot_general(q16, k.astype(BF16), (((1,), (1,)), ((), ())), preferred_element_type=F32)
          * jnp.exp(dmat - m))
    num = (w_inter * jnp.dot(q16, c_ref[...].astype(BF16), preferred_element_type=F32)
           + jnp.dot(qk.astype(BF16), v, preferred_element_type=F32))
    den = w_inter * jnp.sum(q * n_row, axis=1, keepdims=True) + jnp.sum(qk, axis=1, keepdims=True)
    h_out = num / jnp.maximum(jnp.abs(den), jnp.exp(-m))

    r_col = g - a_col + li_col
    m_new = jnp.maximum(g + m_st, jnp.max(r_col, axis=0, keepdims=True))
    w_old = jnp.exp(g + m_st - m_new)
    kw = k * jnp.exp(r_col - m_new)
    c_ref[...] = w_old * c_ref[...] + jnp.dot(kw.T.astype(BF16), v, preferred_element_type=F32)
    n_new = w_old * n_row + jnp.sum(kw, axis=0, keepdims=True)
    return h_out, n_new, m_new


def _mlstm_kernel(q_ref, k_ref, v_ref, gate_ref, bias_ref, mo_ref, gn_ref, o_ref,
                  hf_ref, hb_ref, cf_ref, cb_ref, *, seq, chunk, post_rows):
    nc = seq // chunk
    cf_ref[...] = jnp.zeros_like(cf_ref)
    cb_ref[...] = jnp.zeros_like(cb_ref)
    bias = bias_ref[0]
    masks = _mlstm_masks(chunk)

    def body(i, carry):
        n_f, m_f, n_b, m_b = carry
        sf = pl.multiple_of(i * chunk, chunk)
        sb = pl.multiple_of((nc - 1 - i) * chunk, chunk)
        h_f, n_f, m_f = _mlstm_chunk(q_ref, k_ref, v_ref, gate_ref[0, 0, :, pl.ds(sf, chunk)] + bias, masks,
                                     cf_ref, n_f, m_f, sf, chunk, 0, 1, False)
        hf_ref[pl.ds(sf, chunk), :] = h_f
        h_b, n_b, m_b = _mlstm_chunk(q_ref, k_ref, v_ref, gate_ref[0, 0, :, pl.ds(sb, chunk)] + bias, masks,
                                     cb_ref, n_b, m_b, sb, chunk, 2, 3, True)
        hb_ref[pl.ds(sb, chunk), :] = h_b
        return n_f, m_f, n_b, m_b

    n0 = jnp.zeros((1, ML_QK), F32)
    m0 = jnp.full((1, 1), NEG_INIT, F32)
    lax.fori_loop(0, nc, body, (n0, m0, n0, m0), unroll=2)

    gn = gn_ref[...]

    def post(i, carry):
        r = pl.multiple_of(i * post_rows, post_rows)
        hm = hf_ref[pl.ds(r, post_rows), :] + hb_ref[pl.ds(r, post_rows), :]
        y = _rms(hm, gn)
        o_ref[0, pl.ds(r, post_rows), :] = (y * jax.nn.sigmoid(mo_ref[0, pl.ds(r, post_rows), :])).astype(o_ref.dtype)
        return carry

    lax.fori_loop(0, seq // post_rows, post, 0)


def _mlstm(z3, gates_t, gate_bias, g_mlstm):
    b, s, _ = z3.shape
    chunk = Tiles.mlstm_chunk
    assert s % (2 * chunk) == 0 and s % Tiles.mlstm_post_rows == 0
    vmem = (2 * s * (2 * ML_QK + 2 * ML_V + LANES) * 4 + 2 * s * ML_V * 2 + 2 * s * ML_V * 4
            + 2 * ML_QK * ML_V * 4 + 24 * chunk * chunk * 4 + (2 << 20))
    return pl.pallas_call(
        functools.partial(_mlstm_kernel, seq=s, chunk=chunk, post_rows=Tiles.mlstm_post_rows),
        grid=(b, ML_HEADS),
        in_specs=[
            pl.BlockSpec((1, s, ML_QK), lambda bi, h: (bi, 0, EV_MQ // ML_QK + h)),
            pl.BlockSpec((1, s, ML_QK), lambda bi, h: (bi, 0, EV_MK // ML_QK + h)),
            pl.BlockSpec((1, s, ML_V), lambda bi, h: (bi, 0, EV_MV // ML_V + h)),
            pl.BlockSpec((1, 1, 4, s), lambda bi, h: (bi, h, 0, 0)),
            pl.BlockSpec((1, 4, 1), lambda bi, h: (h, 0, 0)),
            pl.BlockSpec((1, s, ML_V), lambda bi, h: (bi, 0, EV_MO // ML_V + h)),
            pl.BlockSpec((1, ML_V), lambda bi, h: (0, h)),
        ],
        out_specs=pl.BlockSpec((1, s, ML_V), lambda bi, h: (bi, 0, h)),
        out_shape=jax.ShapeDtypeStruct((b, s, ML_HEADS * ML_V), BF16),
        scratch_shapes=[
            pltpu.VMEM((s, ML_V), F32),
            pltpu.VMEM((s, ML_V), F32),
            pltpu.VMEM((ML_QK, ML_V), F32),
            pltpu.VMEM((ML_QK, ML_V), F32),
        ],
        compiler_params=_compiler_params(("parallel", "parallel"), vmem),
        name="mlstm",
    )(z3, z3, z3, gates_t, gate_bias, z3, g_mlstm.reshape(1, -1))


def _outproj_kernel(*refs, n_lhs):
    x_ref = refs[0]
    a_refs = refs[1:1 + n_lhs]
    w_refs = refs[1 + n_lhs:1 + 2 * n_lhs]
    o_ref = refs[1 + 2 * n_lhs]
    acc = x_ref[...]
    for a_ref, w_ref in zip(a_refs, w_refs):
        acc = acc + jnp.dot(a_ref[...], w_ref[...], preferred_element_type=F32)
    o_ref[...] = acc


def _outproj(x, lhs, weights):
    n, d = x.shape
    tm = Tiles.out_rows
    assert n % tm == 0
    row = lambda i: (i, 0)
    fixed = lambda i: (0, 0)
    k_total = sum(a.shape[1] for a in lhs)
    vmem = 2 * 2 * tm * d * 4 + 2 * tm * k_total * 2 + 2 * k_total * d * 2 + tm * d * 4
    return pl.pallas_call(
        functools.partial(_outproj_kernel, n_lhs=len(lhs)),
        grid=(n // tm,),
        in_specs=([pl.BlockSpec((tm, d), row)]
                  + [pl.BlockSpec((tm, a.shape[1]), row) for a in lhs]
                  + [pl.BlockSpec(w.shape, fixed) for w in weights]),
        out_specs=pl.BlockSpec((tm, d), row),
        out_shape=jax.ShapeDtypeStruct((n, d), F32),
        compiler_params=_compiler_params(("parallel",), vmem),
        name="outproj",
    )(x, *lhs, *[w.astype(BF16) for w in weights])


def _diff_attn_kernel(slope_ref, q1_ref, q2_ref, k1_ref, k2_ref, vt_ref, pq_ref, pk_ref, lam_ref, gs_ref, *rest,
                      n_chunks, tk, lam_init, n_casts):
    cast_in, o_ref, cast_out, vta_ref = rest[:n_casts], rest[n_casts], rest[n_casts + 1:-1], rest[-1]
    for src, dst in zip(cast_in, cast_out):
        dst[...] = src[...].astype(BF16)
    h = pl.program_id(1)
    slope = slope_ref[h] * LOG2_E
    q1 = q1_ref[0]
    q2 = q2_ref[0]
    tq = q1.shape[0]
    dv = 2 * DA_HEAD
    rows = vta_ref.shape[0]

    @pl.when(pl.program_id(2) == 0)
    def _():
        vta_ref[:dv, :] = vt_ref[0, 0]
        vta_ref[dv:, :] = jnp.ones((rows - dv, vta_ref.shape[1]), BF16)
    pos_q = slope * pq_ref[0].astype(F32)
    nt = (((1,), (1,)), ((), ()))

    def scores(c):
        rows_c = slice(c * tk, (c + 1) * tk)
        dist = jnp.abs(slope * pk_ref[0, rows_c, :].astype(F32) - pos_q)
        s1 = lax.dot_general(k1_ref[0, rows_c, :], q1, nt, preferred_element_type=F32) - dist
        s2 = lax.dot_general(k2_ref[0, rows_c, :], q2, nt, preferred_element_type=F32) - dist
        return s1, s2

    def softmax_step(s, vt, m, acc):
        m_new = jnp.maximum(m, jnp.max(s, axis=0, keepdims=True))
        alpha = jnp.exp2(m - m_new)
        p = jnp.exp2(s - m_new).astype(BF16)
        return m_new, alpha * acc + jnp.dot(vt, p, preferred_element_type=F32)

    m1 = m2 = jnp.full((1, tq), NEG_INIT, F32)
    acc1 = acc2 = jnp.zeros((rows, tq), F32)
    s1, s2 = scores(0)
    for c in range(n_chunks):
        nxt = scores(c + 1) if c + 1 < n_chunks else (None, None)
        vt = vta_ref[:, c * tk:(c + 1) * tk]
        m1, acc1 = softmax_step(s1, vt, m1, acc1)
        m2, acc2 = softmax_step(s2, vt, m2, acc2)
        s1, s2 = nxt

    lam_vec = lam_ref[...]
    lam = (jnp.exp(jnp.sum(lam_vec[0:1] * lam_vec[1:2], axis=1, keepdims=True))
           - jnp.exp(jnp.sum(lam_vec[2:3] * lam_vec[3:4], axis=1, keepdims=True)) + lam_init)
    w1 = 1.0 / acc1[dv:dv + 1]
    w2 = lam / acc2[dv:dv + 1]
    o_t = acc1[:dv] * w1 - acc2[:dv] * w2
    o_ref[0] = (_rms(o_t.T, gs_ref[...]) * (1.0 - lam_init)).astype(o_ref.dtype)


def _diff_attention(zq, vt, pos_col, pos_row, slopes, lam_vec, g_sub, lam_init, side_casts=()):
    b, s, _ = zq.shape
    tq, tk = Tiles.attn_q, Tiles.attn_k
    assert s % tq == 0 and s % tk == 0
    nq = s // tq
    dv = 2 * DA_HEAD
    rows = dv + BF16_SUBLANES
    k_base = DA_HEADS * 2
    cast_in_specs, cast_out_specs, cast_shapes = _side_cast_specs(
        side_casts, b * DA_HEADS * nq, lambda bi, h, i: (bi * DA_HEADS + h) * nq + i)
    cast_bytes = sum(3 * 2 * w.size * 2 // (b * DA_HEADS * nq) for w in side_casts)
    vmem = (2 * 2 * tq * DA_HEAD * 2 + 3 * s * (2 * DA_HEAD + rows) * 2 + 2 * s * LANES * 4 + 2 * tq * dv * 2
            + 14 * tq * tk * 4 + cast_bytes)
    out = pl.pallas_call(
        functools.partial(_diff_attn_kernel, n_chunks=s // tk, tk=tk, lam_init=lam_init, n_casts=len(side_casts)),
        grid=(b, DA_HEADS, nq),
        in_specs=[
            pl.BlockSpec(memory_space=pltpu.SMEM),
            pl.BlockSpec((1, tq, DA_HEAD), lambda bi, h, i: (bi, i, 2 * h)),
            pl.BlockSpec((1, tq, DA_HEAD), lambda bi, h, i: (bi, i, 2 * h + 1)),
            pl.BlockSpec((1, s, DA_HEAD), lambda bi, h, i: (bi, 0, k_base + 2 * h)),
            pl.BlockSpec((1, s, DA_HEAD), lambda bi, h, i: (bi, 0, k_base + 2 * h + 1)),
            pl.BlockSpec((1, 1, dv, s), lambda bi, h, i: (bi, h, 0, 0)),
            pl.BlockSpec((1, 1, tq), lambda bi, h, i: (bi, 0, i)),
            pl.BlockSpec((1, s, 1), lambda bi, h, i: (bi, 0, 0)),
            pl.BlockSpec((4, DA_HEAD), lambda bi, h, i: (0, 0)),
            pl.BlockSpec((1, dv), lambda bi, h, i: (0, 0)),
        ] + cast_in_specs,
        out_specs=[pl.BlockSpec((1, tq, dv), lambda bi, h, i: (bi, i, h))] + cast_out_specs,
        out_shape=[jax.ShapeDtypeStruct((b, s, DA_HEADS * dv), BF16)] + cast_shapes,
        scratch_shapes=[pltpu.VMEM((rows, s), BF16)],
        compiler_params=_compiler_params(("parallel", "parallel", "arbitrary"), vmem),
        name="diff_attn",
    )(slopes, zq, zq, zq, zq, vt, pos_row, pos_col, lam_vec, g_sub.reshape(1, dv), *side_casts)
    return out[0], out[1:]


def _pack_even_w_in(w_in):
    d = w_in.shape[0]
    sizes = (MLA_Q_RANK, MLA_KV_RANK, MLA_ROPE, ML_HEADS * ML_QK, ML_HEADS * ML_QK, ML_HEADS * ML_V,
             ML_HEADS * ML_V, 4 * ML_HEADS)
    c_q, c_kv, k_r, m_q, m_k, m_v, m_o, m_g = jnp.split(w_in, np.cumsum(sizes)[:-1].tolist(), axis=1)
    pad = lambda w, width: jnp.concatenate([w, jnp.zeros((d, width - w.shape[1]), w.dtype)], axis=1)
    packed = jnp.concatenate([c_q, c_kv, m_v, m_o, m_q, m_k, pad(k_r, LANES), pad(m_g, LANES)], axis=1)
    assert packed.shape[1] == EV_COLS
    return packed


def _pack_w_uq(w_uq):
    r = w_uq.shape[0]
    w = w_uq.reshape(r, MLA_HEADS, MLA_NOPE + MLA_ROPE)
    w = jnp.concatenate([w, jnp.zeros((r, MLA_HEADS, MLA_QK_PAD - MLA_NOPE - MLA_ROPE), w.dtype)], axis=2)
    return w.reshape(r, MLA_HEADS * MLA_QK_PAD).astype(BF16)


def _split_w_ukv(w_ukv):
    r = w_ukv.shape[0]
    w = w_ukv.reshape(r, MLA_HEADS, MLA_NOPE + MLA_V)
    wk = w[:, :, :MLA_NOPE].reshape(r, MLA_HEADS * MLA_NOPE)
    wv_t = w[:, :, MLA_NOPE:].reshape(r, MLA_HEADS * MLA_V).T
    return wk.astype(BF16), wv_t.astype(BF16)


def _rope_tables(positions):
    inv_freq = ROPE_THETA ** (-jnp.arange(0, MLA_ROPE, 2, dtype=F32) / MLA_ROPE)
    ang = positions.astype(F32).reshape(-1, 1) * inv_freq
    cos, sin = jnp.cos(ang), jnp.sin(ang)
    zeros = jnp.zeros((ang.shape[0], LANES - MLA_ROPE), F32)
    return jnp.concatenate([cos, cos, zeros], axis=1), jnp.concatenate([-sin, sin, zeros], axis=1)


def _even_mixer(x, b, s, positions, norm, w_in, g_cq, w_uq, g_ckv, w_ukv, b_gates, g_mlstm, w_o, side_casts):
    n = b * s
    col_scale = jnp.ones((EV_COLS,), F32).at[EV_MK:EV_MK + ML_HEADS * ML_QK].set(ML_QK ** -0.5)
    z = _normproj(x, norm, _pack_even_w_in(w_in.astype(BF16)), col_scale)

    cos_t, sin_t = _rope_tables(positions)
    wk, wv_t = _split_w_ukv(w_ukv)
    q, k, vt = _mla_prep(z, b, s, g_cq, g_ckv, _pack_w_uq(w_uq), wk, wv_t, cos_t, sin_t)
    a_out, (w_o, *casted) = _mla_attention(q.reshape(b, s, -1), k.reshape(b, s, -1), vt, (w_o, *side_casts))

    z3 = z.reshape(b, s, EV_COLS)
    gates_t = z3[:, :, EV_MG:EV_MG + 4 * ML_HEADS].reshape(b, s, 4, ML_HEADS).transpose(0, 3, 2, 1)
    gate_bias = b_gates.astype(F32).reshape(4, ML_HEADS).T.reshape(ML_HEADS, 4, 1)
    m_out = _mlstm(z3, gates_t, gate_bias, g_mlstm)

    split = MLA_HEADS * MLA_V
    return _outproj(x, [a_out.reshape(n, -1), m_out.reshape(n, -1)], [w_o[:split], w_o[split:]]), casted


def _odd_mixer(x, b, s, positions, norm, w_in, lam_q1, lam_k1, lam_q2, lam_k2, g_sub, w_o, lam_init, side_casts):
    n = b * s
    width = DA_HEADS * 2 * DA_HEAD
    zq, vt = _qkv_proj(x, b, s, norm, w_in, LOG2_E * DA_HEAD ** -0.5, DA_HEADS, 2 * DA_HEAD)
    zq = zq.reshape(b, s, 2 * width)
    slopes = jnp.asarray([2.0 ** (-8.0 * (h + 1) / DA_HEADS) for h in range(DA_HEADS)], dtype=F32)
    lam_vec = jnp.stack([lam_q1, lam_k1, lam_q2, lam_k2]).astype(F32)
    pos = positions - positions[:, :1]
    o, casted = _diff_attention(zq, vt, pos.reshape(b, s, 1), pos.reshape(b, 1, s), slopes, lam_vec, g_sub, lam_init,
                                side_casts)
    return _outproj(x, [o.reshape(n, width)], [w_o]), casted


def kernel(x, positions, l0_ffn1_norm, l0_ffn1_w_gu, l0_ffn1_w_down, l0_mix_norm, l0_w_in, l0_g_cq, l0_w_uq, l0_g_ckv, l0_w_ukv, l0_b_gates, l0_g_mlstm, l0_w_o, l0_ffn2_norm, l0_ffn2_w_gu, l0_ffn2_w_down, l1_ffn1_norm, l1_ffn1_w_gu, l1_ffn1_w_down, l1_mix_norm, l1_w_in, l1_lam_q1, l1_lam_k1, l1_lam_q2, l1_lam_k2, l1_g_sub, l1_w_o, l1_ffn2_norm, l1_ffn2_w_gu, l1_ffn2_w_down, final_norm):
    b, s, d = x.shape
    h = x.reshape(b * s, d)
    h = _ffn(h, l0_ffn1_norm, l0_ffn1_w_gu, l0_ffn1_w_down)
    h, (w2_gu, w2_down, w3_gu, w3_down, w_in1, w_o1) = _even_mixer(
        h, b, s, positions, l0_mix_norm, l0_w_in, l0_g_cq, l0_w_uq, l0_g_ckv, l0_w_ukv, l0_b_gates, l0_g_mlstm, l0_w_o,
        side_casts=(l0_ffn2_w_gu, l0_ffn2_w_down, l1_ffn1_w_gu, l1_ffn1_w_down, l1_w_in, l1_w_o))
    h = _ffn(h, l0_ffn2_norm, w2_gu, w2_down)
    h = _ffn(h, l1_ffn1_norm, w3_gu, w3_down)
    lam_init = 0.8 - 0.6 * math.exp(-0.3 * 1)
    h, (w4_gu, w4_down) = _odd_mixer(
        h, b, s, positions, l1_mix_norm, w_in1, l1_lam_q1, l1_lam_k1, l1_lam_q2, l1_lam_k2, l1_g_sub, w_o1, lam_init,
        side_casts=(l1_ffn2_w_gu, l1_ffn2_w_down))
    h = _ffn(h, l1_ffn2_norm, w4_gu, w4_down, final_gain=final_norm)
    return h.reshape(b, s, d)
```

```python
import functools
import math

import jax
import jax.numpy as jnp
import numpy as np
from jax import lax
from jax.experimental import pallas as pl
from jax.experimental.pallas import tpu as pltpu

F32 = jnp.float32
BF16 = jnp.bfloat16

D_MODEL = 2048
MLA_HEADS = 8
MLA_Q_RANK = 512
MLA_KV_RANK = 512
MLA_NOPE = 128
MLA_ROPE = 64
MLA_V = 128
ROPE_THETA = 10000.0
ML_HEADS = 4
ML_QK = 128
ML_V = 256
DA_HEADS = 8
DA_HEAD = 128
D_FF = 5632
EPS = 1e-6
NEG_INIT = -1e30
LOG2_E = math.log2(math.e)

LANES = 128
BF16_SUBLANES = 16
MLA_QK_PAD = 256
V7X_VMEM_BYTES = 64 * 1024 * 1024
VMEM_REQUEST_CAP_BYTES = V7X_VMEM_BYTES - 6 * 1024 * 1024

EV_CQ = 0
EV_CKV = EV_CQ + MLA_Q_RANK
EV_MV = EV_CKV + MLA_KV_RANK
EV_MO = EV_MV + ML_HEADS * ML_V
EV_MQ = EV_MO + ML_HEADS * ML_V
EV_MK = EV_MQ + ML_HEADS * ML_QK
EV_KR = EV_MK + ML_HEADS * ML_QK
EV_MG = EV_KR + LANES
EV_COLS = EV_MG + LANES
assert EV_MV % ML_V == 0 and EV_MO % ML_V == 0 and EV_MQ % ML_QK == 0 and EV_MK % ML_QK == 0


class Tiles:
    ffn_rows = 1024
    ffn_cols = 512
    proj_rows = 512
    proj_col_steps_even = 2
    prep_rows = 512
    out_rows = 512
    mla_q = 2048
    mla_k = 512
    attn_q = 1024
    attn_k = 512
    mlstm_chunk = 256
    mlstm_post_rows = 512


def _compiler_params(semantics, vmem_bytes):
    return pltpu.CompilerParams(dimension_semantics=semantics,
                                vmem_limit_bytes=int(min(vmem_bytes, VMEM_REQUEST_CAP_BYTES)))


def _cast_blocking(shape, steps):
    rows, cols = shape
    for col_blocks in range(1, cols // LANES + 1):
        if cols % col_blocks or (cols // col_blocks) % LANES or steps % col_blocks:
            continue
        row_blocks = steps // col_blocks
        if rows % row_blocks == 0 and (rows // row_blocks) % BF16_SUBLANES == 0:
            return rows // row_blocks, cols // col_blocks
    raise ValueError(f"{shape} does not tile into {steps} aligned blocks")


def _side_cast_specs(side_casts, steps, step_index):
    specs = []
    for w in side_casts:
        rb, cb = _cast_blocking(w.shape, steps)
        col_blocks = w.shape[1] // cb
        specs.append(pl.BlockSpec(
            (rb, cb), lambda *idx, n=col_blocks: (step_index(*idx) // n, step_index(*idx) % n)))
    return specs, list(specs), [jax.ShapeDtypeStruct(w.shape, BF16) for w in side_casts]


def _rms(x, gain):
    y = x * lax.rsqrt(jnp.mean(x * x, axis=-1, keepdims=True) + EPS)
    return y * gain


def _ffn_kernel(*refs, n_ff_steps, final):
    if final:
        x_ref, g_ref, wg_ref, wu_ref, wd_ref, gf_ref, o_ref, xn_ref = refs
    else:
        x_ref, g_ref, wg_ref, wu_ref, wd_ref, o_ref, xn_ref = refs
    j = pl.program_id(1)

    def branch(xn):
        half = wg_ref.shape[1] // 2
        acts = []
        for c in range(2):
            cols = slice(c * half, (c + 1) * half)
            gate = jnp.dot(xn, wg_ref[:, cols], preferred_element_type=F32)
            up = jnp.dot(xn, wu_ref[:, cols], preferred_element_type=F32)
            acts.append((0.5 * (gate * jax.nn.sigmoid(gate)) * up).astype(BF16))
        act = jnp.concatenate(acts, axis=1)
        return jnp.dot(act, wd_ref[...], preferred_element_type=F32)

    @pl.when(j == 0)
    def _():
        x = x_ref[...]
        xn = _rms(x, g_ref[...]).astype(BF16)
        xn_ref[...] = xn
        o_ref[...] = x + branch(xn)

    last = n_ff_steps - 1

    @pl.when((j > 0) & (j < last) if final else j > 0)
    def _():
        o_ref[...] += branch(xn_ref[...])

    if final:
        @pl.when(j == last)
        def _():
            o_ref[...] = _rms(o_ref[...] + branch(xn_ref[...]), gf_ref[...])


def _ffn(x, gain, w_gu, w_down, final_gain=None):
    n, d = x.shape
    tm, tf = Tiles.ffn_rows, Tiles.ffn_cols
    assert n % tm == 0 and D_FF % tf == 0 and D_FF // tf >= 2 and w_gu.shape == (d, 2 * D_FF)
    nf = D_FF // tf
    wgu = w_gu.astype(BF16)
    wd = w_down.astype(BF16)
    final = final_gain is not None
    in_specs = [
        pl.BlockSpec((tm, d), lambda i, j: (i, 0)),
        pl.BlockSpec((1, d), lambda i, j: (0, 0)),
        pl.BlockSpec((d, tf), lambda i, j: (0, j)),
        pl.BlockSpec((d, tf), lambda i, j: (0, j + nf)),
        pl.BlockSpec((tf, d), lambda i, j: (j, 0)),
    ]
    args = [x, gain.reshape(1, d), wgu, wgu, wd]
    if final:
        in_specs.append(pl.BlockSpec((1, d), lambda i, j: (0, 0)))
        args.append(final_gain.reshape(1, d))
    vmem = (2 * 2 * tm * d * 4 + tm * d * 2 + 2 * 3 * d * tf * 2 + 4 * tm * tf * 4 + tm * d * 4)
    return pl.pallas_call(
        functools.partial(_ffn_kernel, n_ff_steps=nf, final=final),
        grid=(n // tm, nf),
        in_specs=in_specs,
        out_specs=pl.BlockSpec((tm, d), lambda i, j: (i, 0)),
        out_shape=jax.ShapeDtypeStruct((n, d), F32),
        scratch_shapes=[pltpu.VMEM((tm, d), BF16)],
        compiler_params=_compiler_params(("parallel", "arbitrary"), vmem),
        name="ffn",
    )(*args)


def _normproj_kernel(x_ref, g_ref, w_ref, cs_ref, o_ref, xn_ref):
    tn = o_ref.shape[1]

    def project(xn):
        col = pl.multiple_of(pl.program_id(1) * tn, LANES)
        acc = jnp.dot(xn, w_ref[:, pl.ds(col, tn)], preferred_element_type=F32)
        o_ref[...] = (acc * cs_ref[...]).astype(o_ref.dtype)

    @pl.when(pl.program_id(1) == 0)
    def _():
        xn = _rms(x_ref[...], g_ref[...]).astype(BF16)
        xn_ref[...] = xn
        project(xn)

    @pl.when(pl.program_id(1) > 0)
    def _():
        project(xn_ref[...])


def _normproj(x, gain, w, col_scale):
    n, d = x.shape
    cols = w.shape[1]
    col_steps = Tiles.proj_col_steps_even
    tm, tn = Tiles.proj_rows, cols // col_steps
    assert n % tm == 0 and tn * col_steps == cols and tn % LANES == 0
    out_dtype = F32
    vmem = 2 * tm * d * 4 + tm * d * 2 + d * cols * 2 + 2 * tm * tn * 4 + 2 * tm * tn * 4 + (4 << 20)
    return pl.pallas_call(
        _normproj_kernel,
        grid=(n // tm, col_steps),
        in_specs=[
            pl.BlockSpec((tm, d), lambda i, j: (i, 0)),
            pl.BlockSpec((1, d), lambda i, j: (0, 0)),
            pl.BlockSpec((d, cols), lambda i, j: (0, 0), pipeline_mode=pl.Buffered(1)),
            pl.BlockSpec((1, tn), lambda i, j: (0, j)),
        ],
        out_specs=pl.BlockSpec((tm, tn), lambda i, j: (i, j)),
        out_shape=jax.ShapeDtypeStruct((n, cols), out_dtype),
        scratch_shapes=[pltpu.VMEM((tm, d), BF16)],
        compiler_params=_compiler_params(("parallel", "arbitrary"), vmem),
        name="normproj",
    )(x, gain.reshape(1, d), w.astype(BF16), col_scale.reshape(1, cols))


def _qkv_proj_kernel(x_ref, g_ref, w_ref, o_ref, vt_ref, xn_ref, *, q_scale, heads, head_width):
    j = pl.program_id(1)

    @pl.when(j == 0)
    def _():
        xn_ref[...] = _rms(x_ref[...], g_ref[...]).astype(BF16)

    width = heads * head_width

    def project(part):
        return jnp.dot(xn_ref[...], w_ref[:, part * width:(part + 1) * width], preferred_element_type=F32)

    @pl.when(j == 0)
    def _():
        o_ref[...] = (project(0) * q_scale).astype(o_ref.dtype)

    @pl.when(j == 1)
    def _():
        o_ref[...] = project(1).astype(o_ref.dtype)

    @pl.when(j == 2)
    def _():
        acc = project(2)
        for h in range(heads):
            vt_ref[0, h] = acc[:, h * head_width:(h + 1) * head_width].T.astype(vt_ref.dtype)


def _qkv_proj(x, b, s, gain, w, q_scale, heads, head_width):
    n, d = x.shape
    width = heads * head_width
    tm = Tiles.proj_rows
    assert w.shape == (d, 3 * width) and s % tm == 0 and n == b * s
    tiles_per_seq = s // tm
    vmem = 2 * tm * d * 4 + tm * d * 2 + 3 * d * width * 2 + 4 * tm * width * 2 + 3 * tm * width * 4
    return pl.pallas_call(
        functools.partial(_qkv_proj_kernel, q_scale=q_scale, heads=heads, head_width=head_width),
        grid=(n // tm, 3),
        in_specs=[
            pl.BlockSpec((tm, d), lambda i, j: (i, 0)),
            pl.BlockSpec((1, d), lambda i, j: (0, 0)),
            pl.BlockSpec((d, 3 * width), lambda i, j: (0, 0), pipeline_mode=pl.Buffered(1)),
        ],
        out_specs=[
            pl.BlockSpec((tm, width), lambda i, j: (i, jnp.minimum(j, 1))),
            pl.BlockSpec((1, heads, head_width, tm), lambda i, j: (i // tiles_per_seq, 0, 0, i % tiles_per_seq)),
        ],
        out_shape=[
            jax.ShapeDtypeStruct((n, 2 * width), BF16),
            jax.ShapeDtypeStruct((b, heads, head_width, s), BF16),
        ],
        scratch_shapes=[pltpu.VMEM((tm, d), BF16)],
        compiler_params=_compiler_params(("parallel", "arbitrary"), vmem),
        name="qkv_proj",
    )(x, gain.reshape(1, d), w.astype(BF16))


def _rope_tile(x, cos_t, sin_t):
    lane = lax.broadcasted_iota(jnp.int32, x.shape, 1)
    half = MLA_ROPE // 2
    partner = jnp.where(lane % MLA_ROPE < half, pltpu.roll(x, LANES - half, 1), pltpu.roll(x, half, 1))
    return x * cos_t + partner * sin_t


def _mla_prep_kernel(cq_ref, ckv_ref, kr_ref, gq_ref, gkv_ref, wq_ref, wk_ref, wvt_ref, cos_ref, sin_ref,
                     q_ref, k_ref, vt_ref):
    cos_t = cos_ref[...]
    sin_t = sin_ref[...]
    cqn = _rms(cq_ref[...], gq_ref[...]).astype(BF16)
    ckvn_f32 = _rms(ckv_ref[...], gkv_ref[...])
    ckvn = ckvn_f32.astype(BF16)
    scale = LOG2_E * (MLA_NOPE + MLA_ROPE) ** -0.5
    q = jnp.dot(cqn, wq_ref[...], preferred_element_type=F32)
    kn = jnp.dot(ckvn, wk_ref[...], preferred_element_type=F32)
    v_t = jnp.dot(wvt_ref[...], ckvn_f32.T.astype(BF16), preferred_element_type=F32)
    ones = jnp.ones((BF16_SUBLANES, v_t.shape[1]), BF16)
    lane = lax.broadcasted_iota(jnp.int32, cos_t.shape, 1)
    k_rope = jnp.where(lane < MLA_ROPE, _rope_tile(kr_ref[...], cos_t, sin_t), 0.0).astype(BF16)
    for h in range(MLA_HEADS):
        base = h * MLA_QK_PAD
        q_ref[:, base:base + MLA_NOPE] = (q[:, base:base + MLA_NOPE] * scale).astype(BF16)
        q_rope = _rope_tile(q[:, base + MLA_NOPE:base + MLA_QK_PAD], cos_t, sin_t)
        q_ref[:, base + MLA_NOPE:base + MLA_QK_PAD] = (q_rope * scale).astype(BF16)
        k_ref[:, base:base + MLA_NOPE] = kn[:, h * MLA_NOPE:(h + 1) * MLA_NOPE].astype(BF16)
        k_ref[:, base + MLA_NOPE:base + MLA_QK_PAD] = k_rope
        vt_ref[0, h, :MLA_V, :] = v_t[h * MLA_V:(h + 1) * MLA_V].astype(BF16)
        vt_ref[0, h, MLA_V:, :] = ones


def _mla_prep(z, b, s, g_cq, g_ckv, wq, wk, wv_t, cos_t, sin_t):
    n = z.shape[0]
    tm = Tiles.prep_rows
    assert s % tm == 0 and n == b * s
    tiles_per_seq = s // tm
    qk_cols = MLA_HEADS * MLA_QK_PAD
    v_cols = MLA_HEADS * MLA_V
    vt_rows = MLA_V + BF16_SUBLANES
    row = lambda i: (i, 0)
    fixed = lambda i: (0, 0)
    vmem = (2 * tm * (MLA_Q_RANK + MLA_KV_RANK + 3 * LANES) * 4 + 2 * MLA_Q_RANK * (qk_cols + 2 * v_cols) * 2
            + 2 * tm * (2 * qk_cols + v_cols) * 2 + 3 * tm * qk_cols * 4)
    return pl.pallas_call(
        _mla_prep_kernel,
        grid=(n // tm,),
        in_specs=[
            pl.BlockSpec((tm, MLA_Q_RANK), lambda i: (i, EV_CQ // MLA_Q_RANK)),
            pl.BlockSpec((tm, MLA_KV_RANK), lambda i: (i, EV_CKV // MLA_KV_RANK)),
            pl.BlockSpec((tm, LANES), lambda i: (i, EV_KR // LANES)),
            pl.BlockSpec((1, MLA_Q_RANK), fixed),
            pl.BlockSpec((1, MLA_KV_RANK), fixed),
            pl.BlockSpec((MLA_Q_RANK, qk_cols), fixed),
            pl.BlockSpec((MLA_KV_RANK, v_cols), fixed),
            pl.BlockSpec((v_cols, MLA_KV_RANK), fixed),
            pl.BlockSpec((tm, LANES), row),
            pl.BlockSpec((tm, LANES), row),
        ],
        out_specs=[
            pl.BlockSpec((tm, qk_cols), row),
            pl.BlockSpec((tm, qk_cols), row),
            pl.BlockSpec((1, MLA_HEADS, vt_rows, tm), lambda i: (i // tiles_per_seq, 0, 0, i % tiles_per_seq)),
        ],
        out_shape=[
            jax.ShapeDtypeStruct((n, qk_cols), BF16),
            jax.ShapeDtypeStruct((n, qk_cols), BF16),
            jax.ShapeDtypeStruct((b, MLA_HEADS, vt_rows, s), BF16),
        ],
        compiler_params=_compiler_params(("parallel",), vmem),
        name="mla_prep",
    )(z, z, z, g_cq.reshape(1, -1), g_ckv.reshape(1, -1), wq, wk, wv_t, cos_t, sin_t)


def _mla_attn_kernel(q_ref, k_ref, vt_ref, *rest, n_chunks, tk, n_casts):
    cast_in, o_ref, cast_out = rest[:n_casts], rest[n_casts], rest[n_casts + 1:]
    for src, dst in zip(cast_in, cast_out):
        dst[...] = src[...].astype(BF16)
    q = q_ref[0]
    tq = q.shape[0]
    rows = vt_ref.shape[2]

    def scores(c):
        k = k_ref[0, c * tk:(c + 1) * tk, :]
        return lax.dot_general(k, q, (((1,), (1,)), ((), ())), preferred_element_type=F32)

    m = jnp.full((1, tq), NEG_INIT, F32)
    acc = jnp.zeros((rows, tq), F32)
    s = scores(0)
    for c in range(n_chunks):
        s_next = scores(c + 1) if c + 1 < n_chunks else None
        m_new = jnp.maximum(m, jnp.max(s, axis=0, keepdims=True))
        alpha = jnp.exp2(m - m_new)
        p = jnp.exp2(s - m_new).astype(BF16)
        acc = alpha * acc + jnp.dot(vt_ref[0, 0, :, c * tk:(c + 1) * tk], p, preferred_element_type=F32)
        m, s = m_new, s_next
    o_t = acc[:MLA_V] * (1.0 / acc[MLA_V:MLA_V + 1])
    o_ref[0] = o_t.T.astype(o_ref.dtype)


def _mla_attention(q, k, vt, side_casts=()):
    b, s, _ = q.shape
    tq, tk = Tiles.mla_q, Tiles.mla_k
    rows = vt.shape[2]
    assert s % tq == 0 and s % tk == 0
    nq = s // tq
    cast_in_specs, cast_out_specs, cast_shapes = _side_cast_specs(
        side_casts, b * MLA_HEADS * nq, lambda bi, h, i: (bi * MLA_HEADS + h) * nq + i)
    cast_bytes = sum(3 * 2 * w.size * 2 // (b * MLA_HEADS * nq) for w in side_casts)
    vmem = (2 * tq * MLA_QK_PAD * 2 + 2 * s * (MLA_QK_PAD + rows) * 2 + 2 * tq * MLA_V * 2
            + 6 * tq * tk * 4 + cast_bytes)
    out = pl.pallas_call(
        functools.partial(_mla_attn_kernel, n_chunks=s // tk, tk=tk, n_casts=len(side_casts)),
        grid=(b, MLA_HEADS, nq),
        in_specs=[
            pl.BlockSpec((1, tq, MLA_QK_PAD), lambda bi, h, i: (bi, i, h)),
            pl.BlockSpec((1, s, MLA_QK_PAD), lambda bi, h, i: (bi, 0, h)),
            pl.BlockSpec((1, 1, rows, s), lambda bi, h, i: (bi, h, 0, 0)),
        ] + cast_in_specs,
        out_specs=[pl.BlockSpec((1, tq, MLA_V), lambda bi, h, i: (bi, i, h))] + cast_out_specs,
        out_shape=[jax.ShapeDtypeStruct((b, s, MLA_HEADS * MLA_V), BF16)] + cast_shapes,
        compiler_params=_compiler_params(("parallel", "parallel", "arbitrary"), vmem),
        name="mla_attn",
    )(q, k, vt, *side_casts)
    return out[0], out[1:]


def _log_sigmoid(x):
    return jnp.minimum(x, 0.0) - jnp.log(1.0 + jnp.exp(-jnp.abs(x)))


def _mlstm_masks(chunk):
    t_idx = lax.broadcasted_iota(jnp.int32, (chunk, chunk), 0)
    s_idx = lax.broadcasted_iota(jnp.int32, (chunk, chunk), 1)
    lower, upper = s_idx <= t_idx, s_idx >= t_idx
    one_hot = lambda m: jnp.where(m, 1.0, 0.0).astype(F32)
    blocked = lambda m: jnp.where(m, 0.0, NEG_INIT).astype(F32)
    return {"eye": one_hot(s_idx == t_idx), "lower": one_hot(lower), "upper": one_hot(upper),
            "neg_lower": blocked(lower), "neg_upper": blocked(upper)}


def _mlstm_chunk(q_ref, k_ref, v_ref, gates, masks, c_ref, n_row, m_st, start, chunk, li_idx, lf_idx, reverse):
    L = chunk
    q = q_ref[0, pl.ds(start, L), :]
    k = k_ref[0, pl.ds(start, L), :]
    v = v_ref[0, pl.ds(start, L), :].astype(BF16)
    li_row = gates[li_idx:li_idx + 1, :]
    lf_row = _log_sigmoid(gates[lf_idx:lf_idx + 1, :])
    li_col = jnp.sum(masks["eye"] * li_row, axis=1, keepdims=True)
    lf_col = jnp.sum(masks["eye"] * lf_row, axis=1, keepdims=True)
    visible, visible_t, blocked = ((masks["upper"], masks["lower"], masks["neg_upper"]) if reverse
                                   else (masks["lower"], masks["upper"], masks["neg_lower"]))
    a_col = jnp.sum(visible * lf_row, axis=1, keepdims=True)
    a_row = jnp.sum(visible_t * lf_col, axis=0, keepdims=True)
    g = jnp.sum(lf_row, axis=1, keepdims=True)

    dmat = (a_col - a_row) + li_row + blocked
    inter = a_col + m_st
    m = jnp.maximum(inter, jnp.max(dmat, axis=1, keepdims=True))
    w_inter = jnp.exp(inter - m)
    q16 = q.astype(BF16)
    qk = (lax.dot_general(q16, k.astype(BF16), (((1,), (1,)), ((), ())), preferred_element_type=F32)
          * jnp.exp(dmat - m))
    num = (w_inter * jnp.dot(q16, c_ref[...].astype(BF16), preferred_element_type=F32)
           + jnp.dot(qk.astype(BF16), v, preferred_element_type=F32))
    den = w_inter * jnp.sum(q * n_row, axis=1, keepdims=True) + jnp.sum(qk, axis=1, keepdims=True)
    h_out = num / jnp.maximum(jnp.abs(den), jnp.exp(-m))

    r_col = g - a_col + li_col
    m_new = jnp.maximum(g + m_st, jnp.max(r_col, axis=0, keepdims=True))
    w_old = jnp.exp(g + m_st - m_new)
    kw = k * jnp.exp(r_col - m_new)
    c_ref[...] = w_old * c_ref[...] + jnp.dot(kw.T.astype(BF16), v, preferred_element_type=F32)
    n_new = w_old * n_row + jnp.sum(kw, axis=0, keepdims=True)
    return h_out, n_new, m_new


def _mlstm_kernel(q_ref, k_ref, v_ref, gate_ref, bias_ref, mo_ref, gn_ref, o_ref,
                  hf_ref, hb_ref, cf_ref, cb_ref, *, seq, chunk, post_rows):
    nc = seq // chunk
    cf_ref[...] = jnp.zeros_like(cf_ref)
    cb_ref[...] = jnp.zeros_like(cb_ref)
    bias = bias_ref[0]
    masks = _mlstm_masks(chunk)

    def body(i, carry):
        n_f, m_f, n_b, m_b = carry
        sf = pl.multiple_of(i * chunk, chunk)
        sb = pl.multiple_of((nc - 1 - i) * chunk, chunk)
        h_f, n_f, m_f = _mlstm_chunk(q_ref, k_ref, v_ref, gate_ref[0, 0, :, pl.ds(sf, chunk)] + bias, masks,
                                     cf_ref, n_f, m_f, sf, chunk, 0, 1, False)
        hf_ref[pl.ds(sf, chunk), :] = h_f
        h_b, n_b, m_b = _mlstm_chunk(q_ref, k_ref, v_ref, gate_ref[0, 0, :, pl.ds(sb, chunk)] + bias, masks,
                                     cb_ref, n_b, m_b, sb, chunk, 2, 3, True)
        hb_ref[pl.ds(sb, chunk), :] = h_b
        return n_f, m_f, n_b, m_b

    n0 = jnp.zeros((1, ML_QK), F32)
    m0 = jnp.full((1, 1), NEG_INIT, F32)
    lax.fori_loop(0, nc, body, (n0, m0, n0, m0), unroll=2)

    gn = gn_ref[...]

    def post(i, carry):
        r = pl.multiple_of(i * post_rows, post_rows)
        hm = hf_ref[pl.ds(r, post_rows), :] + hb_ref[pl.ds(r, post_rows), :]
        y = _rms(hm, gn)
        o_ref[0, pl.ds(r, post_rows), :] = (y * jax.nn.sigmoid(mo_ref[0, pl.ds(r, post_rows), :])).astype(o_ref.dtype)
        return carry

    lax.fori_loop(0, seq // post_rows, post, 0)


def _mlstm(z3, gates_t, gate_bias, g_mlstm):
    b, s, _ = z3.shape
    chunk = Tiles.mlstm_chunk
    assert s % (2 * chunk) == 0 and s % Tiles.mlstm_post_rows == 0
    vmem = (2 * s * (2 * ML_QK + 2 * ML_V + LANES) * 4 + 2 * s * ML_V * 2 + 2 * s * ML_V * 4
            + 2 * ML_QK * ML_V * 4 + 24 * chunk * chunk * 4 + (2 << 20))
    return pl.pallas_call(
        functools.partial(_mlstm_kernel, seq=s, chunk=chunk, post_rows=Tiles.mlstm_post_rows),
        grid=(b, ML_HEADS),
        in_specs=[
            pl.BlockSpec((1, s, ML_QK), lambda bi, h: (bi, 0, EV_MQ // ML_QK + h)),
            pl.BlockSpec((1, s, ML_QK), lambda bi, h: (bi, 0, EV_MK // ML_QK + h)),
            pl.BlockSpec((1, s, ML_V), lambda bi, h: (bi, 0, EV_MV // ML_V + h)),
            pl.BlockSpec((1, 1, 4, s), lambda bi, h: (bi, h, 0, 0)),
            pl.BlockSpec((1, 4, 1), lambda bi, h: (h, 0, 0)),
            pl.BlockSpec((1, s, ML_V), lambda bi, h: (bi, 0, EV_MO // ML_V + h)),
            pl.BlockSpec((1, ML_V), lambda bi, h: (0, h)),
        ],
        out_specs=pl.BlockSpec((1, s, ML_V), lambda bi, h: (bi, 0, h)),
        out_shape=jax.ShapeDtypeStruct((b, s, ML_HEADS * ML_V), BF16),
        scratch_shapes=[
            pltpu.VMEM((s, ML_V), F32),
            pltpu.VMEM((s, ML_V), F32),
            pltpu.VMEM((ML_QK, ML_V), F32),
            pltpu.VMEM((ML_QK, ML_V), F32),
        ],
        compiler_params=_compiler_params(("parallel", "parallel"), vmem),
        name="mlstm",
    )(z3, z3, z3, gates_t, gate_bias, z3, g_mlstm.reshape(1, -1))


def _outproj_kernel(*refs, n_lhs):
    x_ref = refs[0]
    a_refs = refs[1:1 + n_lhs]
    w_refs = refs[1 + n_lhs:1 + 2 * n_lhs]
    o_ref = refs[1 + 2 * n_lhs]
    acc = x_ref[...]
    for a_ref, w_ref in zip(a_refs, w_refs):
        acc = acc + jnp.dot(a_ref[...], w_ref[...], preferred_element_type=F32)
    o_ref[...] = acc


def _outproj(x, lhs, weights):
    n, d = x.shape
    tm = Tiles.out_rows
    assert n % tm == 0
    row = lambda i: (i, 0)
    fixed = lambda i: (0, 0)
    k_total = sum(a.shape[1] for a in lhs)
    vmem = 2 * 2 * tm * d * 4 + 2 * tm * k_total * 2 + 2 * k_total * d * 2 + tm * d * 4
    return pl.pallas_call(
        functools.partial(_outproj_kernel, n_lhs=len(lhs)),
        grid=(n // tm,),
        in_specs=([pl.BlockSpec((tm, d), row)]
                  + [pl.BlockSpec((tm, a.shape[1]), row) for a in lhs]
                  + [pl.BlockSpec(w.shape, fixed) for w in weights]),
        out_specs=pl.BlockSpec((tm, d), row),
        out_shape=jax.ShapeDtypeStruct((n, d), F32),
        compiler_params=_compiler_params(("parallel",), vmem),
        name="outproj",
    )(x, *lhs, *[w.astype(BF16) for w in weights])


def _diff_attn_kernel(slope_ref, q1_ref, q2_ref, k1_ref, k2_ref, vt_ref, pq_ref, pk_ref, lam_ref, gs_ref, *rest,
                      n_chunks, tk, lam_init, n_casts):
    cast_in, o_ref, cast_out, vta_ref = rest[:n_casts], rest[n_casts], rest[n_casts + 1:-1], rest[-1]
    for src, dst in zip(cast_in, cast_out):
        dst[...] = src[...].astype(BF16)
    h = pl.program_id(1)
    slope = slope_ref[h] * LOG2_E
    q1 = q1_ref[0]
    q2 = q2_ref[0]
    tq = q1.shape[0]
    dv = 2 * DA_HEAD
    rows = vta_ref.shape[0]

    @pl.when(pl.program_id(2) == 0)
    def _():
        vta_ref[:dv, :] = vt_ref[0, 0]
        vta_ref[dv:, :] = jnp.ones((rows - dv, vta_ref.shape[1]), BF16)
    pos_q = slope * pq_ref[0].astype(F32)
    nt = (((1,), (1,)), ((), ()))

    def scores(c):
        rows_c = slice(c * tk, (c + 1) * tk)
        dist = jnp.abs(slope * pk_ref[0, rows_c, :].astype(F32) - pos_q)
        s1 = lax.dot_general(k1_ref[0, rows_c, :], q1, nt, preferred_element_type=F32) - dist
        s2 = lax.dot_general(k2_ref[0, rows_c, :], q2, nt, preferred_element_type=F32) - dist
        return s1, s2

    def softmax_step(s, vt, m, acc):
        m_new = jnp.maximum(m, jnp.max(s, axis=0, keepdims=True))
        alpha = jnp.exp2(m - m_new)
        p = jnp.exp2(s - m_new).astype(BF16)
        return m_new, alpha * acc + jnp.dot(vt, p, preferred_element_type=F32)

    m1 = m2 = jnp.full((1, tq), NEG_INIT, F32)
    acc1 = acc2 = jnp.zeros((rows, tq), F32)
    s1, s2 = scores(0)
    for c in range(n_chunks):
        nxt = scores(c + 1) if c + 1 < n_chunks else (None, None)
        vt = vta_ref[:, c * tk:(c + 1) * tk]
        m1, acc1 = softmax_step(s1, vt, m1, acc1)
        m2, acc2 = softmax_step(s2, vt, m2, acc2)
        s1, s2 = nxt

    lam_vec = lam_ref[...]
    lam = (jnp.exp(jnp.sum(lam_vec[0:1] * lam_vec[1:2], axis=1, keepdims=True))
           - jnp.exp(jnp.sum(lam_vec[2:3] * lam_vec[3:4], axis=1, keepdims=True)) + lam_init)
    w1 = 1.0 / acc1[dv:dv + 1]
    w2 = lam / acc2[dv:dv + 1]
    o_t = acc1[:dv] * w1 - acc2[:dv] * w2
    o_ref[0] = (_rms(o_t.T, gs_ref[...]) * (1.0 - lam_init)).astype(o_ref.dtype)


def _diff_attention(zq, vt, pos_col, pos_row, slopes, lam_vec, g_sub, lam_init, side_casts=()):
    b, s, _ = zq.shape
    tq, tk = Tiles.attn_q, Tiles.attn_k
    assert s % tq == 0 and s % tk == 0
    nq = s // tq
    dv = 2 * DA_HEAD
    rows = dv + BF16_SUBLANES
    k_base = DA_HEADS * 2
    cast_in_specs, cast_out_specs, cast_shapes = _side_cast_specs(
        side_casts, b * DA_HEADS * nq, lambda bi, h, i: (bi * DA_HEADS + h) * nq + i)
    cast_bytes = sum(3 * 2 * w.size * 2 // (b * DA_HEADS * nq) for w in side_casts)
    vmem = (2 * 2 * tq * DA_HEAD * 2 + 3 * s * (2 * DA_HEAD + rows) * 2 + 2 * s * LANES * 4 + 2 * tq * dv * 2
            + 14 * tq * tk * 4 + cast_bytes)
    out = pl.pallas_call(
        functools.partial(_diff_attn_kernel, n_chunks=s // tk, tk=tk, lam_init=lam_init, n_casts=len(side_casts)),
        grid=(b, DA_HEADS, nq),
        in_specs=[
            pl.BlockSpec(memory_space=pltpu.SMEM),
            pl.BlockSpec((1, tq, DA_HEAD), lambda bi, h, i: (bi, i, 2 * h)),
            pl.BlockSpec((1, tq, DA_HEAD), lambda bi, h, i: (bi, i, 2 * h + 1)),
            pl.BlockSpec((1, s, DA_HEAD), lambda bi, h, i: (bi, 0, k_base + 2 * h)),
            pl.BlockSpec((1, s, DA_HEAD), lambda bi, h, i: (bi, 0, k_base + 2 * h + 1)),
            pl.BlockSpec((1, 1, dv, s), lambda bi, h, i: (bi, h, 0, 0)),
            pl.BlockSpec((1, 1, tq), lambda bi, h, i: (bi, 0, i)),
            pl.BlockSpec((1, s, 1), lambda bi, h, i: (bi, 0, 0)),
            pl.BlockSpec((4, DA_HEAD), lambda bi, h, i: (0, 0)),
            pl.BlockSpec((1, dv), lambda bi, h, i: (0, 0)),
        ] + cast_in_specs,
        out_specs=[pl.BlockSpec((1, tq, dv), lambda bi, h, i: (bi, i, h))] + cast_out_specs,
        out_shape=[jax.ShapeDtypeStruct((b, s, DA_HEADS * dv), BF16)] + cast_shapes,
        scratch_shapes=[pltpu.VMEM((rows, s), BF16)],
        compiler_params=_compiler_params(("parallel", "parallel", "arbitrary"), vmem),
        name="diff_attn",
    )(slopes, zq, zq, zq, zq, vt, pos_row, pos_col, lam_vec, g_sub.reshape(1, dv), *side_casts)
    return out[0], out[1:]


def _pack_even_w_in(w_in):
    d = w_in.shape[0]
    sizes = (MLA_Q_RANK, MLA_KV_RANK, MLA_ROPE, ML_HEADS * ML_QK, ML_HEADS * ML_QK, ML_HEADS * ML_V,
             ML_HEADS * ML_V, 4 * ML_HEADS)
    c_q, c_kv, k_r, m_q, m_k, m_v, m_o, m_g = jnp.split(w_in, np.cumsum(sizes)[:-1].tolist(), axis=1)
    pad = lambda w, width: jnp.concatenate([w, jnp.zeros((d, width - w.shape[1]), w.dtype)], axis=1)
    packed = jnp.concatenate([c_q, c_kv, m_v, m_o, m_q, m_k, pad(k_r, LANES), pad(m_g, LANES)], axis=1)
    assert packed.shape[1] == EV_COLS
    return packed


def _pack_w_uq(w_uq):
    r = w_uq.shape[0]
    w = w_uq.reshape(r, MLA_HEADS, MLA_NOPE + MLA_ROPE)
    w = jnp.concatenate([w, jnp.zeros((r, MLA_HEADS, MLA_QK_PAD - MLA_NOPE - MLA_ROPE), w.dtype)], axis=2)
    return w.reshape(r, MLA_HEADS * MLA_QK_PAD).astype(BF16)


def _split_w_ukv(w_ukv):
    r = w_ukv.shape[0]
    w = w_ukv.reshape(r, MLA_HEADS, MLA_NOPE + MLA_V)
    wk = w[:, :, :MLA_NOPE].reshape(r, MLA_HEADS * MLA_NOPE)
    wv_t = w[:, :, MLA_NOPE:].reshape(r, MLA_HEADS * MLA_V).T
    return wk.astype(BF16), wv_t.astype(BF16)


def _rope_tables(positions):
    inv_freq = ROPE_THETA ** (-jnp.arange(0, MLA_ROPE, 2, dtype=F32) / MLA_ROPE)
    ang = positions.astype(F32).reshape(-1, 1) * inv_freq
    cos, sin = jnp.cos(ang), jnp.sin(ang)
    zeros = jnp.zeros((ang.shape[0], LANES - MLA_ROPE), F32)
    return jnp.concatenate([cos, cos, zeros], axis=1), jnp.concatenate([-sin, sin, zeros], axis=1)


def _even_mixer(x, b, s, positions, norm, w_in, g_cq, w_uq, g_ckv, w_ukv, b_gates, g_mlstm, w_o, side_casts):
    n = b * s
    col_scale = jnp.ones((EV_COLS,), F32).at[EV_MK:EV_MK + ML_HEADS * ML_QK].set(ML_QK ** -0.5)
    z = _normproj(x, norm, _pack_even_w_in(w_in.astype(BF16)), col_scale)

    cos_t, sin_t = _rope_tables(positions)
    wk, wv_t = _split_w_ukv(w_ukv)
    q, k, vt = _mla_prep(z, b, s, g_cq, g_ckv, _pack_w_uq(w_uq), wk, wv_t, cos_t, sin_t)
    a_out, (w_o, *casted) = _mla_attention(q.reshape(b, s, -1), k.reshape(b, s, -1), vt, (w_o, *side_casts))

    z3 = z.reshape(b, s, EV_COLS)
    gates_t = z3[:, :, EV_MG:EV_MG + 4 * ML_HEADS].reshape(b, s, 4, ML_HEADS).transpose(0, 3, 2, 1)
    gate_bias = b_gates.astype(F32).reshape(4, ML_HEADS).T.reshape(ML_HEADS, 4, 1)
    m_out = _mlstm(z3, gates_t, gate_bias, g_mlstm)

    split = MLA_HEADS * MLA_V
    return _outproj(x, [a_out.reshape(n, -1), m_out.reshape(n, -1)], [w_o[:split], w_o[split:]]), casted


def _odd_mixer(x, b, s, positions, norm, w_in, lam_q1, lam_k1, lam_q2, lam_k2, g_sub, w_o, lam_init, side_casts):
    n = b * s
    width = DA_HEADS * 2 * DA_HEAD
    zq, vt = _qkv_proj(x, b, s, norm, w_in, LOG2_E * DA_HEAD ** -0.5, DA_HEADS, 2 * DA_HEAD)
    zq = zq.reshape(b, s, 2 * width)
    slopes = jnp.asarray([2.0 ** (-8.0 * (h + 1) / DA_HEADS) for h in range(DA_HEADS)], dtype=F32)
    lam_vec = jnp.stack([lam_q1, lam_k1, lam_q2, lam_k2]).astype(F32)
    pos = positions - positions[:, :1]
    o, casted = _diff_attention(zq, vt, pos.reshape(b, s, 1), pos.reshape(b, 1, s), slopes, lam_vec, g_sub, lam_init,
                                side_casts)
    return _outproj(x, [o.reshape(n, width)], [w_o]), casted


def kernel(x, positions, l0_ffn1_norm, l0_ffn1_w_gu, l0_ffn1_w_down, l0_mix_norm, l0_w_in, l0_g_cq, l0_w_uq, l0_g_ckv, l0_w_ukv, l0_b_gates, l0_g_mlstm, l0_w_o, l0_ffn2_norm, l0_ffn2_w_gu, l0_ffn2_w_down, l1_ffn1_norm, l1_ffn1_w_gu, l1_ffn1_w_down, l1_mix_norm, l1_w_in, l1_lam_q1, l1_lam_k1, l1_lam_q2, l1_lam_k2, l1_g_sub, l1_w_o, l1_ffn2_norm, l1_ffn2_w_gu, l1_ffn2_w_down, final_norm):
    b, s, d = x.shape
    h = x.reshape(b * s, d)
    h = _ffn(h, l0_ffn1_norm, l0_ffn1_w_gu, l0_ffn1_w_down)
    h, (w2_gu, w2_down, w3_gu, w3_down, w_in1, w_o1) = _even_mixer(
        h, b, s, positions, l0_mix_norm, l0_w_in, l0_g_cq, l0_w_uq, l0_g_ckv, l0_w_ukv, l0_b_gates, l0_g_mlstm, l0_w_o,
        side_casts=(l0_ffn2_w_gu, l0_ffn2_w_down, l1_ffn1_w_gu, l1_ffn1_w_down, l1_w_in, l1_w_o))
    h = _ffn(h, l0_ffn2_norm, w2_gu, w2_down)
    h = _ffn(h, l1_ffn1_norm, w3_gu, w3_down)
    lam_init = 0.8 - 0.6 * math.exp(-0.3 * 1)
    h, (w4_gu, w4_down) = _odd_mixer(
        h, b, s, positions, l1_mix_norm, w_in1, l1_lam_q1, l1_lam_k1, l1_lam_q2, l1_lam_k2, l1_g_sub, w_o1, lam_init,
        side_casts=(l1_ffn2_w_gu, l1_ffn2_w_down))
    h = _ffn(h, l1_ffn2_norm, w4_gu, w4_down, final_gain=final_norm)
    return h.reshape(b, s, d)
```
